```python
import math
import jax, jax.numpy as jnp
from jax import lax
import numpy as np

D_MODEL = 1024
BATCH = 8
SEQ = 2048
DEPTH = 1
DEC_BATCH = 128
DEC_SEQ = 8
PAST_LEN = 16384
PAGE_SIZE = 128

N_MEM = 256
CONV_CH = 512
CONV_K = 31
DN_HEADS = 4
DN_DK = 128
DN_DV = 128
DN_QK = DN_HEADS * DN_DK
DN_V = DN_HEADS * DN_DV
QKV_W = 2 * DN_QK + DN_V
SHORT_K = 4
CHUNK = 64
X_HEADS = 4
X_HEAD_DIM = D_MODEL // X_HEADS
D_FF = 4 * D_MODEL
DEEPNORM_ALPHA = (2 * DEPTH) ** 0.25
DEEPNORM_BETA = (8 * DEPTH) ** -0.25
LN_EPS = 1e-5
RMS_EPS = 1e-6
L2_EPS = 1e-6
IN_SIZES = (2 * CONV_CH, QKV_W, DN_V, DN_HEADS, DN_HEADS, D_MODEL, D_MODEL)
IN_OFFSETS = tuple(sum(IN_SIZES[:i + 1]) for i in range(len(IN_SIZES) - 1))
N_IN = sum(IN_SIZES)

kernel_name = 'hybrid_conformer_gdn_decode_step'


def layer_norm(x, g, b):
    xf = x.astype(jnp.float32)
    mu = xf.mean(-1, keepdims=True)
    var = jnp.square(xf - mu).mean(-1, keepdims=True)
    return ((xf - mu) * lax.rsqrt(var + LN_EPS) * g.astype(jnp.float32) + b.astype(jnp.float32)).astype(x.dtype)


def l2_norm(x):
    xf = x.astype(jnp.float32)
    return xf * lax.rsqrt(jnp.sum(xf * xf, -1, keepdims=True) + L2_EPS)


def causal_dwconv(hist, x, w):
    xx = jnp.concatenate([hist.astype(x.dtype), x], axis=1)
    y = lax.conv_general_dilated(xx, w.astype(x.dtype)[:, None, :], (1,), 'VALID',
                                 dimension_numbers=('NWC', 'WIO', 'NWC'),
                                 feature_group_count=x.shape[-1])
    return y, xx[:, xx.shape[1] - (w.shape[0] - 1):]


def to_chunks(t, B, n, C, pad):
    t = t.astype(jnp.float32)
    t = jnp.pad(t, [(0, 0), (0, pad)] + [(0, 0)] * (t.ndim - 2))
    t = t.reshape((B, n, C) + t.shape[2:])
    t = jnp.swapaxes(t, 2, 3)
    return jnp.moveaxis(t, 1, 0)


def gated_delta_rule(q, k, v, beta, logdecay, s0):
    B, L, H, DK = q.shape
    C = min(CHUNK, L)
    n = -(-L // C)
    pad = n * C - L
    qc, kc, vc = (to_chunks(t, B, n, C, pad) for t in (q, k, v))
    bc, gc = (to_chunks(t, B, n, C, pad) for t in (beta, logdecay))
    G = jnp.cumsum(gc, axis=-1)
    idx = jnp.arange(C)
    tril = idx[:, None] >= idx[None, :]
    strict = idx[:, None] > idx[None, :]
    diff = G[..., :, None] - G[..., None, :]
    decay_mat = jnp.where(tril, jnp.exp(jnp.where(tril, diff, 0.0)), 0.0)
    kk = jnp.einsum('nbhid,nbhjd->nbhij', kc, kc)
    A = jnp.where(strict, bc[..., :, None] * kk * decay_mat, 0.0)
    M = A + jnp.eye(C, dtype=jnp.float32)
    eG = jnp.exp(G)
    rhs = jnp.concatenate([(bc * eG)[..., None] * kc, bc[..., None] * vc], axis=-1)
    sol = lax.linalg.triangular_solve(M, rhs, left_side=True, lower=True, unit_diagonal=True)
    Wc, Uv = sol[..., :DK], sol[..., DK:]
    Aqk = jnp.einsum('nbhid,nbhjd->nbhij', qc, kc) * decay_mat
    eGC = jnp.exp(G[..., -1])
    k_tail = kc * jnp.exp(G[..., -1:] - G)[..., None]

    def step(S, xs):
        w_, uv_, q_, aqk_, eg_, egc_, kt_ = xs
        U = uv_ - jnp.einsum('bhck,bhkv->bhcv', w_, S)
        o = eg_[..., None] * jnp.einsum('bhck,bhkv->bhcv', q_, S) + jnp.einsum('bhij,bhjv->bhiv', aqk_, U)
        S = egc_[..., None, None] * S + jnp.einsum('bhck,bhcv->bhkv', kt_, U)
        return S, o

    s_fin, o = lax.scan(step, s0.astype(jnp.float32), (Wc, Uv, qc, Aqk, eG, eGC, k_tail))
    o = jnp.swapaxes(jnp.moveaxis(o, 0, 1), 2, 3).reshape(B, n * C, H, -1)[:, :L]
    return o, s_fin


def hybrid_layer(x, mem_k, mem_v, conv_hist, qkv_hist, s0, p):
    B, L, _ = x.shape
    proj = x @ p['w_in']
    glu, qkv, og, beta_raw, dec_raw, g_a, g_b = jnp.split(proj, IN_OFFSETS, axis=-1)
    glu = glu + p['b_glu']
    u = glu[..., :CONV_CH] * jax.nn.sigmoid(glu[..., CONV_CH:])
    cv, conv_new = causal_dwconv(conv_hist, u, p['w_dw'])
    cv = jax.nn.silu(layer_norm(cv + p['b_dw'], p['ln_conv_g'], p['ln_conv_b']))
    br_a = cv @ p['w_conv_out']
    qkv, qkv_new = causal_dwconv(qkv_hist, qkv, p['w_short'])
    qkv = jax.nn.silu(qkv)
    q, k, v = jnp.split(qkv, (DN_QK, 2 * DN_QK), axis=-1)
    q = l2_norm(q.reshape(B, L, DN_HEADS, DN_DK)) * (DN_DK ** -0.5)
    k = l2_norm(k.reshape(B, L, DN_HEADS, DN_DK))
    v = v.reshape(B, L, DN_HEADS, DN_DV)
    beta = jax.nn.sigmoid(beta_raw.astype(jnp.float32))
    logdecay = -jnp.exp(p['a_log'].astype(jnp.float32)) * jax.nn.softplus(
        dec_raw.astype(jnp.float32) + p['dt_bias'].astype(jnp.float32))
    o, s_new = gated_delta_rule(q, k, v, beta, logdecay, s0)
    o = o * lax.rsqrt(jnp.mean(o * o, -1, keepdims=True) + RMS_EPS) * p['dn_norm_g'].astype(jnp.float32)
    o = o * jax.nn.silu(og.reshape(B, L, DN_HEADS, DN_DV).astype(jnp.float32))
    br_b = o.reshape(B, L, DN_V).astype(x.dtype) @ p['w_dn_out']
    mixed = jax.nn.sigmoid(g_a) * br_a + jax.nn.sigmoid(g_b) * br_b
    h = layer_norm(DEEPNORM_ALPHA * x + mixed @ p['w_o'], p['ln1_g'], p['ln1_b'])
    xq = (h @ p['w_xq']).reshape(B, L, X_HEADS, X_HEAD_DIM)
    sc = jnp.einsum('blhd,bmhd->bhlm', xq, mem_k.astype(xq.dtype)).astype(jnp.float32) * (X_HEAD_DIM ** -0.5)
    pr = jax.nn.softmax(sc, axis=-1).astype(h.dtype)
    xo = jnp.einsum('bhlm,bmhd->blhd', pr, mem_v.astype(h.dtype)).reshape(B, L, D_MODEL) @ p['w_xo']
    h = layer_norm(DEEPNORM_ALPHA * h + xo, p['ln2_g'], p['ln2_b'])
    ff = jnp.square(jax.nn.relu(h @ p['w_ff1'])) @ p['w_ff2']
    y = layer_norm(DEEPNORM_ALPHA * h + ff, p['ln3_g'], p['ln3_b'])
    return y, conv_new, qkv_new, s_new


def setup_inputs(seed: int = 0) -> dict:
    key = jax.random.key(seed)
    ks = iter(jax.random.split(key, 48))

    def nrm(shape, scale):
        return jax.random.normal(next(ks), shape, jnp.float32) * scale

    def gain(n):
        return 1.0 + nrm((DEPTH, n), 0.05)

    dt = jnp.exp(jax.random.uniform(next(ks), (DEPTH, DN_HEADS), jnp.float32,
                                    minval=math.log(1e-3), maxval=math.log(1e-1)))
    w_mem_kv = jnp.concatenate([nrm((DEPTH, D_MODEL, D_MODEL), D_MODEL ** -0.5),
                                nrm((DEPTH, D_MODEL, D_MODEL), D_MODEL ** -0.5 * DEEPNORM_BETA)], axis=-1)
    return {
        'x_prompt': nrm((BATCH, SEQ, D_MODEL), 1.0),
        'x_sample': nrm((DEC_BATCH, DEC_SEQ, D_MODEL), 1.0),
        'mem_prompt': nrm((BATCH, N_MEM, D_MODEL), 1.0),
        'state_conv': nrm((DEPTH, DEC_BATCH, CONV_K - 1, CONV_CH), 0.5),
        'state_qkv_conv': nrm((DEPTH, DEC_BATCH, SHORT_K - 1, QKV_W), 1.0),
        'state_delta': nrm((DEPTH, DEC_BATCH, DN_HEADS, DN_DK, DN_DV), DN_DK ** -0.5),
        'cache_mem_k': nrm((DEPTH, DEC_BATCH, N_MEM, X_HEADS, X_HEAD_DIM), 1.0),
        'cache_mem_v': nrm((DEPTH, DEC_BATCH, N_MEM, X_HEADS, X_HEAD_DIM), DEEPNORM_BETA),
        'w_in': nrm((DEPTH, D_MODEL, N_IN), D_MODEL ** -0.5),
        'b_glu': nrm((DEPTH, 2 * CONV_CH), 0.02),
        'w_dw': nrm((DEPTH, CONV_K, CONV_CH), CONV_K ** -0.5),
        'b_dw': nrm((DEPTH, CONV_CH), 0.02),
        'ln_conv_g': gain(CONV_CH),
        'ln_conv_b': nrm((DEPTH, CONV_CH), 0.02),
        'w_conv_out': nrm((DEPTH, CONV_CH, D_MODEL), CONV_CH ** -0.5 * DEEPNORM_BETA),
        'w_short': nrm((DEPTH, SHORT_K, QKV_W), SHORT_K ** -0.5),
        'a_log': jnp.log(jax.random.uniform(next(ks), (DEPTH, DN_HEADS), jnp.float32, minval=1.0, maxval=16.0)),
        'dt_bias': dt + jnp.log(-jnp.expm1(-dt)),
        'dn_norm_g': gain(DN_DV),
        'w_dn_out': nrm((DEPTH, DN_V, D_MODEL), DN_V ** -0.5 * DEEPNORM_BETA),
        'w_o': nrm((DEPTH, D_MODEL, D_MODEL), D_MODEL ** -0.5 * DEEPNORM_BETA),
        'ln1_g': gain(D_MODEL),
        'ln1_b': nrm((DEPTH, D_MODEL), 0.02),
        'w_xq': nrm((DEPTH, D_MODEL, D_MODEL), D_MODEL ** -0.5),
        'w_mem_kv': w_mem_kv,
        'w_xo': nrm((DEPTH, D_MODEL, D_MODEL), D_MODEL ** -0.5 * DEEPNORM_BETA),
        'ln2_g': gain(D_MODEL),
        'ln2_b': nrm((DEPTH, D_MODEL), 0.02),
        'w_ff1': nrm((DEPTH, D_MODEL, D_FF), D_MODEL ** -0.5 * DEEPNORM_BETA),
        'w_ff2': nrm((DEPTH, D_FF, D_MODEL), D_FF ** -0.5 * DEEPNORM_BETA),
        'ln3_g': gain(D_MODEL),
        'ln3_b': nrm((DEPTH, D_MODEL), 0.02),
    }


def reference(x_prompt, x_sample, mem_prompt, state_conv, state_qkv_conv, state_delta, cache_mem_k, cache_mem_v,
              w_in, b_glu, w_dw, b_dw, ln_conv_g, ln_conv_b, w_conv_out, w_short, a_log, dt_bias, dn_norm_g,
              w_dn_out, w_o, ln1_g, ln1_b, w_xq, w_mem_kv, w_xo, ln2_g, ln2_b, w_ff1, w_ff2, ln3_g, ln3_b):
    B = x_prompt.shape[0]
    h_p, h_s = x_prompt, x_sample
    p_conv, p_qkv, p_delta, p_mk, p_mv = [], [], [], [], []
    s_conv, s_qkv, s_delta = [], [], []
    for l in range(DEPTH):
        p = dict(w_in=w_in[l], b_glu=b_glu[l], w_dw=w_dw[l], b_dw=b_dw[l], ln_conv_g=ln_conv_g[l],
                 ln_conv_b=ln_conv_b[l], w_conv_out=w_conv_out[l], w_short=w_short[l], a_log=a_log[l],
                 dt_bias=dt_bias[l], dn_norm_g=dn_norm_g[l], w_dn_out=w_dn_out[l], w_o=w_o[l],
                 ln1_g=ln1_g[l], ln1_b=ln1_b[l], w_xq=w_xq[l], w_xo=w_xo[l], ln2_g=ln2_g[l], ln2_b=ln2_b[l],
                 w_ff1=w_ff1[l], w_ff2=w_ff2[l], ln3_g=ln3_g[l], ln3_b=ln3_b[l])
        mk, mv = jnp.split(mem_prompt @ w_mem_kv[l], 2, axis=-1)
        mk = mk.reshape(B, N_MEM, X_HEADS, X_HEAD_DIM)
        mv = mv.reshape(B, N_MEM, X_HEADS, X_HEAD_DIM)
        h_p, c_new, q_new, d_new = hybrid_layer(
            h_p, mk, mv,
            jnp.zeros((B, CONV_K - 1, CONV_CH), x_prompt.dtype),
            jnp.zeros((B, SHORT_K - 1, QKV_W), x_prompt.dtype),
            jnp.zeros((B, DN_HEADS, DN_DK, DN_DV), jnp.float32), p)
        p_conv.append(c_new)
        p_qkv.append(q_new)
        p_delta.append(d_new.astype(x_prompt.dtype))
        p_mk.append(mk)
        p_mv.append(mv)
        h_s, c2, q2, d2 = hybrid_layer(h_s, cache_mem_k[l], cache_mem_v[l], state_conv[l],
                                       state_qkv_conv[l], state_delta[l], p)
        s_conv.append(c2.astype(state_conv.dtype))
        s_qkv.append(q2.astype(state_qkv_conv.dtype))
        s_delta.append(d2.astype(state_delta.dtype))
    return (h_p, h_s, jnp.stack(p_conv), jnp.stack(p_qkv), jnp.stack(p_delta), jnp.stack(p_mk), jnp.stack(p_mv),
            jnp.stack(s_conv), jnp.stack(s_qkv), jnp.stack(s_delta))
```

```python
import functools

import jax
import jax.numpy as jnp
from jax import lax
from jax.experimental import pallas as pl
from jax.experimental.pallas import tpu as pltpu

F32 = jnp.float32
BF16 = jnp.bfloat16

D_MODEL = 1024
N_MEM = 256
CONV_CH = 512
CONV_K = 31
DN_HEADS = 4
DN_DK = 128
DN_DV = 128
DN_QK = DN_HEADS * DN_DK
DN_V = DN_HEADS * DN_DV
QKV_W = 2 * DN_QK + DN_V
SHORT_K = 4
CHUNK = 64
X_HEADS = 4
X_HEAD_DIM = D_MODEL // X_HEADS
D_FF = 4 * D_MODEL
DEPTH = 1
DEEPNORM_ALPHA = (2 * DEPTH) ** 0.25
LN_EPS = 1e-5
RMS_EPS = 1e-6
L2_EPS = 1e-6

LANES = 128
N_PACK = 2 * CONV_CH + QKV_W + DN_V + 2 * D_MODEL + LANES
OFF_QKV = 2 * CONV_CH
OFF_OG = OFF_QKV + QKV_W
OFF_GA = OFF_OG + DN_V
OFF_GB = OFF_GA + D_MODEL
OFF_BD = OFF_GB + D_MODEL
VMEM_LIMIT = 56 * 1024 * 1024


def _dot(a, b):
    return jnp.dot(a.astype(BF16), b.astype(BF16), preferred_element_type=F32)


def _dot_nt(a, b):
    return lax.dot_general(a.astype(BF16), b.astype(BF16), (((1,), (1,)), ((), ())),
                           preferred_element_type=F32)


def _sigmoid(x):
    return 1.0 / (1.0 + jnp.exp(-x))


def _silu(x):
    return x * _sigmoid(x)


def _softplus(x):
    return jnp.maximum(x, 0.0) + jnp.log(1.0 + jnp.exp(-jnp.abs(x)))


def _ln(x, g, b):
    mu = jnp.mean(x, -1, keepdims=True)
    xc = x - mu
    var = jnp.mean(xc * xc, -1, keepdims=True)
    return xc * lax.rsqrt(var + LN_EPS) * g + b


def _l2n(x):
    return x * lax.rsqrt(jnp.sum(x * x, -1, keepdims=True) + L2_EPS)


def _wspec(shape):
    return pl.BlockSpec(shape, lambda *_: (0,) * len(shape))


def _params(sem):
    return pltpu.CompilerParams(dimension_semantics=sem, vmem_limit_bytes=VMEM_LIMIT)


def _inproj_kernel(x_ref, w_ref, bglu_ref, u_ref, qkv_ref, og_ref, ga_ref, gb_ref, bd_ref):
    x = x_ref[...].astype(BF16)
    glu = jnp.dot(x, w_ref[:, 0:OFF_QKV], preferred_element_type=F32) + bglu_ref[...]
    u_ref[...] = glu[:, :CONV_CH] * _sigmoid(glu[:, CONV_CH:])
    qkv_ref[...] = jnp.dot(x, w_ref[:, OFF_QKV:OFF_OG], preferred_element_type=F32)
    og_ref[...] = jnp.dot(x, w_ref[:, OFF_OG:OFF_GA], preferred_element_type=F32)
    ga_ref[...] = jnp.dot(x, w_ref[:, OFF_GA:OFF_GB], preferred_element_type=F32)
    gb_ref[...] = jnp.dot(x, w_ref[:, OFF_GB:OFF_BD], preferred_element_type=F32)
    bd_ref[...] = jnp.dot(x, w_ref[:, OFF_BD:N_PACK], preferred_element_type=F32)


def _inproj(x, w_pack, b_glu, tm):
    n = x.shape[0]
    widths = (CONV_CH, QKV_W, DN_V, D_MODEL, D_MODEL, LANES)
    row = lambda w: pl.BlockSpec((tm, w), lambda i: (i, 0))
    return pl.pallas_call(
        _inproj_kernel,
        grid=(n // tm,),
        in_specs=[row(D_MODEL), _wspec((D_MODEL, N_PACK)), _wspec((1, 2 * CONV_CH))],
        out_specs=[row(w) for w in widths],
        out_shape=[jax.ShapeDtypeStruct((n, w), F32) for w in widths],
        compiler_params=_params(("parallel",)),
        name="inproj",
    )(x, w_pack, b_glu)


def _merge_kernel(cvn_ref, on_ref, ga_ref, gb_ref, x_ref, wca_ref, wdn_ref, wo_ref, wxq_ref,
                  g1_ref, b1_ref, h_ref, xq_ref):
    br_a = _dot(cvn_ref[...], wca_ref[...])
    br_b = _dot(on_ref[...], wdn_ref[...])
    mixed = _sigmoid(ga_ref[...]) * br_a + _sigmoid(gb_ref[...]) * br_b
    h = _ln(DEEPNORM_ALPHA * x_ref[...] + _dot(mixed, wo_ref[...]), g1_ref[...], b1_ref[...])
    h_ref[...] = h
    xq_ref[...] = _dot(h, wxq_ref[...])


def _merge(cvn, on, ga, gb, x, wca, wdn, wo, wxq, g1, b1, tm):
    n = x.shape[0]
    row = lambda w: pl.BlockSpec((tm, w), lambda i: (i, 0))
    return pl.pallas_call(
        _merge_kernel,
        grid=(n // tm,),
        in_specs=[row(CONV_CH), row(DN_V), row(D_MODEL), row(D_MODEL), row(D_MODEL),
                  _wspec((CONV_CH, D_MODEL)), _wspec((DN_V, D_MODEL)), _wspec((D_MODEL, D_MODEL)),
                  _wspec((D_MODEL, D_MODEL)), _wspec((1, D_MODEL)), _wspec((1, D_MODEL))],
        out_specs=[row(D_MODEL), row(D_MODEL)],
        out_shape=[jax.ShapeDtypeStruct((n, D_MODEL), F32)] * 2,
        compiler_params=_params(("parallel",)),
        name="merge",
    )(cvn, on, ga, gb, x, wca, wdn, wo, wxq, g1, b1)


FF_BLOCK = 1024


def _tail_kernel(att_ref, h_ref, wxo_ref, w1_ref, w2_ref, g2_ref, b2_ref, g3_ref, b3_ref, y_ref):
    xo = _dot(att_ref[...], wxo_ref[...])
    h2 = _ln(DEEPNORM_ALPHA * h_ref[...] + xo, g2_ref[...], b2_ref[...])
    h2b = h2.astype(BF16)
    ff = jnp.zeros(h2.shape, F32)
    for c in range(D_FF // FF_BLOCK):
        a = jnp.dot(h2b, w1_ref[:, c * FF_BLOCK:(c + 1) * FF_BLOCK], preferred_element_type=F32)
        a = jnp.square(jnp.maximum(a, 0.0))
        ff = ff + jnp.dot(a.astype(BF16), w2_ref[c * FF_BLOCK:(c + 1) * FF_BLOCK, :],
                          preferred_element_type=F32)
    y_ref[...] = _ln(DEEPNORM_ALPHA * h2 + ff, g3_ref[...], b3_ref[...])


def _tail(att, h, wxo, w1, w2, g2, b2, g3, b3, tm):
    n = h.shape[0]
    row = lambda w: pl.BlockSpec((tm, w), lambda i: (i, 0))
    return pl.pallas_call(
        _tail_kernel,
        grid=(n // tm,),
        in_specs=[row(D_MODEL), row(D_MODEL), _wspec((D_MODEL, D_MODEL)), _wspec((D_MODEL, D_FF)),
                  _wspec((D_FF, D_MODEL)), _wspec((1, D_MODEL)), _wspec((1, D_MODEL)),
                  _wspec((1, D_MODEL)), _wspec((1, D_MODEL))],
        out_specs=row(D_MODEL),
        out_shape=jax.ShapeDtypeStruct((n, D_MODEL), F32),
        compiler_params=_params(("parallel",)),
        name="tail",
    )(att, h, wxo, w1, w2, g2, b2, g3, b3)


def _memkv_kernel(m_ref, w_ref, k_ref, v_ref):
    m = m_ref[...].astype(BF16)
    k_ref[...] = jnp.dot(m, w_ref[:, :D_MODEL], preferred_element_type=F32)
    v_ref[...] = jnp.dot(m, w_ref[:, D_MODEL:], preferred_element_type=F32)


def _memkv(mem, w, tm):
    n = mem.shape[0]
    row = pl.BlockSpec((tm, D_MODEL), lambda i: (i, 0))
    return pl.pallas_call(
        _memkv_kernel,
        grid=(n // tm,),
        in_specs=[row, _wspec((D_MODEL, 2 * D_MODEL))],
        out_specs=[row, row],
        out_shape=[jax.ShapeDtypeStruct((n, D_MODEL), F32)] * 2,
        compiler_params=_params(("parallel",)),
        name="memkv",
    )(mem, w)


def _attn_kernel(q_ref, k_ref, v_ref, o_ref, *, nseq):
    scale = X_HEAD_DIM ** -0.5
    for s in range(nseq):
        for hd in range(X_HEADS):
            sl = slice(hd * X_HEAD_DIM, (hd + 1) * X_HEAD_DIM)
            sc = _dot_nt(q_ref[s, :, sl], k_ref[s, :, sl]) * scale
            sc = sc - jnp.max(sc, -1, keepdims=True)
            e = jnp.exp(sc)
            pr = e / jnp.sum(e, -1, keepdims=True)
            o_ref[s, :, sl] = _dot(pr, v_ref[s, :, sl])


def _attn(q, k, v, nseq, tq):
    b, l, _ = q.shape
    return pl.pallas_call(
        functools.partial(_attn_kernel, nseq=nseq),
        grid=(b // nseq, l // tq),
        in_specs=[pl.BlockSpec((nseq, tq, D_MODEL), lambda i, j: (i, j, 0)),
                  pl.BlockSpec((nseq, N_MEM, D_MODEL), lambda i, j: (i, 0, 0)),
                  pl.BlockSpec((nseq, N_MEM, D_MODEL), lambda i, j: (i, 0, 0))],
        out_specs=pl.BlockSpec((nseq, tq, D_MODEL), lambda i, j: (i, j, 0)),
        out_shape=jax.ShapeDtypeStruct((b, l, D_MODEL), F32),
        compiler_params=_params(("parallel", "parallel")),
        name="attn",
    )(q, k, v)


PM_TL = 512
PM_RB = 64
CONV_PAD = 32
SHORT_PAD = 8


def _split3(x):
    h = x.astype(BF16)
    r = x - h.astype(F32)
    m = r.astype(BF16)
    l = (r - m.astype(F32)).astype(BF16)
    return h, m, l


def _dot_exact_lhs(lhs_bf16, x):
    h, m, l = _split3(x)
    d = lambda p: jnp.dot(lhs_bf16, p, preferred_element_type=F32)
    return d(h) + d(m) + d(l)


def _pmid_kernel(u_ref, qkv_ref, og_ref, bd_ref, wdw_ref, bdw_ref, lcg_ref, lcb_ref, wsh_ref,
                 alog_ref, dtb_ref, dng_ref, cvn_ref, on_ref, sout_ref,
                 xx, qx, qs, ks, vs, bs, gs, S):
    t = pl.program_id(1)
    nt = pl.num_programs(1)
    tl = PM_TL

    @pl.when(t == 0)
    def _():
        xx[0:CONV_PAD, :] = jnp.zeros((CONV_PAD, CONV_CH), F32)
        qx[0:SHORT_PAD, :] = jnp.zeros((SHORT_PAD, QKV_W), F32)
        S[...] = jnp.zeros(S.shape, F32)

    xx[CONV_PAD:CONV_PAD + tl, :] = u_ref[...]
    off = CONV_PAD - (CONV_K - 1)
    for rb in range(tl // PM_RB):
        r0 = rb * PM_RB
        acc = jnp.zeros((PM_RB, CONV_CH), F32)
        for j in range(CONV_K):
            acc = acc + wdw_ref[j:j + 1, :] * xx[r0 + j + off:r0 + j + off + PM_RB, :]
        cv = _ln(acc + bdw_ref[...], lcg_ref[...], lcb_ref[...])
        cvn_ref[r0:r0 + PM_RB, :] = _silu(cv)
    xx[0:CONV_PAD, :] = xx[tl:tl + CONV_PAD, :]

    qx[SHORT_PAD:SHORT_PAD + tl, :] = qkv_ref[...]
    offs = SHORT_PAD - (SHORT_K - 1)
    for part, dst in enumerate((qs, ks, vs)):
        c0 = part * DN_QK
        for rb in range(tl // PM_RB):
            r0 = rb * PM_RB
            acc = jnp.zeros((PM_RB, DN_QK), F32)
            for j in range(SHORT_K):
                acc = acc + wsh_ref[j:j + 1, c0:c0 + DN_QK] * qx[r0 + j + offs:r0 + j + offs + PM_RB,
                                                               c0:c0 + DN_QK]
            y = _silu(acc)
            if part < 2:
                for hd in range(DN_HEADS):
                    yh = _l2n(y[:, hd * DN_DK:(hd + 1) * DN_DK])
                    if part == 0:
                        yh = yh * (DN_DK ** -0.5)
                    dst[r0:r0 + PM_RB, hd * DN_DK:(hd + 1) * DN_DK] = yh
            else:
                dst[r0:r0 + PM_RB, :] = y
    qx[0:SHORT_PAD, :] = qx[tl:tl + SHORT_PAD, :]

    bd = bd_ref[...]
    bs[...] = _sigmoid(bd)
    gs[...] = -jnp.exp(alog_ref[...]) * _softplus(bd + dtb_ref[...])

    c = CHUNK
    ri = lax.broadcasted_iota(jnp.int32, (c, c), 0)
    ci = lax.broadcasted_iota(jnp.int32, (c, c), 1)
    tril = ri >= ci
    strict = ri > ci
    tril_b = jnp.where(tril, 1.0, 0.0).astype(BF16)
    upper_f = jnp.where(ri > ci, 1.0, 0.0)
    eye = jnp.where(ri == ci, 1.0, 0.0)

    def chunk_body(ch, carry):
        r0 = pl.multiple_of(ch * c, c)
        beta_c = bs[pl.ds(r0, c), :]
        g_c = gs[pl.ds(r0, c), :]
        for hd in range(DN_HEADS):
            ls = slice(hd * DN_DK, (hd + 1) * DN_DK)
            q = qs[pl.ds(r0, c), ls]
            k = ks[pl.ds(r0, c), ls]
            v = vs[pl.ds(r0, c), ls]
            b_col = beta_c[:, hd:hd + 1]
            g_col = g_c[:, DN_HEADS + hd:DN_HEADS + hd + 1]
            gsum = _dot_exact_lhs(tril_b, jnp.broadcast_to(g_col, (c, DN_DK)))
            diff = _dot_exact_lhs(tril_b, g_col * upper_f)
            decay = jnp.where(tril, jnp.exp(jnp.where(tril, diff, 0.0)), 0.0)
            e_g = jnp.exp(gsum)
            g_last = gsum[c - 1:c, :]
            kq = _dot_nt(jnp.concatenate([k, q], axis=0), k)
            a = jnp.where(strict, b_col * kq[:c] * decay, 0.0)
            aqk = kq[c:] * decay
            p = eye - a
            x = a
            for _ in range(5):
                x = _dot(x, x)
                p = p + _dot(p, x)
            rhs = jnp.concatenate([(b_col * e_g) * k, b_col * v], axis=1)
            sol = _dot(p, rhs)
            w_ = sol[:, :DN_DK]
            uv = sol[:, DN_DK:]
            s_h = S[hd]
            wq_s = _dot(jnp.concatenate([w_, q], axis=0), s_h)
            u_new = uv - wq_s[:c]
            o = e_g * wq_s[c:] + _dot(aqk, u_new)
            k_tail = k * jnp.exp(g_last - gsum)
            S[hd] = jnp.exp(g_last) * s_h + _dot(k_tail.T, u_new)
            o = o * lax.rsqrt(jnp.mean(o * o, -1, keepdims=True) + RMS_EPS) * dng_ref[...]
            on_ref[pl.ds(r0, c), ls] = o * _silu(og_ref[pl.ds(r0, c), ls])
        return carry

    lax.fori_loop(0, tl // c, chunk_body, 0)

    @pl.when(t == nt - 1)
    def _():
        sout_ref[0] = S[...]


def _pmid(u, qkv, og, bd, wdw, bdw, lcg, lcb, wsh, alog, dtb, dng, batch, seq):
    tl = PM_TL
    nt = seq // tl
    n = batch * seq
    row = lambda w: pl.BlockSpec((tl, w), lambda b, t: (b * nt + t, 0))
    return pl.pallas_call(
        _pmid_kernel,
        grid=(batch, nt),
        in_specs=[row(CONV_CH), row(QKV_W), row(DN_V), row(LANES),
                  _wspec((CONV_PAD, CONV_CH)), _wspec((1, CONV_CH)), _wspec((1, CONV_CH)),
                  _wspec((1, CONV_CH)), _wspec((SHORT_K, QKV_W)), _wspec((1, LANES)),
                  _wspec((1, LANES)), _wspec((1, DN_DV))],
        out_specs=[row(CONV_CH), row(DN_V),
                   pl.BlockSpec((1, DN_HEADS, DN_DK, DN_DV), lambda b, t: (b, 0, 0, 0))],
        out_shape=[jax.ShapeDtypeStruct((n, CONV_CH), F32), jax.ShapeDtypeStruct((n, DN_V), F32),
                   jax.ShapeDtypeStruct((batch, DN_HEADS, DN_DK, DN_DV), F32)],
        scratch_shapes=[pltpu.VMEM((tl + CONV_PAD, CONV_CH), F32),
                        pltpu.VMEM((tl + SHORT_PAD, QKV_W), F32),
                        pltpu.VMEM((tl, DN_QK), F32), pltpu.VMEM((tl, DN_QK), F32),
                        pltpu.VMEM((tl, DN_V), F32), pltpu.VMEM((tl, LANES), F32),
                        pltpu.VMEM((tl, LANES), F32),
                        pltpu.VMEM((DN_HEADS, DN_DK, DN_DV), F32)],
        compiler_params=_params(("parallel", "arbitrary")),
        name="prompt_mid",
    )(u, qkv, og, bd, wdw, bdw, lcg, lcb, wsh, alog, dtb, dng)


SM_BT = 32
SM_BS = 8


def _smid1_kernel(u_ref, qkv_ref, bd_ref, hc_ref, hq_ref, wdw_ref, bdw_ref, lcg_ref, lcb_ref, wsh_ref,
                  alog_ref, dtb_ref,
                  cvn_ref, cnew_ref, qnew_ref, w_ref, q2_ref, uv_ref, o1_ref, kt_ref, egc_ref):
    nl = u_ref.shape[0]
    nh = CONV_K - 1
    xx = [hc_ref[i] for i in range(nh)] + [u_ref[i] for i in range(nl)]
    for t in range(nl):
        acc = wdw_ref[0:1, :] * xx[t]
        for j in range(1, CONV_K):
            acc = acc + wdw_ref[j:j + 1, :] * xx[t + j]
        cvn_ref[t] = _silu(_ln(acc + bdw_ref[...], lcg_ref[...], lcb_ref[...]))
    for i in range(nh):
        cnew_ref[i] = xx[nl + i]

    ns = SHORT_K - 1
    qq = [hq_ref[i] for i in range(ns)] + [qkv_ref[i] for i in range(nl)]
    for i in range(ns):
        qnew_ref[i] = qq[nl + i]
    qkv = []
    for t in range(nl):
        acc = wsh_ref[0:1, :] * qq[t]
        for j in range(1, SHORT_K):
            acc = acc + wsh_ref[j:j + 1, :] * qq[t + j]
        qkv.append(_silu(acc))

    beta = [_sigmoid(bd_ref[t]) for t in range(nl)]
    ld = [-jnp.exp(alog_ref[...]) * _softplus(bd_ref[t] + dtb_ref[...]) for t in range(nl)]

    for hd in range(DN_HEADS):
        ls = slice(hd * DN_DK, (hd + 1) * DN_DK)
        q = [_l2n(qkv[t][:, hd * DN_DK:(hd + 1) * DN_DK]) * (DN_DK ** -0.5) for t in range(nl)]
        k = [_l2n(qkv[t][:, DN_QK + hd * DN_DK:DN_QK + (hd + 1) * DN_DK]) for t in range(nl)]
        v = [qkv[t][:, 2 * DN_QK + hd * DN_DV:2 * DN_QK + (hd + 1) * DN_DV] for t in range(nl)]
        b = [beta[t][:, hd:hd + 1] for t in range(nl)]
        g = [ld[t][:, DN_HEADS + hd:DN_HEADS + hd + 1] for t in range(nl)]
        gc = [g[0]]
        for t in range(1, nl):
            gc.append(gc[t - 1] + g[t])
        w_l, uv_l = [], []
        for i in range(nl):
            wi = (b[i] * jnp.exp(gc[i])) * k[i]
            ui = b[i] * v[i]
            for j in range(i):
                a_ij = b[i] * jnp.sum(k[i] * k[j], -1, keepdims=True) * jnp.exp(gc[i] - gc[j])
                wi = wi - a_ij * w_l[j]
                ui = ui - a_ij * uv_l[j]
            w_l.append(wi)
            uv_l.append(ui)
        for i in range(nl):
            o1 = jnp.zeros_like(v[i])
            q2 = jnp.exp(gc[i]) * q[i]
            for j in range(i + 1):
                aqk = jnp.sum(q[i] * k[j], -1, keepdims=True) * jnp.exp(gc[i] - gc[j])
                o1 = o1 + aqk * uv_l[j]
                q2 = q2 - aqk * w_l[j]
            w_ref[i, :, ls] = w_l[i]
            uv_ref[i, :, ls] = uv_l[i]
            o1_ref[i, :, ls] = o1
            q2_ref[i, :, ls] = q2
            kt_ref[i, :, ls] = k[i] * jnp.exp(gc[nl - 1] - gc[i])
        egc_ref[hd] = jnp.broadcast_to(jnp.exp(gc[nl - 1]), (u_ref.shape[1], LANES))


def _smid1(u, qkv, bd, hc, hq, wdw, bdw, lcg, lcb, wsh, alog, dtb):
    nl, b, _ = u.shape
    bt = SM_BT
    tm = lambda r, w: pl.BlockSpec((r, bt, w), lambda i: (0, i, 0))
    outs = [(nl, CONV_CH), (CONV_K - 1, CONV_CH), (SHORT_K - 1, QKV_W), (nl, DN_QK), (nl, DN_QK),
            (nl, DN_V), (nl, DN_V), (nl, DN_QK), (DN_HEADS, LANES)]
    return pl.pallas_call(
        _smid1_kernel,
        grid=(b // bt,),
        in_specs=[tm(nl, CONV_CH), tm(nl, QKV_W), tm(nl, LANES), tm(CONV_K - 1, CONV_CH),
                  tm(SHORT_K - 1, QKV_W),
                  _wspec((CONV_PAD, CONV_CH)), _wspec((1, CONV_CH)), _wspec((1, CONV_CH)),
                  _wspec((1, CONV_CH)), _wspec((SHORT_K, QKV_W)), _wspec((1, LANES)),
                  _wspec((1, LANES))],
        out_specs=[tm(r, w) for r, w in outs],
        out_shape=[jax.ShapeDtypeStruct((r, b, w), F32) for r, w in outs],
        compiler_params=_params(("parallel",)),
        name="sample_mid1",
    )(u, qkv, bd, hc, hq, wdw, bdw, lcg, lcb, wsh, alog, dtb)


def _smid2_kernel(w_ref, q2_ref, uv_ref, o1_ref, kt_ref, og_ref, egc_ref, s_ref, dng_ref,
                  on_ref, snew_ref):
    nl = w_ref.shape[1]
    for hd in range(DN_HEADS):
        ls = slice(hd * DN_DK, (hd + 1) * DN_DK)
        s_h = s_ref[:, hd]
        lhs = jnp.concatenate([w_ref[:, :, ls], q2_ref[:, :, ls]], axis=1)
        r = jnp.einsum('bck,bkv->bcv', lhs.astype(BF16), s_h.astype(BF16),
                       preferred_element_type=F32)
        u_new = uv_ref[:, :, ls] - r[:, :nl]
        o = o1_ref[:, :, ls] + r[:, nl:]
        upd = jnp.einsum('bck,bcv->bkv', kt_ref[:, :, ls].astype(BF16), u_new.astype(BF16),
                         preferred_element_type=F32)
        snew_ref[:, hd] = egc_ref[:, hd] * s_h + upd
        o = o * lax.rsqrt(jnp.mean(o * o, -1, keepdims=True) + RMS_EPS) * dng_ref[...]
        on_ref[:, :, ls] = o * _silu(og_ref[:, :, ls])


def _smid2(w, q2, uv, o1, kt, og, egc, s, dng):
    b, nl, _ = w.shape
    bs = SM_BS
    sq = pl.BlockSpec((bs, nl, DN_V), lambda i: (i, 0, 0))
    st = pl.BlockSpec((bs, DN_HEADS, DN_DK, DN_DV), lambda i: (i, 0, 0, 0))
    return pl.pallas_call(
        _smid2_kernel,
        grid=(b // bs,),
        in_specs=[sq, sq, sq, sq, sq, sq,
                  pl.BlockSpec((bs, DN_HEADS, 1, LANES), lambda i: (i, 0, 0, 0)), st,
                  _wspec((1, DN_DV))],
        out_specs=[sq, st],
        out_shape=[jax.ShapeDtypeStruct((b, nl, DN_V), F32),
                   jax.ShapeDtypeStruct((b, DN_HEADS, DN_DK, DN_DV), F32)],
        compiler_params=_params(("parallel",)),
        name="sample_mid2",
    )(w, q2, uv, o1, kt, og, egc, s, dng)


def _tm(x):
    return jnp.swapaxes(x, 0, 1)


def kernel(x_prompt, x_sample, mem_prompt, state_conv, state_qkv_conv, state_delta, cache_mem_k, cache_mem_v, w_in, b_glu, w_dw, b_dw, ln_conv_g, ln_conv_b, w_conv_out, w_short, a_log, dt_bias, dn_norm_g, w_dn_out, w_o, ln1_g, ln1_b, w_xq, w_mem_kv, w_xo, ln2_g, ln2_b, w_ff1, w_ff2, ln3_g, ln3_b):
    assert w_in.shape[0] == DEPTH == 1
    bp, lp, _ = x_prompt.shape
    bsm, ls_, _ = x_sample.shape

    w = w_in[0]
    o = (2 * CONV_CH, 2 * CONV_CH + QKV_W, 2 * CONV_CH + QKV_W + DN_V)
    o_beta, o_dec, o_ga = o[2], o[2] + DN_HEADS, o[2] + 2 * DN_HEADS
    w_pack = jnp.concatenate(
        [w[:, :o[2]], w[:, o_ga:o_ga + 2 * D_MODEL], w[:, o_beta:o_ga],
         jnp.zeros((D_MODEL, LANES - 2 * DN_HEADS), w.dtype)], axis=1).astype(BF16)
    lane_pad = lambda a: jnp.concatenate(
        [jnp.zeros((DN_HEADS,), F32), a.astype(F32), jnp.zeros((LANES - 2 * DN_HEADS,), F32)])[None]
    alog = lane_pad(a_log[0])
    dtb = lane_pad(dt_bias[0])
    wdw = jnp.concatenate([w_dw[0], jnp.zeros((CONV_PAD - CONV_K, CONV_CH), F32)], axis=0)
    r2 = lambda a: a[0][None]
    bglu, bdw, lcg, lcb = r2(b_glu), r2(b_dw), r2(ln_conv_g), r2(ln_conv_b)
    dng = r2(dn_norm_g)
    wsh = w_short[0]
    wca, wdn, wo, wxq, wxo = (a[0].astype(BF16) for a in (w_conv_out, w_dn_out, w_o, w_xq, w_xo))
    w1, w2, wkv = w_ff1[0].astype(BF16), w_ff2[0].astype(BF16), w_mem_kv[0].astype(BF16)
    g1, b1, g2, b2, g3, b3 = (r2(a) for a in (ln1_g, ln1_b, ln2_g, ln2_b, ln3_g, ln3_b))

    n_p = bp * lp
    xp = x_prompt.reshape(n_p, D_MODEL)
    mk, mv = _memkv(mem_prompt.reshape(bp * N_MEM, D_MODEL), wkv, 512)
    u, qkv, og, ga, gb, bd = _inproj(xp, w_pack, bglu, 256)
    cvn, on, p_delta = _pmid(u, qkv, og, bd, wdw, bdw, lcg, lcb, wsh, alog, dtb, dng, bp, lp)
    h, xq = _merge(cvn, on, ga, gb, xp, wca, wdn, wo, wxq, g1, b1, 512)
    att = _attn(xq.reshape(bp, lp, D_MODEL), mk.reshape(bp, N_MEM, D_MODEL),
                mv.reshape(bp, N_MEM, D_MODEL), 1, 512)
    y_p = _tail(att.reshape(n_p, D_MODEL), h, wxo, w1, w2, g2, b2, g3, b3, 512).reshape(bp, lp, D_MODEL)
    p_conv = u.reshape(bp, lp, CONV_CH)[:, lp - (CONV_K - 1):]
    p_qkv = qkv.reshape(bp, lp, QKV_W)[:, lp - (SHORT_K - 1):]
    p_mk = mk.reshape(bp, N_MEM, X_HEADS, X_HEAD_DIM)
    p_mv = mv.reshape(bp, N_MEM, X_HEADS, X_HEAD_DIM)

    n_s = bsm * ls_
    xs = _tm(x_sample).reshape(n_s, D_MODEL)
    u, qkv, og, ga, gb, bd = _inproj(xs, w_pack, bglu, 256)
    t3 = lambda a: a.reshape(ls_, bsm, a.shape[-1])
    cvn, c_new, q_new, w_, q2, uv, o1, kt, egc = _smid1(
        t3(u), t3(qkv), t3(bd), _tm(state_conv[0]), _tm(state_qkv_conv[0]),
        wdw, bdw, lcg, lcb, wsh, alog, dtb)
    on, s_delta = _smid2(_tm(w_), _tm(q2), _tm(uv), _tm(o1), _tm(kt), _tm(t3(og)),
                         _tm(egc)[:, :, None, :], state_delta[0], dng)
    h, xq = _merge(cvn.reshape(n_s, CONV_CH), _tm(on).reshape(n_s, DN_V), ga, gb, xs,
                   wca, wdn, wo, wxq, g1, b1, 512)
    att = _attn(_tm(t3(xq)), cache_mem_k[0].reshape(bsm, N_MEM, D_MODEL),
                cache_mem_v[0].reshape(bsm, N_MEM, D_MODEL), 4, ls_)
    y_s = _tail(_tm(att).reshape(n_s, D_MODEL), h, wxo, w1, w2, g2, b2, g3, b3, 512)
    y_s = _tm(t3(y_s))

    return (y_p, y_s, p_conv[None], p_qkv[None], p_delta[None], p_mk[None], p_mv[None],
            _tm(c_new)[None], _tm(q_new)[None], s_delta[None])
```

```python
import functools

import jax
import jax.numpy as jnp
from jax import lax
from jax.experimental import pallas as pl
from jax.experimental.pallas import tpu as pltpu

F32 = jnp.float32
BF16 = jnp.bfloat16

D_MODEL = 1024
N_MEM = 256
CONV_CH = 512
CONV_K = 31
DN_HEADS = 4
DN_DK = 128
DN_DV = 128
DN_QK = DN_HEADS * DN_DK
DN_V = DN_HEADS * DN_DV
QKV_W = 2 * DN_QK + DN_V
SHORT_K = 4
CHUNK = 64
X_HEADS = 4
X_HEAD_DIM = D_MODEL // X_HEADS
D_FF = 4 * D_MODEL
DEPTH = 1
DEEPNORM_ALPHA = (2 * DEPTH) ** 0.25
LN_EPS = 1e-5
RMS_EPS = 1e-6
L2_EPS = 1e-6

LANES = 128
N_PACK = 2 * CONV_CH + QKV_W + DN_V + 2 * D_MODEL + LANES
OFF_QKV = 2 * CONV_CH
OFF_OG = OFF_QKV + QKV_W
OFF_GA = OFF_OG + DN_V
OFF_GB = OFF_GA + D_MODEL
OFF_BD = OFF_GB + D_MODEL
VMEM_LIMIT = 56 * 1024 * 1024


def _dot(a, b):
    return jnp.dot(a.astype(BF16), b.astype(BF16), preferred_element_type=F32)


def _dot_nt(a, b):
    return lax.dot_general(a.astype(BF16), b.astype(BF16), (((1,), (1,)), ((), ())),
                           preferred_element_type=F32)


def _sigmoid(x):
    return 1.0 / (1.0 + jnp.exp(-x))


def _silu(x):
    return x * _sigmoid(x)


def _softplus(x):
    return jnp.maximum(x, 0.0) + jnp.log(1.0 + jnp.exp(-jnp.abs(x)))


def _ln(x, g, b):
    mu = jnp.mean(x, -1, keepdims=True)
    xc = x - mu
    var = jnp.mean(xc * xc, -1, keepdims=True)
    return xc * lax.rsqrt(var + LN_EPS) * g + b


def _l2n(x):
    return x * lax.rsqrt(jnp.sum(x * x, -1, keepdims=True) + L2_EPS)


def _wspec(shape):
    return pl.BlockSpec(shape, lambda *_: (0,) * len(shape))


def _params(sem):
    return pltpu.CompilerParams(dimension_semantics=sem, vmem_limit_bytes=VMEM_LIMIT)


def _inproj_kernel(x_ref, w_ref, bglu_ref, u_ref, qkv_ref, og_ref, ga_ref, gb_ref, bd_ref):
    x = x_ref[...].astype(BF16)
    glu = jnp.dot(x, w_ref[:, 0:OFF_QKV], preferred_element_type=F32) + bglu_ref[...]
    u_ref[...] = glu[:, :CONV_CH] * _sigmoid(glu[:, CONV_CH:])
    qkv_ref[...] = jnp.dot(x, w_ref[:, OFF_QKV:OFF_OG], preferred_element_type=F32)
    og_ref[...] = jnp.dot(x, w_ref[:, OFF_OG:OFF_GA], preferred_element_type=F32)
    ga_ref[...] = jnp.dot(x, w_ref[:, OFF_GA:OFF_GB], preferred_element_type=F32)
    gb_ref[...] = jnp.dot(x, w_ref[:, OFF_GB:OFF_BD], preferred_element_type=F32)
    bd_ref[...] = jnp.dot(x, w_ref[:, OFF_BD:N_PACK], preferred_element_type=F32)


def _inproj(x, w_pack, b_glu, tm):
    n = x.shape[0]
    widths = (CONV_CH, QKV_W, DN_V, D_MODEL, D_MODEL, LANES)
    row = lambda w: pl.BlockSpec((tm, w), lambda i: (i, 0))
    return pl.pallas_call(
        _inproj_kernel,
        grid=(n // tm,),
        in_specs=[row(D_MODEL), _wspec((D_MODEL, N_PACK)), _wspec((1, 2 * CONV_CH))],
        out_specs=[row(w) for w in widths],
        out_shape=[jax.ShapeDtypeStruct((n, w), F32) for w in widths],
        compiler_params=_params(("parallel",)),
        name="inproj",
    )(x, w_pack, b_glu)


def _merge_kernel(cvn_ref, on_ref, ga_ref, gb_ref, x_ref, wca_ref, wdn_ref, wo_ref, wxq_ref,
                  g1_ref, b1_ref, h_ref, xq_ref):
    br_a = _dot(cvn_ref[...], wca_ref[...])
    br_b = _dot(on_ref[...], wdn_ref[...])
    mixed = _sigmoid(ga_ref[...]) * br_a + _sigmoid(gb_ref[...]) * br_b
    h = _ln(DEEPNORM_ALPHA * x_ref[...] + _dot(mixed, wo_ref[...]), g1_ref[...], b1_ref[...])
    h_ref[...] = h
    xq_ref[...] = _dot(h, wxq_ref[...])


def _merge(cvn, on, ga, gb, x, wca, wdn, wo, wxq, g1, b1, tm):
    n = x.shape[0]
    row = lambda w: pl.BlockSpec((tm, w), lambda i: (i, 0))
    return pl.pallas_call(
        _merge_kernel,
        grid=(n // tm,),
        in_specs=[row(CONV_CH), row(DN_V), row(D_MODEL), row(D_MODEL), row(D_MODEL),
                  _wspec((CONV_CH, D_MODEL)), _wspec((DN_V, D_MODEL)), _wspec((D_MODEL, D_MODEL)),
                  _wspec((D_MODEL, D_MODEL)), _wspec((1, D_MODEL)), _wspec((1, D_MODEL))],
        out_specs=[row(D_MODEL), row(D_MODEL)],
        out_shape=[jax.ShapeDtypeStruct((n, D_MODEL), F32)] * 2,
        compiler_params=_params(("parallel",)),
        name="merge",
    )(cvn, on, ga, gb, x, wca, wdn, wo, wxq, g1, b1)


FF_BLOCK = 1024


def _tail_kernel(att_ref, h_ref, wxo_ref, w1_ref, w2_ref, g2_ref, b2_ref, g3_ref, b3_ref, y_ref):
    xo = _dot(att_ref[...], wxo_ref[...])
    h2 = _ln(DEEPNORM_ALPHA * h_ref[...] + xo, g2_ref[...], b2_ref[...])
    h2b = h2.astype(BF16)
    ff = jnp.zeros(h2.shape, F32)
    for c in range(D_FF // FF_BLOCK):
        a = jnp.dot(h2b, w1_ref[:, c * FF_BLOCK:(c + 1) * FF_BLOCK], preferred_element_type=F32)
        a = jnp.square(jnp.maximum(a, 0.0))
        ff = ff + jnp.dot(a.astype(BF16), w2_ref[c * FF_BLOCK:(c + 1) * FF_BLOCK, :],
                          preferred_element_type=F32)
    y_ref[...] = _ln(DEEPNORM_ALPHA * h2 + ff, g3_ref[...], b3_ref[...])


def _tail(att, h, wxo, w1, w2, g2, b2, g3, b3, tm):
    n = h.shape[0]
    row = lambda w: pl.BlockSpec((tm, w), lambda i: (i, 0))
    return pl.pallas_call(
        _tail_kernel,
        grid=(n // tm,),
        in_specs=[row(D_MODEL), row(D_MODEL), _wspec((D_MODEL, D_MODEL)), _wspec((D_MODEL, D_FF)),
                  _wspec((D_FF, D_MODEL)), _wspec((1, D_MODEL)), _wspec((1, D_MODEL)),
                  _wspec((1, D_MODEL)), _wspec((1, D_MODEL))],
        out_specs=row(D_MODEL),
        out_shape=jax.ShapeDtypeStruct((n, D_MODEL), F32),
        compiler_params=_params(("parallel",)),
        name="tail",
    )(att, h, wxo, w1, w2, g2, b2, g3, b3)


def _memkv_kernel(m_ref, w_ref, k_ref, v_ref):
    m = m_ref[...].astype(BF16)
    k_ref[...] = jnp.dot(m, w_ref[:, :D_MODEL], preferred_element_type=F32)
    v_ref[...] = jnp.dot(m, w_ref[:, D_MODEL:], preferred_element_type=F32)


def _memkv(mem, w, tm):
    n = mem.shape[0]
    row = pl.BlockSpec((tm, D_MODEL), lambda i: (i, 0))
    return pl.pallas_call(
        _memkv_kernel,
        grid=(n // tm,),
        in_specs=[row, _wspec((D_MODEL, 2 * D_MODEL))],
        out_specs=[row, row],
        out_shape=[jax.ShapeDtypeStruct((n, D_MODEL), F32)] * 2,
        compiler_params=_params(("parallel",)),
        name="memkv",
    )(mem, w)


def _attn_kernel(q_ref, k_ref, v_ref, o_ref, *, nseq, split):
    scale = X_HEAD_DIM ** -0.5
    for s in range(nseq):
        for hd in range(X_HEADS):
            sl = slice(hd * X_HEAD_DIM, (hd + 1) * X_HEAD_DIM)
            k = k_ref[s, :, hd, :] if split else k_ref[s, :, sl]
            v = v_ref[s, :, hd, :] if split else v_ref[s, :, sl]
            sc = _dot_nt(q_ref[s, :, sl], k) * scale
            sc = sc - jnp.max(sc, -1, keepdims=True)
            e = jnp.exp(sc)
            pr = e / jnp.sum(e, -1, keepdims=True)
            o_ref[s, :, sl] = _dot(pr, v)


def _attn(q, k, v, nseq, tq):
    b, l, _ = q.shape
    split = k.ndim == 5
    if split:
        kv_spec = pl.BlockSpec((None, nseq, N_MEM, X_HEADS, X_HEAD_DIM), lambda i, j: (0, i, 0, 0, 0))
    else:
        kv_spec = pl.BlockSpec((nseq, N_MEM, D_MODEL), lambda i, j: (i, 0, 0))
    return pl.pallas_call(
        functools.partial(_attn_kernel, nseq=nseq, split=split),
        grid=(b // nseq, l // tq),
        in_specs=[pl.BlockSpec((nseq, tq, D_MODEL), lambda i, j: (i, j, 0)), kv_spec, kv_spec],
        out_specs=pl.BlockSpec((nseq, tq, D_MODEL), lambda i, j: (i, j, 0)),
        out_shape=jax.ShapeDtypeStruct((b, l, D_MODEL), F32),
        compiler_params=_params(("parallel", "parallel")),
        name="attn",
    )(q, k, v)


PM_TL = 512
PM_RB = 64
CONV_PAD = 32
SHORT_PAD = 8


def _split3(x):
    h = x.astype(BF16)
    r = x - h.astype(F32)
    m = r.astype(BF16)
    l = (r - m.astype(F32)).astype(BF16)
    return h, m, l


def _dot_exact_lhs(lhs_bf16, x):
    h, m, l = _split3(x)
    d = lambda p: jnp.dot(lhs_bf16, p, preferred_element_type=F32)
    return d(h) + d(m) + d(l)


def _bmm(a, b):
    return jnp.einsum('bij,bjk->bik', a.astype(BF16), b.astype(BF16), preferred_element_type=F32)


def _bmm_nt(a, b):
    return jnp.einsum('bik,bjk->bij', a.astype(BF16), b.astype(BF16), preferred_element_type=F32)


def _bmm_tn(a, b):
    return jnp.einsum('bki,bkj->bij', a.astype(BF16), b.astype(BF16), preferred_element_type=F32)


def _causal_taps(src_ref, w_ref, r0, rows, ntaps, off, c0, width):
    acc = None
    for ph in range(8):
        taps = [j for j in range(ntaps) if (j + off) % 8 == ph]
        if not taps:
            continue
        n = rows + (8 if ph else 0)
        z = None
        for j in taps:
            a8 = r0 + j + off - ph
            term = w_ref[j:j + 1, c0:c0 + width] * src_ref[a8:a8 + n, c0:c0 + width]
            z = term if z is None else z + term
        if ph:
            z = pltpu.roll(z, n - ph, axis=0)[0:rows]
        acc = z if acc is None else acc + z
    return acc


def _pmid_kernel(u_ref, qkv_ref, og_ref, bd_ref, wdw_ref, bdw_ref, lcg_ref, lcb_ref, wsh_ref,
                 alog_ref, dtb_ref, dng_ref, cvn_ref, on_ref, sout_ref,
                 xx, qx, qs, ks, vs, S, ktw_s, n_s, q2_s, o1_s, sl_s, egc_s):
    t = pl.program_id(1)
    nt = pl.num_programs(1)
    tl = PM_TL
    c = CHUNK
    nc = tl // c

    @pl.when(t == 0)
    def _():
        xx[0:CONV_PAD, :] = jnp.zeros((CONV_PAD, CONV_CH), F32)
        qx[0:SHORT_PAD, :] = jnp.zeros((SHORT_PAD, QKV_W), F32)
        S[...] = jnp.zeros(S.shape, F32)

    xx[CONV_PAD:CONV_PAD + tl, :] = u_ref[...]
    for rb in range(tl // PM_RB):
        r0 = rb * PM_RB
        acc = _causal_taps(xx, wdw_ref, r0, PM_RB, CONV_K, CONV_PAD - (CONV_K - 1), 0, CONV_CH)
        cv = _ln(acc + bdw_ref[...], lcg_ref[...], lcb_ref[...])
        cvn_ref[r0:r0 + PM_RB, :] = _silu(cv)
    xx[0:CONV_PAD, :] = xx[tl:tl + CONV_PAD, :]

    qx[SHORT_PAD:SHORT_PAD + tl, :] = qkv_ref[...]
    for part, dst in enumerate((qs, ks, vs)):
        c0 = part * DN_QK
        for rb in range(tl // PM_RB):
            r0 = rb * PM_RB
            acc = _causal_taps(qx, wsh_ref, r0, PM_RB, SHORT_K, SHORT_PAD - (SHORT_K - 1), c0, DN_QK)
            y = _silu(acc)
            if part < 2:
                for hd in range(DN_HEADS):
                    yh = _l2n(y[:, hd * DN_DK:(hd + 1) * DN_DK])
                    if part == 0:
                        yh = yh * (DN_DK ** -0.5)
                    dst[r0:r0 + PM_RB, hd * DN_DK:(hd + 1) * DN_DK] = yh
            else:
                dst[r0:r0 + PM_RB, :] = y
    qx[0:SHORT_PAD, :] = qx[tl:tl + SHORT_PAD, :]

    bd = bd_ref[...]
    beta = _sigmoid(bd)
    ld = -jnp.exp(alog_ref[...]) * _softplus(bd + dtb_ref[...])
    rt = lax.broadcasted_iota(jnp.int32, (tl, tl), 0)
    ct = lax.broadcasted_iota(jnp.int32, (tl, tl), 1)
    blk_tril = jnp.where((rt >= ct) & ((rt // c) == (ct // c)), 1.0, 0.0).astype(BF16)
    gsum = _dot_exact_lhs(blk_tril, ld)
    beta3 = beta.reshape(nc, c, LANES)
    ld3 = ld.reshape(nc, c, LANES)
    gsum3 = gsum.reshape(nc, c, LANES)
    glast3 = gsum3[:, c - 1:c, :]
    eg3 = jnp.exp(gsum3)
    etail3 = jnp.exp(glast3 - gsum3)
    egc3 = jnp.exp(glast3)

    ri = lax.broadcasted_iota(jnp.int32, (c, c), 0)
    ci = lax.broadcasted_iota(jnp.int32, (c, c), 1)
    tril = (ri >= ci)[None]
    strict = (ri > ci)[None]
    tril_b = jnp.broadcast_to(jnp.where(ri >= ci, 1.0, 0.0).astype(BF16)[None], (nc, c, c))
    upper_f = jnp.where(ri > ci, 1.0, 0.0)[None]
    eye = jnp.where(ri == ci, 1.0, 0.0)[None]

    for hd in range(DN_HEADS):
        ls = slice(hd * DN_DK, (hd + 1) * DN_DK)
        lb = slice(hd, hd + 1)
        lg = slice(DN_HEADS + hd, DN_HEADS + hd + 1)
        q = qs[:, ls].reshape(nc, c, DN_DK)
        k = ks[:, ls].reshape(nc, c, DN_DK)
        v = vs[:, ls].reshape(nc, c, DN_DV)
        b_col = beta3[:, :, lb]
        h3, m3, l3 = _split3(ld3[:, :, lg] * upper_f)
        diff = _bmm(tril_b, h3) + _bmm(tril_b, m3) + _bmm(tril_b, l3)
        decay = jnp.where(tril, jnp.exp(jnp.where(tril, diff, 0.0)), 0.0)
        kq = _bmm_nt(jnp.concatenate([k, q], axis=1), k)
        a = jnp.where(strict, b_col * kq[:, :c] * decay, 0.0)
        aqk = kq[:, c:] * decay
        p = eye - a
        x = a
        for _ in range(5):
            x = _bmm(x, x)
            p = p + _bmm(p, x)
        rhs = jnp.concatenate([(b_col * eg3[:, :, lg]) * k, b_col * v], axis=2)
        sol = _bmm(p, rhs)
        qo = _bmm(aqk, sol)
        q2_s[hd] = (eg3[:, :, lg] * q - qo[:, :, :DN_DK]).astype(BF16)
        o1_s[hd] = qo[:, :, DN_DK:]
        kn = _bmm_tn(k * etail3[:, :, lg], sol)
        ktw_s[hd] = kn[:, :, :DN_DK].astype(BF16)
        n_s[hd] = kn[:, :, DN_DK:]
        egc_s[hd] = jnp.broadcast_to(egc3[:, :, lg], (nc, 1, DN_DV))

    def carry_body(ch, carry):
        for hd in range(DN_HEADS):
            s_h = S[hd]
            s_b = s_h.astype(BF16)
            sl_s[hd, ch] = s_b
            S[hd] = egc_s[hd, ch] * s_h + (n_s[hd, ch] - jnp.dot(ktw_s[hd, ch], s_b,
                                                              preferred_element_type=F32))
        return carry

    lax.fori_loop(0, nc, carry_body, 0)

    for hd in range(DN_HEADS):
        ls = slice(hd * DN_DK, (hd + 1) * DN_DK)
        o = o1_s[hd] + jnp.einsum('bck,bkv->bcv', q2_s[hd], sl_s[hd], preferred_element_type=F32)
        o = o * lax.rsqrt(jnp.mean(o * o, -1, keepdims=True) + RMS_EPS) * dng_ref[...]
        on_ref[:, ls] = o.reshape(tl, DN_DV) * _silu(og_ref[:, ls])

    @pl.when(t == nt - 1)
    def _():
        sout_ref[0] = S[...]


def _pmid(u, qkv, og, bd, wdw, bdw, lcg, lcb, wsh, alog, dtb, dng, batch, seq):
    tl = PM_TL
    nt = seq // tl
    nc = tl // CHUNK
    n = batch * seq
    row = lambda w: pl.BlockSpec((tl, w), lambda b, t: (b * nt + t, 0))
    return pl.pallas_call(
        _pmid_kernel,
        grid=(batch, nt),
        in_specs=[row(CONV_CH), row(QKV_W), row(DN_V), row(LANES),
                  _wspec((CONV_PAD, CONV_CH)), _wspec((1, CONV_CH)), _wspec((1, CONV_CH)),
                  _wspec((1, CONV_CH)), _wspec((SHORT_K, QKV_W)), _wspec((1, LANES)),
                  _wspec((1, LANES)), _wspec((1, DN_DV))],
        out_specs=[row(CONV_CH), row(DN_V),
                   pl.BlockSpec((1, DN_HEADS, DN_DK, DN_DV), lambda b, t: (b, 0, 0, 0))],
        out_shape=[jax.ShapeDtypeStruct((n, CONV_CH), F32), jax.ShapeDtypeStruct((n, DN_V), F32),
                   jax.ShapeDtypeStruct((batch, DN_HEADS, DN_DK, DN_DV), F32)],
        scratch_shapes=[pltpu.VMEM((tl + CONV_PAD, CONV_CH), F32),
                        pltpu.VMEM((tl + SHORT_PAD, QKV_W), F32),
                        pltpu.VMEM((tl, DN_QK), F32), pltpu.VMEM((tl, DN_QK), F32),
                        pltpu.VMEM((tl, DN_V), F32),
                        pltpu.VMEM((DN_HEADS, DN_DK, DN_DV), F32),
                        pltpu.VMEM((DN_HEADS, nc, DN_DK, DN_DK), BF16),
                        pltpu.VMEM((DN_HEADS, nc, DN_DK, DN_DV), F32),
                        pltpu.VMEM((DN_HEADS, nc, CHUNK, DN_DK), BF16),
                        pltpu.VMEM((DN_HEADS, nc, CHUNK, DN_DV), F32),
                        pltpu.VMEM((DN_HEADS, nc, DN_DK, DN_DV), BF16),
                        pltpu.VMEM((DN_HEADS, nc, 1, DN_DV), F32)],
        compiler_params=_params(("parallel", "arbitrary")),
        name="prompt_mid",
    )(u, qkv, og, bd, wdw, bdw, lcg, lcb, wsh, alog, dtb, dng)


SM_BT = 32
SM_BS = 8


def _smid1_kernel(u_ref, qkv_ref, bd_ref, hc_ref, hq_ref, wdw_ref, bdw_ref, lcg_ref, lcb_ref, wsh_ref,
                  alog_ref, dtb_ref,
                  cvn_ref, cnew_ref, qnew_ref, w_ref, q2_ref, uv_ref, o1_ref, kt_ref, egc_ref):
    nl = u_ref.shape[0]
    nh = CONV_K - 1
    xx = [hc_ref[i] for i in range(nh)] + [u_ref[i] for i in range(nl)]
    for t in range(nl):
        acc = wdw_ref[0:1, :] * xx[t]
        for j in range(1, CONV_K):
            acc = acc + wdw_ref[j:j + 1, :] * xx[t + j]
        cvn_ref[t] = _silu(_ln(acc + bdw_ref[...], lcg_ref[...], lcb_ref[...]))
    for i in range(nh):
        cnew_ref[i] = xx[nl + i]

    ns = SHORT_K - 1
    qq = [hq_ref[i] for i in range(ns)] + [qkv_ref[i] for i in range(nl)]
    for i in range(ns):
        qnew_ref[i] = qq[nl + i]
    qkv = []
    for t in range(nl):
        acc = wsh_ref[0:1, :] * qq[t]
        for j in range(1, SHORT_K):
            acc = acc + wsh_ref[j:j + 1, :] * qq[t + j]
        qkv.append(_silu(acc))

    beta = [_sigmoid(bd_ref[t]) for t in range(nl)]
    ld = [-jnp.exp(alog_ref[...]) * _softplus(bd_ref[t] + dtb_ref[...]) for t in range(nl)]

    for hd in range(DN_HEADS):
        ls = slice(hd * DN_DK, (hd + 1) * DN_DK)
        q = [_l2n(qkv[t][:, hd * DN_DK:(hd + 1) * DN_DK]) * (DN_DK ** -0.5) for t in range(nl)]
        k = [_l2n(qkv[t][:, DN_QK + hd * DN_DK:DN_QK + (hd + 1) * DN_DK]) for t in range(nl)]
        v = [qkv[t][:, 2 * DN_QK + hd * DN_DV:2 * DN_QK + (hd + 1) * DN_DV] for t in range(nl)]
        b = [beta[t][:, hd:hd + 1] for t in range(nl)]
        g = [ld[t][:, DN_HEADS + hd:DN_HEADS + hd + 1] for t in range(nl)]
        gc = [g[0]]
        for t in range(1, nl):
            gc.append(gc[t - 1] + g[t])
        w_l, uv_l = [], []
        for i in range(nl):
            wi = (b[i] * jnp.exp(gc[i])) * k[i]
            ui = b[i] * v[i]
            for j in range(i):
                a_ij = b[i] * jnp.sum(k[i] * k[j], -1, keepdims=True) * jnp.exp(gc[i] - gc[j])
                wi = wi - a_ij * w_l[j]
                ui = ui - a_ij * uv_l[j]
            w_l.append(wi)
            uv_l.append(ui)
        for i in range(nl):
            o1 = jnp.zeros_like(v[i])
            q2 = jnp.exp(gc[i]) * q[i]
            for j in range(i + 1):
                aqk = jnp.sum(q[i] * k[j], -1, keepdims=True) * jnp.exp(gc[i] - gc[j])
                o1 = o1 + aqk * uv_l[j]
                q2 = q2 - aqk * w_l[j]
            w_ref[i, :, ls] = w_l[i]
            uv_ref[i, :, ls] = uv_l[i]
            o1_ref[i, :, ls] = o1
            q2_ref[i, :, ls] = q2
            kt_ref[i, :, ls] = k[i] * jnp.exp(gc[nl - 1] - gc[i])
        egc_ref[hd] = jnp.broadcast_to(jnp.exp(gc[nl - 1]), (u_ref.shape[1], LANES))


def _smid1(u, qkv, bd, hc, hq, wdw, bdw, lcg, lcb, wsh, alog, dtb):
    nl, b, _ = u.shape
    bt = SM_BT
    tm = lambda r, w: pl.BlockSpec((r, bt, w), lambda i: (0, i, 0))
    outs = [(nl, CONV_CH), (CONV_K - 1, CONV_CH), (SHORT_K - 1, QKV_W), (nl, DN_QK), (nl, DN_QK),
            (nl, DN_V), (nl, DN_V), (nl, DN_QK), (DN_HEADS, LANES)]
    return pl.pallas_call(
        _smid1_kernel,
        grid=(b // bt,),
        in_specs=[tm(nl, CONV_CH), tm(nl, QKV_W), tm(nl, LANES), tm(CONV_K - 1, CONV_CH),
                  tm(SHORT_K - 1, QKV_W),
                  _wspec((CONV_PAD, CONV_CH)), _wspec((1, CONV_CH)), _wspec((1, CONV_CH)),
                  _wspec((1, CONV_CH)), _wspec((SHORT_K, QKV_W)), _wspec((1, LANES)),
                  _wspec((1, LANES))],
        out_specs=[tm(r, w) for r, w in outs],
        out_shape=[jax.ShapeDtypeStruct((r, b, w), F32) for r, w in outs],
        compiler_params=_params(("parallel",)),
        name="sample_mid1",
    )(u, qkv, bd, hc, hq, wdw, bdw, lcg, lcb, wsh, alog, dtb)


def _smid2_kernel(w_ref, q2_ref, uv_ref, o1_ref, kt_ref, og_ref, egc_ref, s_ref, dng_ref,
                  on_ref, snew_ref):
    nl = w_ref.shape[1]
    for hd in range(DN_HEADS):
        ls = slice(hd * DN_DK, (hd + 1) * DN_DK)
        s_h = s_ref[:, hd]
        lhs = jnp.concatenate([w_ref[:, :, ls], q2_ref[:, :, ls]], axis=1)
        r = jnp.einsum('bck,bkv->bcv', lhs.astype(BF16), s_h.astype(BF16),
                       preferred_element_type=F32)
        u_new = uv_ref[:, :, ls] - r[:, :nl]
        o = o1_ref[:, :, ls] + r[:, nl:]
        upd = jnp.einsum('bck,bcv->bkv', kt_ref[:, :, ls].astype(BF16), u_new.astype(BF16),
                         preferred_element_type=F32)
        snew_ref[:, hd] = egc_ref[:, hd] * s_h + upd
        o = o * lax.rsqrt(jnp.mean(o * o, -1, keepdims=True) + RMS_EPS) * dng_ref[...]
        on_ref[:, :, ls] = o * _silu(og_ref[:, :, ls])


def _smid2(w, q2, uv, o1, kt, og, egc, s, dng):
    b, nl, _ = w.shape
    bs = SM_BS
    sq = pl.BlockSpec((bs, nl, DN_V), lambda i: (i, 0, 0))
    st = pl.BlockSpec((bs, DN_HEADS, DN_DK, DN_DV), lambda i: (i, 0, 0, 0))
    return pl.pallas_call(
        _smid2_kernel,
        grid=(b // bs,),
        in_specs=[sq, sq, sq, sq, sq, sq,
                  pl.BlockSpec((bs, DN_HEADS, 1, LANES), lambda i: (i, 0, 0, 0)), st,
                  _wspec((1, DN_DV))],
        out_specs=[sq, st],
        out_shape=[jax.ShapeDtypeStruct((b, nl, DN_V), F32),
                   jax.ShapeDtypeStruct((b, DN_HEADS, DN_DK, DN_DV), F32)],
        compiler_params=_params(("parallel",)),
        name="sample_mid2",
    )(w, q2, uv, o1, kt, og, egc, s, dng)


def _tm(x):
    return jnp.swapaxes(x, 0, 1)


def kernel(x_prompt, x_sample, mem_prompt, state_conv, state_qkv_conv, state_delta, cache_mem_k, cache_mem_v, w_in, b_glu, w_dw, b_dw, ln_conv_g, ln_conv_b, w_conv_out, w_short, a_log, dt_bias, dn_norm_g, w_dn_out, w_o, ln1_g, ln1_b, w_xq, w_mem_kv, w_xo, ln2_g, ln2_b, w_ff1, w_ff2, ln3_g, ln3_b):
    assert w_in.shape[0] == DEPTH == 1
    bp, lp, _ = x_prompt.shape
    bsm, ls_, _ = x_sample.shape

    w = w_in[0]
    o = (2 * CONV_CH, 2 * CONV_CH + QKV_W, 2 * CONV_CH + QKV_W + DN_V)
    o_beta, o_dec, o_ga = o[2], o[2] + DN_HEADS, o[2] + 2 * DN_HEADS
    w_pack = jnp.concatenate(
        [w[:, :o[2]], w[:, o_ga:o_ga + 2 * D_MODEL], w[:, o_beta:o_ga],
         jnp.zeros((D_MODEL, LANES - 2 * DN_HEADS), w.dtype)], axis=1).astype(BF16)
    lane_pad = lambda a: jnp.concatenate(
        [jnp.zeros((DN_HEADS,), F32), a.astype(F32), jnp.zeros((LANES - 2 * DN_HEADS,), F32)])[None]
    alog = lane_pad(a_log[0])
    dtb = lane_pad(dt_bias[0])
    wdw = jnp.concatenate([w_dw[0], jnp.zeros((CONV_PAD - CONV_K, CONV_CH), F32)], axis=0)
    r2 = lambda a: a[0][None]
    bglu, bdw, lcg, lcb = r2(b_glu), r2(b_dw), r2(ln_conv_g), r2(ln_conv_b)
    dng = r2(dn_norm_g)
    wsh = w_short[0]
    wca, wdn, wo, wxq, wxo = (a[0].astype(BF16) for a in (w_conv_out, w_dn_out, w_o, w_xq, w_xo))
    w1, w2, wkv = w_ff1[0].astype(BF16), w_ff2[0].astype(BF16), w_mem_kv[0].astype(BF16)
    g1, b1, g2, b2, g3, b3 = (r2(a) for a in (ln1_g, ln1_b, ln2_g, ln2_b, ln3_g, ln3_b))

    n_p = bp * lp
    xp = x_prompt.reshape(n_p, D_MODEL)
    mk, mv = _memkv(mem_prompt.reshape(bp * N_MEM, D_MODEL), wkv, 512)
    u, qkv, og, ga, gb, bd = _inproj(xp, w_pack, bglu, 256)
    cvn, on, p_delta = _pmid(u, qkv, og, bd, wdw, bdw, lcg, lcb, wsh, alog, dtb, dng, bp, lp)
    h, xq = _merge(cvn, on, ga, gb, xp, wca, wdn, wo, wxq, g1, b1, 512)
    att = _attn(xq.reshape(bp, lp, D_MODEL), mk.reshape(bp, N_MEM, D_MODEL),
                mv.reshape(bp, N_MEM, D_MODEL), 1, 512)
    y_p = _tail(att.reshape(n_p, D_MODEL), h, wxo, w1, w2, g2, b2, g3, b3, 512).reshape(bp, lp, D_MODEL)
    p_conv = u.reshape(bp, lp, CONV_CH)[:, lp - (CONV_K - 1):]
    p_qkv = qkv.reshape(bp, lp, QKV_W)[:, lp - (SHORT_K - 1):]
    p_mk = mk.reshape(bp, N_MEM, X_HEADS, X_HEAD_DIM)
    p_mv = mv.reshape(bp, N_MEM, X_HEADS, X_HEAD_DIM)

    n_s = bsm * ls_
    xs = _tm(x_sample).reshape(n_s, D_MODEL)
    u, qkv, og, ga, gb, bd = _inproj(xs, w_pack, bglu, 256)
    t3 = lambda a: a.reshape(ls_, bsm, a.shape[-1])
    cvn, c_new, q_new, w_, q2, uv, o1, kt, egc = _smid1(
        t3(u), t3(qkv), t3(bd), _tm(state_conv[0]), _tm(state_qkv_conv[0]),
        wdw, bdw, lcg, lcb, wsh, alog, dtb)
    on, s_delta = _smid2(_tm(w_), _tm(q2), _tm(uv), _tm(o1), _tm(kt), _tm(t3(og)),
                         _tm(egc)[:, :, None, :], state_delta[0], dng)
    h, xq = _merge(cvn.reshape(n_s, CONV_CH), _tm(on).reshape(n_s, DN_V), ga, gb, xs,
                   wca, wdn, wo, wxq, g1, b1, 512)
    att = _attn(_tm(t3(xq)), cache_mem_k, cache_mem_v, 4, ls_)
    y_s = _tail(_tm(att).reshape(n_s, D_MODEL), h, wxo, w1, w2, g2, b2, g3, b3, 512)
    y_s = _tm(t3(y_s))

    return (y_p, y_s, p_conv[None], p_qkv[None], p_delta[None], p_mk[None], p_mv[None],
            _tm(c_new)[None], _tm(q_new)[None], s_delta[None])
```

```python
import functools

import jax
import jax.numpy as jnp
from jax import lax
from jax.experimental import pallas as pl
from jax.experimental.pallas import tpu as pltpu

F32 = jnp.float32
BF16 = jnp.bfloat16

D_MODEL = 1024
N_MEM = 256
CONV_CH = 512
CONV_K = 31
DN_HEADS = 4
DN_DK = 128
DN_DV = 128
DN_QK = DN_HEADS * DN_DK
DN_V = DN_HEADS * DN_DV
QKV_W = 2 * DN_QK + DN_V
SHORT_K = 4
CHUNK = 64
X_HEADS = 4
X_HEAD_DIM = D_MODEL // X_HEADS
D_FF = 4 * D_MODEL
DEPTH = 1
DEEPNORM_ALPHA = (2 * DEPTH) ** 0.25
LN_EPS = 1e-5
RMS_EPS = 1e-6
L2_EPS = 1e-6

LANES = 128
N_PACK = 2 * CONV_CH + QKV_W + DN_V + 2 * D_MODEL + LANES
OFF_QKV = 2 * CONV_CH
OFF_OG = OFF_QKV + QKV_W
OFF_GA = OFF_OG + DN_V
OFF_GB = OFF_GA + D_MODEL
OFF_BD = OFF_GB + D_MODEL
VMEM_LIMIT = 56 * 1024 * 1024


def _dot(a, b):
    return jnp.dot(a.astype(BF16), b.astype(BF16), preferred_element_type=F32)


def _dot_nt(a, b):
    return lax.dot_general(a.astype(BF16), b.astype(BF16), (((1,), (1,)), ((), ())),
                           preferred_element_type=F32)


def _sigmoid(x):
    return 1.0 / (1.0 + jnp.exp(-x))


def _silu(x):
    return x * _sigmoid(x)


def _softplus(x):
    return jnp.maximum(x, 0.0) + jnp.log(1.0 + jnp.exp(-jnp.abs(x)))


def _ln(x, g, b):
    mu = jnp.mean(x, -1, keepdims=True)
    xc = x - mu
    var = jnp.mean(xc * xc, -1, keepdims=True)
    return xc * lax.rsqrt(var + LN_EPS) * g + b


def _l2n(x):
    return x * lax.rsqrt(jnp.sum(x * x, -1, keepdims=True) + L2_EPS)


def _wspec(shape):
    return pl.BlockSpec(shape, lambda *_: (0,) * len(shape))


def _params(sem):
    return pltpu.CompilerParams(dimension_semantics=sem, vmem_limit_bytes=VMEM_LIMIT)


def _inproj_kernel(x_ref, w_ref, bglu_ref, u_ref, qkv_ref, og_ref, ga_ref, gb_ref, bd_ref):
    x = x_ref[...].astype(BF16)
    glu = jnp.dot(x, w_ref[:, 0:OFF_QKV], preferred_element_type=F32) + bglu_ref[...]
    u_ref[...] = glu[:, :CONV_CH] * _sigmoid(glu[:, CONV_CH:])
    qkv_ref[...] = jnp.dot(x, w_ref[:, OFF_QKV:OFF_OG], preferred_element_type=F32)
    og_ref[...] = jnp.dot(x, w_ref[:, OFF_OG:OFF_GA], preferred_element_type=F32)
    ga_ref[...] = jnp.dot(x, w_ref[:, OFF_GA:OFF_GB], preferred_element_type=F32)
    gb_ref[...] = jnp.dot(x, w_ref[:, OFF_GB:OFF_BD], preferred_element_type=F32)
    bd_ref[...] = jnp.dot(x, w_ref[:, OFF_BD:N_PACK], preferred_element_type=F32)


def _inproj(x, w_pack, b_glu, tm):
    n = x.shape[0]
    widths = (CONV_CH, QKV_W, DN_V, D_MODEL, D_MODEL, LANES)
    row = lambda w: pl.BlockSpec((tm, w), lambda i: (i, 0))
    return pl.pallas_call(
        _inproj_kernel,
        grid=(n // tm,),
        in_specs=[row(D_MODEL), _wspec((D_MODEL, N_PACK)), _wspec((1, 2 * CONV_CH))],
        out_specs=[row(w) for w in widths],
        out_shape=[jax.ShapeDtypeStruct((n, w), F32) for w in widths],
        compiler_params=_params(("parallel",)),
        name="inproj",
    )(x, w_pack, b_glu)


def _merge_kernel(cvn_ref, on_ref, ga_ref, gb_ref, x_ref, wca_ref, wdn_ref, wo_ref, wxq_ref,
                  g1_ref, b1_ref, h_ref, xq_ref):
    br_a = _dot(cvn_ref[...], wca_ref[...])
    br_b = _dot(on_ref[...], wdn_ref[...])
    mixed = _sigmoid(ga_ref[...]) * br_a + _sigmoid(gb_ref[...]) * br_b
    h = _ln(DEEPNORM_ALPHA * x_ref[...] + _dot(mixed, wo_ref[...]), g1_ref[...], b1_ref[...])
    h_ref[...] = h
    xq_ref[...] = _dot(h, wxq_ref[...])


def _merge(cvn, on, ga, gb, x, wca, wdn, wo, wxq, g1, b1, tm):
    n = x.shape[0]
    row = lambda w: pl.BlockSpec((tm, w), lambda i: (i, 0))
    return pl.pallas_call(
        _merge_kernel,
        grid=(n // tm,),
        in_specs=[row(CONV_CH), row(DN_V), row(D_MODEL), row(D_MODEL), row(D_MODEL),
                  _wspec((CONV_CH, D_MODEL)), _wspec((DN_V, D_MODEL)), _wspec((D_MODEL, D_MODEL)),
                  _wspec((D_MODEL, D_MODEL)), _wspec((1, D_MODEL)), _wspec((1, D_MODEL))],
        out_specs=[row(D_MODEL), row(D_MODEL)],
        out_shape=[jax.ShapeDtypeStruct((n, D_MODEL), F32)] * 2,
        compiler_params=_params(("parallel",)),
        name="merge",
    )(cvn, on, ga, gb, x, wca, wdn, wo, wxq, g1, b1)


FF_BLOCK = 1024


def _tail_kernel(att_ref, h_ref, wxo_ref, w1_ref, w2_ref, g2_ref, b2_ref, g3_ref, b3_ref, y_ref):
    xo = _dot(att_ref[...], wxo_ref[...])
    h2 = _ln(DEEPNORM_ALPHA * h_ref[...] + xo, g2_ref[...], b2_ref[...])
    h2b = h2.astype(BF16)
    ff = jnp.zeros(h2.shape, F32)
    for c in range(D_FF // FF_BLOCK):
        a = jnp.dot(h2b, w1_ref[:, c * FF_BLOCK:(c + 1) * FF_BLOCK], preferred_element_type=F32)
        a = jnp.square(jnp.maximum(a, 0.0))
        ff = ff + jnp.dot(a.astype(BF16), w2_ref[c * FF_BLOCK:(c + 1) * FF_BLOCK, :],
                          preferred_element_type=F32)
    y_ref[...] = _ln(DEEPNORM_ALPHA * h2 + ff, g3_ref[...], b3_ref[...])


def _tail(att, h, wxo, w1, w2, g2, b2, g3, b3, tm):
    n = h.shape[0]
    row = lambda w: pl.BlockSpec((tm, w), lambda i: (i, 0))
    return pl.pallas_call(
        _tail_kernel,
        grid=(n // tm,),
        in_specs=[row(D_MODEL), row(D_MODEL), _wspec((D_MODEL, D_MODEL)), _wspec((D_MODEL, D_FF)),
                  _wspec((D_FF, D_MODEL)), _wspec((1, D_MODEL)), _wspec((1, D_MODEL)),
                  _wspec((1, D_MODEL)), _wspec((1, D_MODEL))],
        out_specs=row(D_MODEL),
        out_shape=jax.ShapeDtypeStruct((n, D_MODEL), F32),
        compiler_params=_params(("parallel",)),
        name="tail",
    )(att, h, wxo, w1, w2, g2, b2, g3, b3)


def _memkv_kernel(m_ref, w_ref, k_ref, v_ref):
    m = m_ref[...].astype(BF16)
    k_ref[...] = jnp.dot(m, w_ref[:, :D_MODEL], preferred_element_type=F32)
    v_ref[...] = jnp.dot(m, w_ref[:, D_MODEL:], preferred_element_type=F32)


def _memkv(mem, w, tm):
    n = mem.shape[0]
    row = pl.BlockSpec((tm, D_MODEL), lambda i: (i, 0))
    return pl.pallas_call(
        _memkv_kernel,
        grid=(n // tm,),
        in_specs=[row, _wspec((D_MODEL, 2 * D_MODEL))],
        out_specs=[row, row],
        out_shape=[jax.ShapeDtypeStruct((n, D_MODEL), F32)] * 2,
        compiler_params=_params(("parallel",)),
        name="memkv",
    )(mem, w)


def _softmax(sc):
    sc = sc - jnp.max(sc, -1, keepdims=True)
    e = jnp.exp(sc)
    return e / jnp.sum(e, -1, keepdims=True)


def _attn_kernel(q_ref, k_ref, v_ref, o_ref, *, nseq):
    scale = X_HEAD_DIM ** -0.5
    for s in range(nseq):
        for hd in range(X_HEADS):
            sl = slice(hd * X_HEAD_DIM, (hd + 1) * X_HEAD_DIM)
            pr = _softmax(_dot_nt(q_ref[s, :, sl], k_ref[s, :, sl]) * scale)
            o_ref[s, :, sl] = _dot(pr, v_ref[s, :, sl])


def _attn_split_kernel(q_ref, k_ref, v_ref, o_ref, *, nseq):
    scale = X_HEAD_DIM ** -0.5
    nl = q_ref.shape[1]
    nk = N_MEM * X_HEADS
    row_head = lax.broadcasted_iota(jnp.int32, (X_HEADS * nl, nk), 0) // nl
    col_head = lax.broadcasted_iota(jnp.int32, (X_HEADS * nl, nk), 1) % X_HEADS
    own = row_head == col_head
    for s in range(nseq):
        k2 = k_ref[s].reshape(nk, X_HEAD_DIM)
        v2 = v_ref[s].reshape(nk, X_HEAD_DIM)
        q = q_ref[s]
        q4 = jnp.concatenate([q[:, h * X_HEAD_DIM:(h + 1) * X_HEAD_DIM] for h in range(X_HEADS)], axis=0)
        pr = _softmax(jnp.where(own, _dot_nt(q4, k2) * scale, -1e30))
        o4 = _dot(pr, v2)
        for h in range(X_HEADS):
            o_ref[s, :, h * X_HEAD_DIM:(h + 1) * X_HEAD_DIM] = o4[h * nl:(h + 1) * nl]


def _attn(q, k, v, nseq, tq):
    b, l, _ = q.shape
    split = k.ndim == 5
    if split:
        kv_spec = pl.BlockSpec((None, nseq, N_MEM, X_HEADS, X_HEAD_DIM), lambda i, j: (0, i, 0, 0, 0))
    else:
        kv_spec = pl.BlockSpec((nseq, N_MEM, D_MODEL), lambda i, j: (i, 0, 0))
    return pl.pallas_call(
        functools.partial(_attn_split_kernel if split else _attn_kernel, nseq=nseq),
        grid=(b // nseq, l // tq),
        in_specs=[pl.BlockSpec((nseq, tq, D_MODEL), lambda i, j: (i, j, 0)), kv_spec, kv_spec],
        out_specs=pl.BlockSpec((nseq, tq, D_MODEL), lambda i, j: (i, j, 0)),
        out_shape=jax.ShapeDtypeStruct((b, l, D_MODEL), F32),
        compiler_params=_params(("parallel", "parallel")),
        name="attn",
    )(q, k, v)


PM_TL = 512
ROW_STRIDE = 4
CONV_RB = 8 * ROW_STRIDE
CONV_PAD = 32
SHORT_PAD = 8


def _split3(x):
    h = x.astype(BF16)
    r = x - h.astype(F32)
    m = r.astype(BF16)
    l = (r - m.astype(F32)).astype(BF16)
    return h, m, l


def _dot_exact_lhs(lhs_bf16, x):
    h, m, l = _split3(x)
    d = lambda p: jnp.dot(lhs_bf16, p, preferred_element_type=F32)
    return d(h) + d(m) + d(l)


def _bmm(a, b):
    return jnp.einsum('bij,bjk->bik', a.astype(BF16), b.astype(BF16), preferred_element_type=F32)


def _bmm_nt(a, b):
    return jnp.einsum('bik,bjk->bij', a.astype(BF16), b.astype(BF16), preferred_element_type=F32)


def _bmm_tn(a, b):
    return jnp.einsum('bki,bkj->bij', a.astype(BF16), b.astype(BF16), preferred_element_type=F32)


def _strided_conv(src_ref, w_ref, dst_ref, slab, dslab, nblk, ntaps, off, post):
    ws = [w_ref[slab, j] for j in range(ntaps)]

    def body(rb, carry):
        r0 = pl.multiple_of(rb * CONV_RB, CONV_RB)
        accs = [None] * ROW_STRIDE
        for q in range(off, off + ntaps + ROW_STRIDE - 1):
            x = src_ref[slab, pl.ds(r0 + q, 8, stride=ROW_STRIDE), :]
            for m in range(ROW_STRIDE):
                j = q - off - m
                if 0 <= j < ntaps:
                    term = ws[j] * x
                    accs[m] = term if accs[m] is None else accs[m] + term
        for m in range(ROW_STRIDE):
            dst_ref[dslab, pl.ds(r0 + m, 8, stride=ROW_STRIDE), :] = post(accs[m])
        return carry

    lax.fori_loop(0, nblk, body, 0, unroll=True)


def _pmid_kernel(u_ref, qkv_ref, og_ref, bd_ref, wdw_ref, bdw_ref, lcg_ref, lcb_ref, wsh_ref,
                 alog_ref, dtb_ref, dng_ref, cvn_ref, on_ref, sout_ref,
                 xx, qx, ya, qs, ks, vs, S, ktw_s, n_s, q2_s, o1_s, sl_s, egc_s):
    t = pl.program_id(1)
    nt = pl.num_programs(1)
    tl = PM_TL
    c = CHUNK
    nc = tl // c
    n_cs = CONV_CH // LANES
    n_qs = QKV_W // LANES

    @pl.when(t == 0)
    def _():
        xx[:, 0:CONV_PAD, :] = jnp.zeros((n_cs, CONV_PAD, LANES), F32)
        qx[:, 0:SHORT_PAD, :] = jnp.zeros((n_qs, SHORT_PAD, LANES), F32)
        S[...] = jnp.zeros(S.shape, F32)

    for sl in range(n_cs):
        xx[sl, CONV_PAD:CONV_PAD + tl, :] = u_ref[:, sl * LANES:(sl + 1) * LANES]
    for sl in range(n_cs):
        _strided_conv(xx, wdw_ref, ya, sl, sl, tl // CONV_RB, CONV_K, CONV_PAD - (CONV_K - 1), lambda y: y)
        xx[sl, 0:CONV_PAD, :] = xx[sl, tl:tl + CONV_PAD, :]
    for rb in range(tl // CHUNK):
        r0 = rb * CHUNK
        cv = jnp.concatenate([ya[sl, r0:r0 + CHUNK, :] for sl in range(n_cs)], axis=1) + bdw_ref[...]
        cvn_ref[r0:r0 + CHUNK, :] = _silu(_ln(cv, lcg_ref[...], lcb_ref[...]))

    for s12 in range(n_qs):
        qx[s12, SHORT_PAD:SHORT_PAD + tl, :] = qkv_ref[:, s12 * LANES:(s12 + 1) * LANES]
        part, hd = divmod(s12, DN_HEADS)
        _strided_conv(qx, wsh_ref, (qs, ks, vs)[part], s12, hd, tl // CONV_RB, SHORT_K,
                      SHORT_PAD - (SHORT_K - 1), _silu)
        qx[s12, 0:SHORT_PAD, :] = qx[s12, tl:tl + SHORT_PAD, :]

    bd = bd_ref[...]
    beta = _sigmoid(bd)
    ld = -jnp.exp(alog_ref[...]) * _softplus(bd + dtb_ref[...])
    rt = lax.broadcasted_iota(jnp.int32, (tl, tl), 0)
    ct = lax.broadcasted_iota(jnp.int32, (tl, tl), 1)
    blk_tril = jnp.where((rt >= ct) & ((rt // c) == (ct // c)), 1.0, 0.0).astype(BF16)
    gsum = _dot_exact_lhs(blk_tril, ld)
    beta3 = beta.reshape(nc, c, LANES)
    ld3 = ld.reshape(nc, c, LANES)
    gsum3 = gsum.reshape(nc, c, LANES)
    glast3 = gsum3[:, c - 1:c, :]
    eg3 = jnp.exp(gsum3)
    etail3 = jnp.exp(glast3 - gsum3)
    egc3 = jnp.exp(glast3)

    ri = lax.broadcasted_iota(jnp.int32, (c, c), 0)
    ci = lax.broadcasted_iota(jnp.int32, (c, c), 1)
    tril = (ri >= ci)[None]
    strict = (ri > ci)[None]
    tril_b = jnp.broadcast_to(jnp.where(ri >= ci, 1.0, 0.0).astype(BF16)[None], (nc, c, c))
    upper_f = jnp.where(ri > ci, 1.0, 0.0)[None]
    eye = jnp.where(ri == ci, 1.0, 0.0)[None]

    for hd in range(DN_HEADS):
        ls = slice(hd * DN_DK, (hd + 1) * DN_DK)
        lb = slice(hd, hd + 1)
        lg = slice(DN_HEADS + hd, DN_HEADS + hd + 1)
        q = _l2n(qs[hd].reshape(nc, c, DN_DK)) * (DN_DK ** -0.5)
        k = _l2n(ks[hd].reshape(nc, c, DN_DK))
        v = vs[hd].reshape(nc, c, DN_DV)
        b_col = beta3[:, :, lb]
        h3, m3, l3 = _split3(ld3[:, :, lg] * upper_f)
        diff = _bmm(tril_b, h3) + _bmm(tril_b, m3) + _bmm(tril_b, l3)
        decay = jnp.where(tril, jnp.exp(jnp.where(tril, diff, 0.0)), 0.0)
        kq = _bmm_nt(jnp.concatenate([k, q], axis=1), k)
        a = jnp.where(strict, b_col * kq[:, :c] * decay, 0.0)
        aqk = kq[:, c:] * decay
        p = eye - a
        x = a
        for _ in range(5):
            x = _bmm(x, x)
            p = p + _bmm(p, x)
        rhs = jnp.concatenate([(b_col * eg3[:, :, lg]) * k, b_col * v], axis=2)
        sol = _bmm(p, rhs)
        qo = _bmm(aqk, sol)
        q2_s[hd] = (eg3[:, :, lg] * q - qo[:, :, :DN_DK]).astype(BF16)
        o1_s[hd] = qo[:, :, DN_DK:]
        kn = _bmm_tn(k * etail3[:, :, lg], sol)
        ktw_s[hd] = kn[:, :, :DN_DK].astype(BF16)
        n_s[hd] = kn[:, :, DN_DK:]
        egc_s[hd] = jnp.broadcast_to(egc3[:, :, lg], (nc, 1, DN_DV))

    def carry_body(ch, carry):
        for hd in range(DN_HEADS):
            s_h = S[hd]
            s_b = s_h.astype(BF16)
            sl_s[hd, ch] = s_b
            S[hd] = egc_s[hd, ch] * s_h + (n_s[hd, ch] - jnp.dot(ktw_s[hd, ch], s_b,
                                                              preferred_element_type=F32))
        return carry

    lax.fori_loop(0, nc, carry_body, 0)

    for hd in range(DN_HEADS):
        ls = slice(hd * DN_DK, (hd + 1) * DN_DK)
        o = o1_s[hd] + jnp.einsum('bck,bkv->bcv', q2_s[hd], sl_s[hd], preferred_element_type=F32)
        o = o * lax.rsqrt(jnp.mean(o * o, -1, keepdims=True) + RMS_EPS) * dng_ref[...]
        on_ref[:, ls] = o.reshape(tl, DN_DV) * _silu(og_ref[:, ls])

    @pl.when(t == nt - 1)
    def _():
        sout_ref[0] = S[...]


def _pmid(u, qkv, og, bd, wdw, bdw, lcg, lcb, wsh, alog, dtb, dng, batch, seq):
    tl = PM_TL
    nt = seq // tl
    nc = tl // CHUNK
    n = batch * seq
    row = lambda w: pl.BlockSpec((tl, w), lambda b, t: (b * nt + t, 0))
    return pl.pallas_call(
        _pmid_kernel,
        grid=(batch, nt),
        in_specs=[row(CONV_CH), row(QKV_W), row(DN_V), row(LANES),
                  _wspec((CONV_CH // LANES, CONV_K, 8, LANES)), _wspec((1, CONV_CH)), _wspec((1, CONV_CH)),
                  _wspec((1, CONV_CH)), _wspec((QKV_W // LANES, SHORT_K, 8, LANES)), _wspec((1, LANES)),
                  _wspec((1, LANES)), _wspec((1, DN_DV))],
        out_specs=[row(CONV_CH), row(DN_V),
                   pl.BlockSpec((1, DN_HEADS, DN_DK, DN_DV), lambda b, t: (b, 0, 0, 0))],
        out_shape=[jax.ShapeDtypeStruct((n, CONV_CH), F32), jax.ShapeDtypeStruct((n, DN_V), F32),
                   jax.ShapeDtypeStruct((batch, DN_HEADS, DN_DK, DN_DV), F32)],
        scratch_shapes=[pltpu.VMEM((CONV_CH // LANES, tl + CONV_PAD, LANES), F32),
                        pltpu.VMEM((QKV_W // LANES, tl + SHORT_PAD, LANES), F32),
                        pltpu.VMEM((CONV_CH // LANES, tl, LANES), F32),
                        pltpu.VMEM((DN_HEADS, tl, DN_DK), F32), pltpu.VMEM((DN_HEADS, tl, DN_DK), F32),
                        pltpu.VMEM((DN_HEADS, tl, DN_DV), F32),
                        pltpu.VMEM((DN_HEADS, DN_DK, DN_DV), F32),
                        pltpu.VMEM((DN_HEADS, nc, DN_DK, DN_DK), BF16),
                        pltpu.VMEM((DN_HEADS, nc, DN_DK, DN_DV), F32),
                        pltpu.VMEM((DN_HEADS, nc, CHUNK, DN_DK), BF16),
                        pltpu.VMEM((DN_HEADS, nc, CHUNK, DN_DV), F32),
                        pltpu.VMEM((DN_HEADS, nc, DN_DK, DN_DV), BF16),
                        pltpu.VMEM((DN_HEADS, nc, 1, DN_DV), F32)],
        compiler_params=_params(("parallel", "arbitrary")),
        name="prompt_mid",
    )(u, qkv, og, bd, wdw, bdw, lcg, lcb, wsh, alog, dtb, dng)


SM_BT = 32
SM_BS = 8


def _smid1_kernel(u_ref, qkv_ref, bd_ref, hc_ref, hq_ref, wdw_ref, bdw_ref, lcg_ref, lcb_ref, wsh_ref,
                  alog_ref, dtb_ref,
                  cvn_ref, cnew_ref, qnew_ref, w_ref, q2_ref, uv_ref, o1_ref, kt_ref, egc_ref):
    nl = u_ref.shape[0]
    nh = CONV_K - 1
    xx = [hc_ref[i] for i in range(nh)] + [u_ref[i] for i in range(nl)]
    for t in range(nl):
        acc = wdw_ref[0:1, :] * xx[t]
        for j in range(1, CONV_K):
            acc = acc + wdw_ref[j:j + 1, :] * xx[t + j]
        cvn_ref[t] = _silu(_ln(acc + bdw_ref[...], lcg_ref[...], lcb_ref[...]))
    for i in range(nh):
        cnew_ref[i] = xx[nl + i]

    ns = SHORT_K - 1
    qq = [hq_ref[i] for i in range(ns)] + [qkv_ref[i] for i in range(nl)]
    for i in range(ns):
        qnew_ref[i] = qq[nl + i]
    qkv = []
    for t in range(nl):
        acc = wsh_ref[0:1, :] * qq[t]
        for j in range(1, SHORT_K):
            acc = acc + wsh_ref[j:j + 1, :] * qq[t + j]
        qkv.append(_silu(acc))

    beta = [_sigmoid(bd_ref[t]) for t in range(nl)]
    ld = [-jnp.exp(alog_ref[...]) * _softplus(bd_ref[t] + dtb_ref[...]) for t in range(nl)]

    for hd in range(DN_HEADS):
        ls = slice(hd * DN_DK, (hd + 1) * DN_DK)
        q = [_l2n(qkv[t][:, hd * DN_DK:(hd + 1) * DN_DK]) * (DN_DK ** -0.5) for t in range(nl)]
        k = [_l2n(qkv[t][:, DN_QK + hd * DN_DK:DN_QK + (hd + 1) * DN_DK]) for t in range(nl)]
        v = [qkv[t][:, 2 * DN_QK + hd * DN_DV:2 * DN_QK + (hd + 1) * DN_DV] for t in range(nl)]
        b = [beta[t][:, hd:hd + 1] for t in range(nl)]
        g = [ld[t][:, DN_HEADS + hd:DN_HEADS + hd + 1] for t in range(nl)]
        gc = [g[0]]
        for t in range(1, nl):
            gc.append(gc[t - 1] + g[t])
        w_l, uv_l = [], []
        for i in range(nl):
            wi = (b[i] * jnp.exp(gc[i])) * k[i]
            ui = b[i] * v[i]
            for j in range(i):
                a_ij = b[i] * jnp.sum(k[i] * k[j], -1, keepdims=True) * jnp.exp(gc[i] - gc[j])
                wi = wi - a_ij * w_l[j]
                ui = ui - a_ij * uv_l[j]
            w_l.append(wi)
            uv_l.append(ui)
        for i in range(nl):
            o1 = jnp.zeros_like(v[i])
            q2 = jnp.exp(gc[i]) * q[i]
            for j in range(i + 1):
                aqk = jnp.sum(q[i] * k[j], -1, keepdims=True) * jnp.exp(gc[i] - gc[j])
                o1 = o1 + aqk * uv_l[j]
                q2 = q2 - aqk * w_l[j]
            w_ref[i, :, ls] = w_l[i]
            uv_ref[i, :, ls] = uv_l[i]
            o1_ref[i, :, ls] = o1
            q2_ref[i, :, ls] = q2
            kt_ref[i, :, ls] = k[i] * jnp.exp(gc[nl - 1] - gc[i])
        egc_ref[hd] = jnp.broadcast_to(jnp.exp(gc[nl - 1]), (u_ref.shape[1], LANES))


def _smid1(u, qkv, bd, hc, hq, wdw, bdw, lcg, lcb, wsh, alog, dtb):
    nl, b, _ = u.shape
    bt = SM_BT
    tm = lambda r, w: pl.BlockSpec((r, bt, w), lambda i: (0, i, 0))
    outs = [(nl, CONV_CH), (CONV_K - 1, CONV_CH), (SHORT_K - 1, QKV_W), (nl, DN_QK), (nl, DN_QK),
            (nl, DN_V), (nl, DN_V), (nl, DN_QK), (DN_HEADS, LANES)]
    return pl.pallas_call(
        _smid1_kernel,
        grid=(b // bt,),
        in_specs=[tm(nl, CONV_CH), tm(nl, QKV_W), tm(nl, LANES), tm(CONV_K - 1, CONV_CH),
                  tm(SHORT_K - 1, QKV_W),
                  _wspec((CONV_PAD, CONV_CH)), _wspec((1, CONV_CH)), _wspec((1, CONV_CH)),
                  _wspec((1, CONV_CH)), _wspec((SHORT_K, QKV_W)), _wspec((1, LANES)),
                  _wspec((1, LANES))],
        out_specs=[tm(r, w) for r, w in outs],
        out_shape=[jax.ShapeDtypeStruct((r, b, w), F32) for r, w in outs],
        compiler_params=_params(("parallel",)),
        name="sample_mid1",
    )(u, qkv, bd, hc, hq, wdw, bdw, lcg, lcb, wsh, alog, dtb)


def _smid2_kernel(w_ref, q2_ref, uv_ref, o1_ref, kt_ref, og_ref, egc_ref, s_ref, dng_ref,
                  on_ref, snew_ref):
    nl = w_ref.shape[1]
    for hd in range(DN_HEADS):
        ls = slice(hd * DN_DK, (hd + 1) * DN_DK)
        s_h = s_ref[:, hd]
        lhs = jnp.concatenate([w_ref[:, :, ls], q2_ref[:, :, ls]], axis=1)
        r = jnp.einsum('bck,bkv->bcv', lhs.astype(BF16), s_h.astype(BF16),
                       preferred_element_type=F32)
        u_new = uv_ref[:, :, ls] - r[:, :nl]
        o = o1_ref[:, :, ls] + r[:, nl:]
        upd = jnp.einsum('bck,bcv->bkv', kt_ref[:, :, ls].astype(BF16), u_new.astype(BF16),
                         preferred_element_type=F32)
        snew_ref[:, hd] = egc_ref[:, hd] * s_h + upd
        o = o * lax.rsqrt(jnp.mean(o * o, -1, keepdims=True) + RMS_EPS) * dng_ref[...]
        on_ref[:, :, ls] = o * _silu(og_ref[:, :, ls])


def _smid2(w, q2, uv, o1, kt, og, egc, s, dng):
    b, nl, _ = w.shape
    bs = SM_BS
    sq = pl.BlockSpec((bs, nl, DN_V), lambda i: (i, 0, 0))
    st = pl.BlockSpec((bs, DN_HEADS, DN_DK, DN_DV), lambda i: (i, 0, 0, 0))
    return pl.pallas_call(
        _smid2_kernel,
        grid=(b // bs,),
        in_specs=[sq, sq, sq, sq, sq, sq,
                  pl.BlockSpec((bs, DN_HEADS, 1, LANES), lambda i: (i, 0, 0, 0)), st,
                  _wspec((1, DN_DV))],
        out_specs=[sq, st],
        out_shape=[jax.ShapeDtypeStruct((b, nl, DN_V), F32),
                   jax.ShapeDtypeStruct((b, DN_HEADS, DN_DK, DN_DV), F32)],
        compiler_params=_params(("parallel",)),
        name="sample_mid2",
    )(w, q2, uv, o1, kt, og, egc, s, dng)


def _tm(x):
    return jnp.swapaxes(x, 0, 1)


def kernel(x_prompt, x_sample, mem_prompt, state_conv, state_qkv_conv, state_delta, cache_mem_k, cache_mem_v, w_in, b_glu, w_dw, b_dw, ln_conv_g, ln_conv_b, w_conv_out, w_short, a_log, dt_bias, dn_norm_g, w_dn_out, w_o, ln1_g, ln1_b, w_xq, w_mem_kv, w_xo, ln2_g, ln2_b, w_ff1, w_ff2, ln3_g, ln3_b):
    assert w_in.shape[0] == DEPTH == 1
    bp, lp, _ = x_prompt.shape
    bsm, ls_, _ = x_sample.shape

    w = w_in[0]
    o = (2 * CONV_CH, 2 * CONV_CH + QKV_W, 2 * CONV_CH + QKV_W + DN_V)
    o_beta, o_dec, o_ga = o[2], o[2] + DN_HEADS, o[2] + 2 * DN_HEADS
    w_pack = jnp.concatenate(
        [w[:, :o[2]], w[:, o_ga:o_ga + 2 * D_MODEL], w[:, o_beta:o_ga],
         jnp.zeros((D_MODEL, LANES - 2 * DN_HEADS), w.dtype)], axis=1).astype(BF16)
    lane_pad = lambda a: jnp.concatenate(
        [jnp.zeros((DN_HEADS,), F32), a.astype(F32), jnp.zeros((LANES - 2 * DN_HEADS,), F32)])[None]
    alog = lane_pad(a_log[0])
    dtb = lane_pad(dt_bias[0])
    wdw = jnp.concatenate([w_dw[0], jnp.zeros((CONV_PAD - CONV_K, CONV_CH), F32)], axis=0)
    r2 = lambda a: a[0][None]
    bglu, bdw, lcg, lcb = r2(b_glu), r2(b_dw), r2(ln_conv_g), r2(ln_conv_b)
    dng = r2(dn_norm_g)
    wsh = w_short[0]
    slabs = lambda a: jnp.broadcast_to(
        jnp.swapaxes(a.reshape(a.shape[0], a.shape[1] // LANES, LANES), 0, 1)[:, :, None, :],
        (a.shape[1] // LANES, a.shape[0], 8, LANES))
    wdw_slab = slabs(w_dw[0])
    wsh_slab = slabs(wsh)
    wca, wdn, wo, wxq, wxo = (a[0].astype(BF16) for a in (w_conv_out, w_dn_out, w_o, w_xq, w_xo))
    w1, w2, wkv = w_ff1[0].astype(BF16), w_ff2[0].astype(BF16), w_mem_kv[0].astype(BF16)
    g1, b1, g2, b2, g3, b3 = (r2(a) for a in (ln1_g, ln1_b, ln2_g, ln2_b, ln3_g, ln3_b))

    n_p = bp * lp
    xp = x_prompt.reshape(n_p, D_MODEL)
    mk, mv = _memkv(mem_prompt.reshape(bp * N_MEM, D_MODEL), wkv, 512)
    u, qkv, og, ga, gb, bd = _inproj(xp, w_pack, bglu, 256)
    cvn, on, p_delta = _pmid(u, qkv, og, bd, wdw_slab, bdw, lcg, lcb, wsh_slab, alog, dtb, dng, bp, lp)
    h, xq = _merge(cvn, on, ga, gb, xp, wca, wdn, wo, wxq, g1, b1, 512)
    att = _attn(xq.reshape(bp, lp, D_MODEL), mk.reshape(bp, N_MEM, D_MODEL),
                mv.reshape(bp, N_MEM, D_MODEL), 1, 512)
    y_p = _tail(att.reshape(n_p, D_MODEL), h, wxo, w1, w2, g2, b2, g3, b3, 512).reshape(bp, lp, D_MODEL)
    p_conv = u.reshape(bp, lp, CONV_CH)[:, lp - (CONV_K - 1):]
    p_qkv = qkv.reshape(bp, lp, QKV_W)[:, lp - (SHORT_K - 1):]
    p_mk = mk.reshape(bp, N_MEM, X_HEADS, X_HEAD_DIM)
    p_mv = mv.reshape(bp, N_MEM, X_HEADS, X_HEAD_DIM)

    n_s = bsm * ls_
    xs = _tm(x_sample).reshape(n_s, D_MODEL)
    u, qkv, og, ga, gb, bd = _inproj(xs, w_pack, bglu, 256)
    t3 = lambda a: a.reshape(ls_, bsm, a.shape[-1])
    cvn, c_new, q_new, w_, q2, uv, o1, kt, egc = _smid1(
        t3(u), t3(qkv), t3(bd), _tm(state_conv[0]), _tm(state_qkv_conv[0]),
        wdw, bdw, lcg, lcb, wsh, alog, dtb)
    on, s_delta = _smid2(_tm(w_), _tm(q2), _tm(uv), _tm(o1), _tm(kt), _tm(t3(og)),
                         _tm(egc)[:, :, None, :], state_delta[0], dng)
    h, xq = _merge(cvn.reshape(n_s, CONV_CH), _tm(on).reshape(n_s, DN_V), ga, gb, xs,
                   wca, wdn, wo, wxq, g1, b1, 512)
    att = _attn(_tm(t3(xq)), cache_mem_k, cache_mem_v, 4, ls_)
    y_s = _tail(_tm(att).reshape(n_s, D_MODEL), h, wxo, w1, w2, g2, b2, g3, b3, 512)
    y_s = _tm(t3(y_s))

    return (y_p, y_s, p_conv[None], p_qkv[None], p_delta[None], p_mk[None], p_mv[None],
            _tm(c_new)[None], _tm(q_new)[None], s_delta[None])
```

```python
import functools

import jax
import jax.numpy as jnp
from jax import lax
from jax.experimental import pallas as pl
from jax.experimental.pallas import tpu as pltpu

F32 = jnp.float32
BF16 = jnp.bfloat16

D_MODEL = 1024
N_MEM = 256
CONV_CH = 512
CONV_K = 31
DN_HEADS = 4
DN_DK = 128
DN_DV = 128
DN_QK = DN_HEADS * DN_DK
DN_V = DN_HEADS * DN_DV
QKV_W = 2 * DN_QK + DN_V
SHORT_K = 4
CHUNK = 64
X_HEADS = 4
X_HEAD_DIM = D_MODEL // X_HEADS
D_FF = 4 * D_MODEL
DEPTH = 1
DEEPNORM_ALPHA = (2 * DEPTH) ** 0.25
LN_EPS = 1e-5
RMS_EPS = 1e-6
L2_EPS = 1e-6

LANES = 128
N_PACK = 2 * CONV_CH + QKV_W + DN_V + 2 * D_MODEL + LANES
OFF_QKV = 2 * CONV_CH
OFF_OG = OFF_QKV + QKV_W
OFF_GA = OFF_OG + DN_V
OFF_GB = OFF_GA + D_MODEL
OFF_BD = OFF_GB + D_MODEL
VMEM_LIMIT = 56 * 1024 * 1024


def _dot(a, b):
    return jnp.dot(a.astype(BF16), b.astype(BF16), preferred_element_type=F32)


def _dot_nt(a, b):
    return lax.dot_general(a.astype(BF16), b.astype(BF16), (((1,), (1,)), ((), ())),
                           preferred_element_type=F32)


def _sigmoid(x):
    return 1.0 / (1.0 + jnp.exp(-x))


def _silu(x):
    return x * _sigmoid(x)


def _softplus(x):
    return jnp.maximum(x, 0.0) + jnp.log(1.0 + jnp.exp(-jnp.abs(x)))


def _ln(x, g, b):
    mu = jnp.mean(x, -1, keepdims=True)
    xc = x - mu
    var = jnp.mean(xc * xc, -1, keepdims=True)
    return xc * lax.rsqrt(var + LN_EPS) * g + b


def _l2n(x):
    return x * lax.rsqrt(jnp.sum(x * x, -1, keepdims=True) + L2_EPS)


def _wspec(shape):
    return pl.BlockSpec(shape, lambda *_: (0,) * len(shape), pipeline_mode=pl.Buffered(1))


def _params(sem):
    return pltpu.CompilerParams(dimension_semantics=sem, vmem_limit_bytes=VMEM_LIMIT)


def _inproj_kernel(x_ref, w_ref, bglu_ref, u_ref, qkv_ref, og_ref, ga_ref, gb_ref, bd_ref):
    x = x_ref[...].astype(BF16)
    glu = jnp.dot(x, w_ref[:, 0:OFF_QKV], preferred_element_type=F32) + bglu_ref[...]
    u_ref[...] = glu[:, :CONV_CH] * _sigmoid(glu[:, CONV_CH:])
    qkv_ref[...] = jnp.dot(x, w_ref[:, OFF_QKV:OFF_OG], preferred_element_type=F32)
    og_ref[...] = jnp.dot(x, w_ref[:, OFF_OG:OFF_GA], preferred_element_type=F32)
    ga_ref[...] = jnp.dot(x, w_ref[:, OFF_GA:OFF_GB], preferred_element_type=F32)
    gb_ref[...] = jnp.dot(x, w_ref[:, OFF_GB:OFF_BD], preferred_element_type=F32)
    bd_ref[...] = jnp.dot(x, w_ref[:, OFF_BD:N_PACK], preferred_element_type=F32)


def _inproj(x, w_pack, b_glu, tm):
    n = x.shape[0]
    widths = (CONV_CH, QKV_W, DN_V, D_MODEL, D_MODEL, LANES)
    row = lambda w: pl.BlockSpec((tm, w), lambda i: (i, 0))
    return pl.pallas_call(
        _inproj_kernel,
        grid=(n // tm,),
        in_specs=[row(D_MODEL), _wspec((D_MODEL, N_PACK)), _wspec((1, 2 * CONV_CH))],
        out_specs=[row(w) for w in widths],
        out_shape=[jax.ShapeDtypeStruct((n, w), F32) for w in widths],
        compiler_params=_params(("parallel",)),
        name="inproj",
    )(x, w_pack, b_glu)


def _merge_kernel(cvn_ref, on_ref, ga_ref, gb_ref, x_ref, wca_ref, wdn_ref, wo_ref, wxq_ref,
                  g1_ref, b1_ref, h_ref, xq_ref):
    br_a = _dot(cvn_ref[...], wca_ref[...])
    br_b = _dot(on_ref[...], wdn_ref[...])
    mixed = _sigmoid(ga_ref[...]) * br_a + _sigmoid(gb_ref[...]) * br_b
    h = _ln(DEEPNORM_ALPHA * x_ref[...] + _dot(mixed, wo_ref[...]), g1_ref[...], b1_ref[...])
    h_ref[...] = h
    xq_ref[...] = _dot(h, wxq_ref[...])


def _merge(cvn, on, ga, gb, x, wca, wdn, wo, wxq, g1, b1, tm):
    n = x.shape[0]
    row = lambda w: pl.BlockSpec((tm, w), lambda i: (i, 0))
    return pl.pallas_call(
        _merge_kernel,
        grid=(n // tm,),
        in_specs=[row(CONV_CH), row(DN_V), row(D_MODEL), row(D_MODEL), row(D_MODEL),
                  _wspec((CONV_CH, D_MODEL)), _wspec((DN_V, D_MODEL)), _wspec((D_MODEL, D_MODEL)),
                  _wspec((D_MODEL, D_MODEL)), _wspec((1, D_MODEL)), _wspec((1, D_MODEL))],
        out_specs=[row(D_MODEL), row(D_MODEL)],
        out_shape=[jax.ShapeDtypeStruct((n, D_MODEL), F32)] * 2,
        compiler_params=_params(("parallel",)),
        name="merge",
    )(cvn, on, ga, gb, x, wca, wdn, wo, wxq, g1, b1)


FF_BLOCK = 1024


def _tail_kernel(att_ref, h_ref, wxo_ref, w1_ref, w2_ref, g2_ref, b2_ref, g3_ref, b3_ref, y_ref):
    xo = _dot(att_ref[...], wxo_ref[...])
    h2 = _ln(DEEPNORM_ALPHA * h_ref[...] + xo, g2_ref[...], b2_ref[...])
    h2b = h2.astype(BF16)
    ff = jnp.zeros(h2.shape, F32)
    for c in range(D_FF // FF_BLOCK):
        a = jnp.dot(h2b, w1_ref[:, c * FF_BLOCK:(c + 1) * FF_BLOCK], preferred_element_type=F32)
        a = jnp.square(jnp.maximum(a, 0.0))
        ff = ff + jnp.dot(a.astype(BF16), w2_ref[c * FF_BLOCK:(c + 1) * FF_BLOCK, :],
                          preferred_element_type=F32)
    y_ref[...] = _ln(DEEPNORM_ALPHA * h2 + ff, g3_ref[...], b3_ref[...])


def _tail(att, h, wxo, w1, w2, g2, b2, g3, b3, tm):
    n = h.shape[0]
    row = lambda w: pl.BlockSpec((tm, w), lambda i: (i, 0))
    return pl.pallas_call(
        _tail_kernel,
        grid=(n // tm,),
        in_specs=[row(D_MODEL), row(D_MODEL), _wspec((D_MODEL, D_MODEL)), _wspec((D_MODEL, D_FF)),
                  _wspec((D_FF, D_MODEL)), _wspec((1, D_MODEL)), _wspec((1, D_MODEL)),
                  _wspec((1, D_MODEL)), _wspec((1, D_MODEL))],
        out_specs=row(D_MODEL),
        out_shape=jax.ShapeDtypeStruct((n, D_MODEL), F32),
        compiler_params=_params(("parallel",)),
        name="tail",
    )(att, h, wxo, w1, w2, g2, b2, g3, b3)


def _memkv_kernel(m_ref, w_ref, k_ref, v_ref):
    m = m_ref[...].astype(BF16)
    k_ref[...] = jnp.dot(m, w_ref[:, :D_MODEL], preferred_element_type=F32)
    v_ref[...] = jnp.dot(m, w_ref[:, D_MODEL:], preferred_element_type=F32)


def _memkv(mem, w, tm):
    n = mem.shape[0]
    row = pl.BlockSpec((tm, D_MODEL), lambda i: (i, 0))
    return pl.pallas_call(
        _memkv_kernel,
        grid=(n // tm,),
        in_specs=[row, _wspec((D_MODEL, 2 * D_MODEL))],
        out_specs=[row, row],
        out_shape=[jax.ShapeDtypeStruct((n, D_MODEL), F32)] * 2,
        compiler_params=_params(("parallel",)),
        name="memkv",
    )(mem, w)


def _softmax(sc):
    sc = sc - jnp.max(sc, -1, keepdims=True)
    e = jnp.exp(sc)
    return e / jnp.sum(e, -1, keepdims=True)


def _attn_kernel(q_ref, k_ref, v_ref, o_ref, *, nseq):
    scale = X_HEAD_DIM ** -0.5
    for s in range(nseq):
        for hd in range(X_HEADS):
            sl = slice(hd * X_HEAD_DIM, (hd + 1) * X_HEAD_DIM)
            pr = _softmax(_dot_nt(q_ref[s, :, sl], k_ref[s, :, sl]) * scale)
            o_ref[s, :, sl] = _dot(pr, v_ref[s, :, sl])


def _attn_split_kernel(q_ref, k_ref, v_ref, o_ref, *, nseq):
    scale = X_HEAD_DIM ** -0.5
    nl = q_ref.shape[1]
    nk = N_MEM * X_HEADS
    row_head = lax.broadcasted_iota(jnp.int32, (X_HEADS * nl, nk), 0) // nl
    col_head = lax.broadcasted_iota(jnp.int32, (X_HEADS * nl, nk), 1) % X_HEADS
    own = row_head == col_head
    for s in range(nseq):
        k2 = k_ref[s].reshape(nk, X_HEAD_DIM)
        v2 = v_ref[s].reshape(nk, X_HEAD_DIM)
        q = q_ref[s]
        q4 = jnp.concatenate([q[:, h * X_HEAD_DIM:(h + 1) * X_HEAD_DIM] for h in range(X_HEADS)], axis=0)
        pr = _softmax(jnp.where(own, _dot_nt(q4, k2) * scale, -1e30))
        o4 = _dot(pr, v2)
        for h in range(X_HEADS):
            o_ref[s, :, h * X_HEAD_DIM:(h + 1) * X_HEAD_DIM] = o4[h * nl:(h + 1) * nl]


def _attn(q, k, v, nseq, tq):
    b, l, _ = q.shape
    split = k.ndim == 5
    if split:
        kv_spec = pl.BlockSpec((None, nseq, N_MEM, X_HEADS, X_HEAD_DIM), lambda i, j: (0, i, 0, 0, 0))
    else:
        kv_spec = pl.BlockSpec((nseq, N_MEM, D_MODEL), lambda i, j: (i, 0, 0))
    return pl.pallas_call(
        functools.partial(_attn_split_kernel if split else _attn_kernel, nseq=nseq),
        grid=(b // nseq, l // tq),
        in_specs=[pl.BlockSpec((nseq, tq, D_MODEL), lambda i, j: (i, j, 0)), kv_spec, kv_spec],
        out_specs=pl.BlockSpec((nseq, tq, D_MODEL), lambda i, j: (i, j, 0)),
        out_shape=jax.ShapeDtypeStruct((b, l, D_MODEL), F32),
        compiler_params=_params(("parallel", "parallel")),
        name="attn",
    )(q, k, v)


PM_TL = 512
ROW_STRIDE = 4
CONV_RB = 8 * ROW_STRIDE
CONV_PAD = 32
SHORT_PAD = 8


def _split3(x):
    h = x.astype(BF16)
    r = x - h.astype(F32)
    m = r.astype(BF16)
    l = (r - m.astype(F32)).astype(BF16)
    return h, m, l


def _dot_exact_lhs(lhs_bf16, x):
    h, m, l = _split3(x)
    d = lambda p: jnp.dot(lhs_bf16, p, preferred_element_type=F32)
    return d(h) + d(m) + d(l)


def _bmm(a, b):
    return jnp.einsum('bij,bjk->bik', a.astype(BF16), b.astype(BF16), preferred_element_type=F32)


def _bmm_nt(a, b):
    return jnp.einsum('bik,bjk->bij', a.astype(BF16), b.astype(BF16), preferred_element_type=F32)


def _bmm_tn(a, b):
    return jnp.einsum('bki,bkj->bij', a.astype(BF16), b.astype(BF16), preferred_element_type=F32)


def _strided_conv(src_ref, w_ref, dst_ref, slab, dslab, nblk, ntaps, off, post):
    ws = [w_ref[slab, j] for j in range(ntaps)]

    def body(rb, carry):
        r0 = pl.multiple_of(rb * CONV_RB, CONV_RB)
        accs = [None] * ROW_STRIDE
        for q in range(off, off + ntaps + ROW_STRIDE - 1):
            x = src_ref[slab, pl.ds(r0 + q, 8, stride=ROW_STRIDE), :]
            for m in range(ROW_STRIDE):
                j = q - off - m
                if 0 <= j < ntaps:
                    term = ws[j] * x
                    accs[m] = term if accs[m] is None else accs[m] + term
        for m in range(ROW_STRIDE):
            dst_ref[dslab, pl.ds(r0 + m, 8, stride=ROW_STRIDE), :] = post(accs[m])
        return carry

    lax.fori_loop(0, nblk, body, 0, unroll=True)


def _pfront_kernel(x_ref, w_ref, bglu_ref, wdw_ref, bdw_ref, lcg_ref, lcb_ref, wsh_ref, alog_ref, dtb_ref,
                   dng_ref, wca_ref, wdn_ref, wo_ref, wxq_ref, g1_ref, b1_ref,
                   h_ref, xq_ref, pconv_ref, pqkv_ref, sout_ref,
                   xx, qx, ya, qs, ks, vs, S, ktw_s, n_s, q2_s, o1_s, sl_s, egc_s, cvn_s, on_s, og_s, ga_s, gb_s):
    t = pl.program_id(1)
    nt = pl.num_programs(1)
    tl = PM_TL
    c = CHUNK
    nc = tl // c
    n_cs = CONV_CH // LANES
    n_qs = QKV_W // LANES

    @pl.when(t == 0)
    def _():
        xx[:, 0:CONV_PAD, :] = jnp.zeros((n_cs, CONV_PAD, LANES), F32)
        qx[:, 0:SHORT_PAD, :] = jnp.zeros((n_qs, SHORT_PAD, LANES), F32)
        S[...] = jnp.zeros(S.shape, F32)

    xb = x_ref[...].astype(BF16)
    proj = lambda lo, hi: jnp.dot(xb, w_ref[:, lo:hi], preferred_element_type=F32)
    conv_a = lambda sl: _strided_conv(xx, wdw_ref, ya, sl, sl, tl // CONV_RB, CONV_K,
                                      CONV_PAD - (CONV_K - 1), lambda y: y)

    def conv_b(part):
        for hd in range(DN_HEADS):
            _strided_conv(qx, wsh_ref, (qs, ks, vs)[part], part * DN_HEADS + hd, hd, tl // CONV_RB, SHORT_K,
                          SHORT_PAD - (SHORT_K - 1), _silu)

    def qkv_part(part):
        y = proj(OFF_QKV + part * DN_QK, OFF_QKV + (part + 1) * DN_QK)
        for hd in range(DN_HEADS):
            qx[part * DN_HEADS + hd, SHORT_PAD:SHORT_PAD + tl, :] = y[:, hd * LANES:(hd + 1) * LANES]

    glu = proj(0, OFF_QKV) + bglu_ref[...]
    u = glu[:, :CONV_CH] * _sigmoid(glu[:, CONV_CH:])
    for sl in range(n_cs):
        xx[sl, CONV_PAD:CONV_PAD + tl, :] = u[:, sl * LANES:(sl + 1) * LANES]
    qkv_part(0)
    conv_a(0)
    qkv_part(1)
    conv_a(1)
    qkv_part(2)
    conv_a(2)
    og_s[...] = proj(OFF_OG, OFF_GA)
    conv_a(3)
    ga_s[:, :DN_V] = proj(OFF_GA, OFF_GA + DN_V)
    conv_b(0)
    ga_s[:, DN_V:] = proj(OFF_GA + DN_V, OFF_GB)
    conv_b(1)
    gb_s[:, :DN_V] = proj(OFF_GB, OFF_GB + DN_V)
    conv_b(2)
    gb_s[:, DN_V:] = proj(OFF_GB + DN_V, OFF_BD)
    bd = proj(OFF_BD, N_PACK)
    for rb in range(tl // CHUNK):
        r0 = rb * CHUNK
        cv = jnp.concatenate([ya[sl, r0:r0 + CHUNK, :] for sl in range(n_cs)], axis=1) + bdw_ref[...]
        cvn_s[r0:r0 + CHUNK, :] = _silu(_ln(cv, lcg_ref[...], lcb_ref[...])).astype(BF16)

    beta = _sigmoid(bd)
    ld = -jnp.exp(alog_ref[...]) * _softplus(bd + dtb_ref[...])
    rt = lax.broadcasted_iota(jnp.int32, (tl, tl), 0)
    ct = lax.broadcasted_iota(jnp.int32, (tl, tl), 1)
    blk_tril = jnp.where((rt >= ct) & ((rt // c) == (ct // c)), 1.0, 0.0).astype(BF16)
    gsum = _dot_exact_lhs(blk_tril, ld)
    beta3 = beta.reshape(nc, c, LANES)
    gsum3 = gsum.reshape(nc, c, LANES)
    gsum_t = gsum.T
    glast3 = gsum3[:, c - 1:c, :]
    eg3 = jnp.exp(gsum3)
    etail3 = jnp.exp(glast3 - gsum3)
    egc3 = jnp.exp(glast3)

    ri = lax.broadcasted_iota(jnp.int32, (c, c), 0)
    ci = lax.broadcasted_iota(jnp.int32, (c, c), 1)
    tril = (ri >= ci)[None]
    strict = (ri > ci)[None]
    eye = jnp.where(ri == ci, 1.0, 0.0)[None]

    for hd in range(DN_HEADS):
        lb = slice(hd, hd + 1)
        lg = slice(DN_HEADS + hd, DN_HEADS + hd + 1)
        q = _l2n(qs[hd].reshape(nc, c, DN_DK)) * (DN_DK ** -0.5)
        k = _l2n(ks[hd].reshape(nc, c, DN_DK))
        v = vs[hd].reshape(nc, c, DN_DV)
        b_col = beta3[:, :, lb]
        g_row = jnp.stack([gsum_t[DN_HEADS + hd:DN_HEADS + hd + 1, ch * c:(ch + 1) * c] for ch in range(nc)])
        diff = gsum3[:, :, lg] - g_row
        decay = jnp.where(tril, jnp.exp(jnp.where(tril, diff, 0.0)), 0.0)
        kq = _bmm_nt(jnp.concatenate([k, q], axis=1), k)
        a = jnp.where(strict, b_col * kq[:, :c] * decay, 0.0)
        aqk = kq[:, c:] * decay
        p = eye - a
        x = a
        for _ in range(5):
            x = _bmm(x, x)
            p = p + _bmm(p, x)
        rhs = jnp.concatenate([(b_col * eg3[:, :, lg]) * k, b_col * v], axis=2)
        sol = _bmm(p, rhs)
        qo = _bmm(aqk, sol)
        q2_s[hd] = (eg3[:, :, lg] * q - qo[:, :, :DN_DK]).astype(BF16)
        o1_s[hd] = qo[:, :, DN_DK:]
        kn = _bmm_tn(k * etail3[:, :, lg], sol)
        ktw_s[hd] = kn[:, :, :DN_DK].astype(BF16)
        n_s[hd] = kn[:, :, DN_DK:]
        egc_s[hd] = jnp.broadcast_to(egc3[:, :, lg], (nc, 1, DN_DV))

    def carry_body(ch, carry):
        for hd in range(DN_HEADS):
            s_h = S[hd]
            s_b = s_h.astype(BF16)
            sl_s[hd, ch] = s_b
            S[hd] = egc_s[hd, ch] * s_h + (n_s[hd, ch] - jnp.dot(ktw_s[hd, ch], s_b,
                                                              preferred_element_type=F32))
        return carry

    lax.fori_loop(0, nc, carry_body, 0)

    for hd in range(DN_HEADS):
        ls = slice(hd * DN_DK, (hd + 1) * DN_DK)
        o = o1_s[hd] + jnp.einsum('bck,bkv->bcv', q2_s[hd], sl_s[hd], preferred_element_type=F32)
        o = o * lax.rsqrt(jnp.mean(o * o, -1, keepdims=True) + RMS_EPS) * dng_ref[...]
        on_s[:, ls] = (o.reshape(tl, DN_DV) * _silu(og_s[:, ls])).astype(BF16)

    mixed = _sigmoid(ga_s[...]) * jnp.dot(cvn_s[...], wca_ref[...], preferred_element_type=F32)
    mixed = mixed + _sigmoid(gb_s[...]) * jnp.dot(on_s[...], wdn_ref[...], preferred_element_type=F32)
    h = _ln(DEEPNORM_ALPHA * x_ref[...] + _dot(mixed, wo_ref[...]), g1_ref[...], b1_ref[...])
    h_ref[...] = h
    xq_ref[...] = _dot(h, wxq_ref[...]).astype(BF16)

    @pl.when(t == nt - 1)
    def _():
        for sl in range(n_cs):
            pconv_ref[0, :, sl * LANES:(sl + 1) * LANES] = xx[sl, tl:tl + CONV_PAD, :]
        for s12 in range(n_qs):
            pqkv_ref[0, :, s12 * LANES:(s12 + 1) * LANES] = qx[s12, tl:tl + SHORT_PAD, :]
        sout_ref[0] = S[...]

    for sl in range(n_cs):
        xx[sl, 0:CONV_PAD, :] = xx[sl, tl:tl + CONV_PAD, :]
    for s12 in range(n_qs):
        qx[s12, 0:SHORT_PAD, :] = qx[s12, tl:tl + SHORT_PAD, :]


def _pfront(x, w_pack, bglu, wdw, bdw, lcg, lcb, wsh, alog, dtb, dng, wca, wdn, wo, wxq, g1, b1, batch, seq):
    tl = PM_TL
    nt = seq // tl
    nc = tl // CHUNK
    n = batch * seq
    row = lambda w: pl.BlockSpec((tl, w), lambda b, t: (b * nt + t, 0))
    per_seq = lambda *s: pl.BlockSpec((1,) + s, lambda b, t: (b,) + (0,) * len(s))
    return pl.pallas_call(
        _pfront_kernel,
        grid=(batch, nt),
        in_specs=[row(D_MODEL), _wspec((D_MODEL, N_PACK)), _wspec((1, 2 * CONV_CH)),
                  _wspec((CONV_CH // LANES, CONV_K, 8, LANES)), _wspec((1, CONV_CH)), _wspec((1, CONV_CH)),
                  _wspec((1, CONV_CH)), _wspec((QKV_W // LANES, SHORT_K, 8, LANES)), _wspec((1, LANES)),
                  _wspec((1, LANES)), _wspec((1, DN_DV)),
                  _wspec((CONV_CH, D_MODEL)), _wspec((DN_V, D_MODEL)), _wspec((D_MODEL, D_MODEL)),
                  _wspec((D_MODEL, D_MODEL)), _wspec((1, D_MODEL)), _wspec((1, D_MODEL))],
        out_specs=[row(D_MODEL), row(D_MODEL), per_seq(CONV_PAD, CONV_CH), per_seq(SHORT_PAD, QKV_W),
                   per_seq(DN_HEADS, DN_DK, DN_DV)],
        out_shape=[jax.ShapeDtypeStruct((n, D_MODEL), F32), jax.ShapeDtypeStruct((n, D_MODEL), BF16),
                   jax.ShapeDtypeStruct((batch, CONV_PAD, CONV_CH), F32),
                   jax.ShapeDtypeStruct((batch, SHORT_PAD, QKV_W), F32),
                   jax.ShapeDtypeStruct((batch, DN_HEADS, DN_DK, DN_DV), F32)],
        scratch_shapes=[pltpu.VMEM((CONV_CH // LANES, tl + CONV_PAD, LANES), F32),
                        pltpu.VMEM((QKV_W // LANES, tl + SHORT_PAD, LANES), F32),
                        pltpu.VMEM((CONV_CH // LANES, tl, LANES), F32),
                        pltpu.VMEM((DN_HEADS, tl, DN_DK), F32), pltpu.VMEM((DN_HEADS, tl, DN_DK), F32),
                        pltpu.VMEM((DN_HEADS, tl, DN_DV), F32),
                        pltpu.VMEM((DN_HEADS, DN_DK, DN_DV), F32),
                        pltpu.VMEM((DN_HEADS, nc, DN_DK, DN_DK), BF16),
                        pltpu.VMEM((DN_HEADS, nc, DN_DK, DN_DV), F32),
                        pltpu.VMEM((DN_HEADS, nc, CHUNK, DN_DK), BF16),
                        pltpu.VMEM((DN_HEADS, nc, CHUNK, DN_DV), F32),
                        pltpu.VMEM((DN_HEADS, nc, DN_DK, DN_DV), BF16),
                        pltpu.VMEM((DN_HEADS, nc, 1, DN_DV), F32),
                        pltpu.VMEM((tl, CONV_CH), BF16), pltpu.VMEM((tl, DN_V), BF16),
                        pltpu.VMEM((tl, DN_V), F32), pltpu.VMEM((tl, D_MODEL), F32),
                        pltpu.VMEM((tl, D_MODEL), F32)],
        compiler_params=_params(("parallel", "arbitrary")),
        name="prompt_front",
    )(x, w_pack, bglu, wdw, bdw, lcg, lcb, wsh, alog, dtb, dng, wca, wdn, wo, wxq, g1, b1)


SM_BT = 32
SM_BS = 8


def _smid1_kernel(u_ref, qkv_ref, bd_ref, hc_ref, hq_ref, wdw_ref, bdw_ref, lcg_ref, lcb_ref, wsh_ref,
                  alog_ref, dtb_ref,
                  cvn_ref, cnew_ref, qnew_ref, w_ref, q2_ref, uv_ref, o1_ref, kt_ref, egc_ref):
    nl = u_ref.shape[0]
    nh = CONV_K - 1
    xx = [hc_ref[i] for i in range(nh)] + [u_ref[i] for i in range(nl)]
    for t in range(nl):
        acc = wdw_ref[0:1, :] * xx[t]
        for j in range(1, CONV_K):
            acc = acc + wdw_ref[j:j + 1, :] * xx[t + j]
        cvn_ref[t] = _silu(_ln(acc + bdw_ref[...], lcg_ref[...], lcb_ref[...]))
    for i in range(nh):
        cnew_ref[i] = xx[nl + i]

    ns = SHORT_K - 1
    qq = [hq_ref[i] for i in range(ns)] + [qkv_ref[i] for i in range(nl)]
    for i in range(ns):
        qnew_ref[i] = qq[nl + i]
    qkv = []
    for t in range(nl):
        acc = wsh_ref[0:1, :] * qq[t]
        for j in range(1, SHORT_K):
            acc = acc + wsh_ref[j:j + 1, :] * qq[t + j]
        qkv.append(_silu(acc))

    beta = [_sigmoid(bd_ref[t]) for t in range(nl)]
    ld = [-jnp.exp(alog_ref[...]) * _softplus(bd_ref[t] + dtb_ref[...]) for t in range(nl)]

    for hd in range(DN_HEADS):
        ls = slice(hd * DN_DK, (hd + 1) * DN_DK)
        q = [_l2n(qkv[t][:, hd * DN_DK:(hd + 1) * DN_DK]) * (DN_DK ** -0.5) for t in range(nl)]
        k = [_l2n(qkv[t][:, DN_QK + hd * DN_DK:DN_QK + (hd + 1) * DN_DK]) for t in range(nl)]
        v = [qkv[t][:, 2 * DN_QK + hd * DN_DV:2 * DN_QK + (hd + 1) * DN_DV] for t in range(nl)]
        b = [beta[t][:, hd:hd + 1] for t in range(nl)]
        g = [ld[t][:, DN_HEADS + hd:DN_HEADS + hd + 1] for t in range(nl)]
        gc = [g[0]]
        for t in range(1, nl):
            gc.append(gc[t - 1] + g[t])
        w_l, uv_l = [], []
        for i in range(nl):
            wi = (b[i] * jnp.exp(gc[i])) * k[i]
            ui = b[i] * v[i]
            for j in range(i):
                a_ij = b[i] * jnp.sum(k[i] * k[j], -1, keepdims=True) * jnp.exp(gc[i] - gc[j])
                wi = wi - a_ij * w_l[j]
                ui = ui - a_ij * uv_l[j]
            w_l.append(wi)
            uv_l.append(ui)
        for i in range(nl):
            o1 = jnp.zeros_like(v[i])
            q2 = jnp.exp(gc[i]) * q[i]
            for j in range(i + 1):
                aqk = jnp.sum(q[i] * k[j], -1, keepdims=True) * jnp.exp(gc[i] - gc[j])
                o1 = o1 + aqk * uv_l[j]
                q2 = q2 - aqk * w_l[j]
            w_ref[i, :, ls] = w_l[i]
            uv_ref[i, :, ls] = uv_l[i]
            o1_ref[i, :, ls] = o1
            q2_ref[i, :, ls] = q2
            kt_ref[i, :, ls] = k[i] * jnp.exp(gc[nl - 1] - gc[i])
        egc_ref[hd] = jnp.broadcast_to(jnp.exp(gc[nl - 1]), (u_ref.shape[1], LANES))


def _smid1(u, qkv, bd, hc, hq, wdw, bdw, lcg, lcb, wsh, alog, dtb):
    nl, b, _ = u.shape
    bt = SM_BT
    tm = lambda r, w: pl.BlockSpec((r, bt, w), lambda i: (0, i, 0))
    outs = [(nl, CONV_CH), (CONV_K - 1, CONV_CH), (SHORT_K - 1, QKV_W), (nl, DN_QK), (nl, DN_QK),
            (nl, DN_V), (nl, DN_V), (nl, DN_QK), (DN_HEADS, LANES)]
    return pl.pallas_call(
        _smid1_kernel,
        grid=(b // bt,),
        in_specs=[tm(nl, CONV_CH), tm(nl, QKV_W), tm(nl, LANES), tm(CONV_K - 1, CONV_CH),
                  tm(SHORT_K - 1, QKV_W),
                  _wspec((CONV_PAD, CONV_CH)), _wspec((1, CONV_CH)), _wspec((1, CONV_CH)),
                  _wspec((1, CONV_CH)), _wspec((SHORT_K, QKV_W)), _wspec((1, LANES)),
                  _wspec((1, LANES))],
        out_specs=[tm(r, w) for r, w in outs],
        out_shape=[jax.ShapeDtypeStruct((r, b, w), F32) for r, w in outs],
        compiler_params=_params(("parallel",)),
        name="sample_mid1",
    )(u, qkv, bd, hc, hq, wdw, bdw, lcg, lcb, wsh, alog, dtb)


def _smid2_kernel(w_ref, q2_ref, uv_ref, o1_ref, kt_ref, og_ref, egc_ref, s_ref, dng_ref,
                  on_ref, snew_ref):
    nl = w_ref.shape[1]
    for hd in range(DN_HEADS):
        ls = slice(hd * DN_DK, (hd + 1) * DN_DK)
        s_h = s_ref[:, hd]
        lhs = jnp.concatenate([w_ref[:, :, ls], q2_ref[:, :, ls]], axis=1)
        r = jnp.einsum('bck,bkv->bcv', lhs.astype(BF16), s_h.astype(BF16),
                       preferred_element_type=F32)
        u_new = uv_ref[:, :, ls] - r[:, :nl]
        o = o1_ref[:, :, ls] + r[:, nl:]
        upd = jnp.einsum('bck,bcv->bkv', kt_ref[:, :, ls].astype(BF16), u_new.astype(BF16),
                         preferred_element_type=F32)
        snew_ref[:, hd] = egc_ref[:, hd] * s_h + upd
        o = o * lax.rsqrt(jnp.mean(o * o, -1, keepdims=True) + RMS_EPS) * dng_ref[...]
        on_ref[:, :, ls] = o * _silu(og_ref[:, :, ls])


def _smid2(w, q2, uv, o1, kt, og, egc, s, dng):
    b, nl, _ = w.shape
    bs = SM_BS
    sq = pl.BlockSpec((bs, nl, DN_V), lambda i: (i, 0, 0))
    st = pl.BlockSpec((bs, DN_HEADS, DN_DK, DN_DV), lambda i: (i, 0, 0, 0))
    return pl.pallas_call(
        _smid2_kernel,
        grid=(b // bs,),
        in_specs=[sq, sq, sq, sq, sq, sq,
                  pl.BlockSpec((bs, DN_HEADS, 1, LANES), lambda i: (i, 0, 0, 0)), st,
                  _wspec((1, DN_DV))],
        out_specs=[sq, st],
        out_shape=[jax.ShapeDtypeStruct((b, nl, DN_V), F32),
                   jax.ShapeDtypeStruct((b, DN_HEADS, DN_DK, DN_DV), F32)],
        compiler_params=_params(("parallel",)),
        name="sample_mid2",
    )(w, q2, uv, o1, kt, og, egc, s, dng)


def _tm(x):
    return jnp.swapaxes(x, 0, 1)


def kernel(x_prompt, x_sample, mem_prompt, state_conv, state_qkv_conv, state_delta, cache_mem_k, cache_mem_v, w_in, b_glu, w_dw, b_dw, ln_conv_g, ln_conv_b, w_conv_out, w_short, a_log, dt_bias, dn_norm_g, w_dn_out, w_o, ln1_g, ln1_b, w_xq, w_mem_kv, w_xo, ln2_g, ln2_b, w_ff1, w_ff2, ln3_g, ln3_b):
    assert w_in.shape[0] == DEPTH == 1
    bp, lp, _ = x_prompt.shape
    bsm, ls_, _ = x_sample.shape

    w = w_in[0]
    o = (2 * CONV_CH, 2 * CONV_CH + QKV_W, 2 * CONV_CH + QKV_W + DN_V)
    o_beta, o_dec, o_ga = o[2], o[2] + DN_HEADS, o[2] + 2 * DN_HEADS
    w_pack = jnp.concatenate(
        [w[:, :o[2]], w[:, o_ga:o_ga + 2 * D_MODEL], w[:, o_beta:o_ga],
         jnp.zeros((D_MODEL, LANES - 2 * DN_HEADS), w.dtype)], axis=1).astype(BF16)
    lane_pad = lambda a: jnp.concatenate(
        [jnp.zeros((DN_HEADS,), F32), a.astype(F32), jnp.zeros((LANES - 2 * DN_HEADS,), F32)])[None]
    alog = lane_pad(a_log[0])
    dtb = lane_pad(dt_bias[0])
    wdw = jnp.concatenate([w_dw[0], jnp.zeros((CONV_PAD - CONV_K, CONV_CH), F32)], axis=0)
    r2 = lambda a: a[0][None]
    bglu, bdw, lcg, lcb = r2(b_glu), r2(b_dw), r2(ln_conv_g), r2(ln_conv_b)
    dng = r2(dn_norm_g)
    wsh = w_short[0]
    slabs = lambda a: jnp.broadcast_to(
        jnp.swapaxes(a.reshape(a.shape[0], a.shape[1] // LANES, LANES), 0, 1)[:, :, None, :],
        (a.shape[1] // LANES, a.shape[0], 8, LANES))
    wdw_slab = slabs(w_dw[0])
    wsh_slab = slabs(wsh)
    wca, wdn, wo, wxq, wxo = (a[0].astype(BF16) for a in (w_conv_out, w_dn_out, w_o, w_xq, w_xo))
    w1, w2, wkv = w_ff1[0].astype(BF16), w_ff2[0].astype(BF16), w_mem_kv[0].astype(BF16)
    g1, b1, g2, b2, g3, b3 = (r2(a) for a in (ln1_g, ln1_b, ln2_g, ln2_b, ln3_g, ln3_b))

    n_p = bp * lp
    xp = x_prompt.reshape(n_p, D_MODEL)
    mk, mv = _memkv(mem_prompt.reshape(bp * N_MEM, D_MODEL), wkv, 512)
    h, xq, p_conv, p_qkv, p_delta = _pfront(xp, w_pack, bglu, wdw_slab, bdw, lcg, lcb, wsh_slab, alog, dtb, dng,
                                            wca, wdn, wo, wxq, g1, b1, bp, lp)
    att = _attn(xq.reshape(bp, lp, D_MODEL), mk.reshape(bp, N_MEM, D_MODEL),
                mv.reshape(bp, N_MEM, D_MODEL), 1, 512)
    y_p = _tail(att.reshape(n_p, D_MODEL), h, wxo, w1, w2, g2, b2, g3, b3, 512).reshape(bp, lp, D_MODEL)
    p_conv = p_conv[:, CONV_PAD - (CONV_K - 1):]
    p_qkv = p_qkv[:, SHORT_PAD - (SHORT_K - 1):]
    p_mk = mk.reshape(bp, N_MEM, X_HEADS, X_HEAD_DIM)
    p_mv = mv.reshape(bp, N_MEM, X_HEADS, X_HEAD_DIM)

    n_s = bsm * ls_
    xs = _tm(x_sample).reshape(n_s, D_MODEL)
    u, qkv, og, ga, gb, bd = _inproj(xs, w_pack, bglu, 256)
    t3 = lambda a: a.reshape(ls_, bsm, a.shape[-1])
    cvn, c_new, q_new, w_, q2, uv, o1, kt, egc = _smid1(
        t3(u), t3(qkv), t3(bd), _tm(state_conv[0]), _tm(state_qkv_conv[0]),
        wdw, bdw, lcg, lcb, wsh, alog, dtb)
    on, s_delta = _smid2(_tm(w_), _tm(q2), _tm(uv), _tm(o1), _tm(kt), _tm(t3(og)),
                         _tm(egc)[:, :, None, :], state_delta[0], dng)
    h, xq = _merge(cvn.reshape(n_s, CONV_CH), _tm(on).reshape(n_s, DN_V), ga, gb, xs,
                   wca, wdn, wo, wxq, g1, b1, 512)
    att = _attn(_tm(t3(xq)), cache_mem_k, cache_mem_v, 4, ls_)
    y_s = _tail(_tm(att).reshape(n_s, D_MODEL), h, wxo, w1, w2, g2, b2, g3, b3, 512)
    y_s = _tm(t3(y_s))

    return (y_p, y_s, p_conv[None], p_qkv[None], p_delta[None], p_mk[None], p_mv[None],
            _tm(c_new)[None], _tm(q_new)[None], s_delta[None])
```

```python
import functools

import jax
import jax.numpy as jnp
from jax import lax
from jax.experimental import pallas as pl
from jax.experimental.pallas import tpu as pltpu

F32 = jnp.float32
BF16 = jnp.bfloat16

D_MODEL = 1024
N_MEM = 256
CONV_CH = 512
CONV_K = 31
DN_HEADS = 4
DN_DK = 128
DN_DV = 128
DN_QK = DN_HEADS * DN_DK
DN_V = DN_HEADS * DN_DV
QKV_W = 2 * DN_QK + DN_V
SHORT_K = 4
CHUNK = 64
X_HEADS = 4
X_HEAD_DIM = D_MODEL // X_HEADS
D_FF = 4 * D_MODEL
DEPTH = 1
DEEPNORM_ALPHA = (2 * DEPTH) ** 0.25
LN_EPS = 1e-5
RMS_EPS = 1e-6
L2_EPS = 1e-6

LANES = 128
N_PACK = 2 * CONV_CH + QKV_W + DN_V + 2 * D_MODEL + LANES
OFF_QKV = 2 * CONV_CH
OFF_OG = OFF_QKV + QKV_W
OFF_GA = OFF_OG + DN_V
OFF_GB = OFF_GA + D_MODEL
OFF_BD = OFF_GB + D_MODEL
VMEM_LIMIT = 56 * 1024 * 1024


def _dot(a, b):
    return jnp.dot(a.astype(BF16), b.astype(BF16), preferred_element_type=F32)


def _dot_nt(a, b):
    return lax.dot_general(a.astype(BF16), b.astype(BF16), (((1,), (1,)), ((), ())),
                           preferred_element_type=F32)


def _sigmoid(x):
    return 1.0 / (1.0 + jnp.exp(-x))


def _silu(x):
    return x * _sigmoid(x)


def _softplus(x):
    return jnp.maximum(x, 0.0) + jnp.log(1.0 + jnp.exp(-jnp.abs(x)))


def _ln(x, g, b):
    mu = jnp.mean(x, -1, keepdims=True)
    xc = x - mu
    var = jnp.mean(xc * xc, -1, keepdims=True)
    return xc * lax.rsqrt(var + LN_EPS) * g + b


def _l2n(x):
    return x * lax.rsqrt(jnp.sum(x * x, -1, keepdims=True) + L2_EPS)


def _wspec(shape):
    return pl.BlockSpec(shape, lambda *_: (0,) * len(shape), pipeline_mode=pl.Buffered(1))


def _params(sem):
    return pltpu.CompilerParams(dimension_semantics=sem, vmem_limit_bytes=VMEM_LIMIT)


def _inproj_kernel(x_ref, w_ref, bglu_ref, u_ref, qkv_ref, og_ref, ga_ref, gb_ref, bd_ref):
    x = x_ref[...].astype(BF16)
    glu = jnp.dot(x, w_ref[:, 0:OFF_QKV], preferred_element_type=F32) + bglu_ref[...]
    u_ref[...] = glu[:, :CONV_CH] * _sigmoid(glu[:, CONV_CH:])
    qkv_ref[...] = jnp.dot(x, w_ref[:, OFF_QKV:OFF_OG], preferred_element_type=F32)
    og_ref[...] = jnp.dot(x, w_ref[:, OFF_OG:OFF_GA], preferred_element_type=F32)
    ga_ref[...] = jnp.dot(x, w_ref[:, OFF_GA:OFF_GB], preferred_element_type=F32)
    gb_ref[...] = jnp.dot(x, w_ref[:, OFF_GB:OFF_BD], preferred_element_type=F32)
    bd_ref[...] = jnp.dot(x, w_ref[:, OFF_BD:N_PACK], preferred_element_type=F32)


def _inproj(x, w_pack, b_glu, tm):
    n = x.shape[0]
    widths = (CONV_CH, QKV_W, DN_V, D_MODEL, D_MODEL, LANES)
    row = lambda w: pl.BlockSpec((tm, w), lambda i: (i, 0))
    return pl.pallas_call(
        _inproj_kernel,
        grid=(n // tm,),
        in_specs=[row(D_MODEL), _wspec((D_MODEL, N_PACK)), _wspec((1, 2 * CONV_CH))],
        out_specs=[row(w) for w in widths],
        out_shape=[jax.ShapeDtypeStruct((n, w), F32) for w in widths],
        compiler_params=_params(("parallel",)),
        name="inproj",
    )(x, w_pack, b_glu)


def _merge_kernel(cvn_ref, on_ref, ga_ref, gb_ref, x_ref, wca_ref, wdn_ref, wo_ref, wxq_ref,
                  g1_ref, b1_ref, h_ref, xq_ref):
    br_a = _dot(cvn_ref[...], wca_ref[...])
    br_b = _dot(on_ref[...], wdn_ref[...])
    mixed = _sigmoid(ga_ref[...]) * br_a + _sigmoid(gb_ref[...]) * br_b
    h = _ln(DEEPNORM_ALPHA * x_ref[...] + _dot(mixed, wo_ref[...]), g1_ref[...], b1_ref[...])
    h_ref[...] = h
    xq_ref[...] = _dot(h, wxq_ref[...])


def _merge(cvn, on, ga, gb, x, wca, wdn, wo, wxq, g1, b1, tm):
    n = x.shape[0]
    row = lambda w: pl.BlockSpec((tm, w), lambda i: (i, 0))
    return pl.pallas_call(
        _merge_kernel,
        grid=(n // tm,),
        in_specs=[row(CONV_CH), row(DN_V), row(D_MODEL), row(D_MODEL), row(D_MODEL),
                  _wspec((CONV_CH, D_MODEL)), _wspec((DN_V, D_MODEL)), _wspec((D_MODEL, D_MODEL)),
                  _wspec((D_MODEL, D_MODEL)), _wspec((1, D_MODEL)), _wspec((1, D_MODEL))],
        out_specs=[row(D_MODEL), row(D_MODEL)],
        out_shape=[jax.ShapeDtypeStruct((n, D_MODEL), F32)] * 2,
        compiler_params=_params(("parallel",)),
        name="merge",
    )(cvn, on, ga, gb, x, wca, wdn, wo, wxq, g1, b1)


FF_BLOCK = 1024


def _tail_kernel(att_ref, h_ref, wxo_ref, w1_ref, w2_ref, g2_ref, b2_ref, g3_ref, b3_ref, y_ref):
    xo = _dot(att_ref[...], wxo_ref[...])
    h2 = _ln(DEEPNORM_ALPHA * h_ref[...] + xo, g2_ref[...], b2_ref[...])
    h2b = h2.astype(BF16)
    ff = jnp.zeros(h2.shape, F32)
    for c in range(D_FF // FF_BLOCK):
        a = jnp.dot(h2b, w1_ref[:, c * FF_BLOCK:(c + 1) * FF_BLOCK], preferred_element_type=F32)
        a = jnp.square(jnp.maximum(a, 0.0))
        ff = ff + jnp.dot(a.astype(BF16), w2_ref[c * FF_BLOCK:(c + 1) * FF_BLOCK, :],
                          preferred_element_type=F32)
    y_ref[...] = _ln(DEEPNORM_ALPHA * h2 + ff, g3_ref[...], b3_ref[...])


def _tail(att, h, wxo, w1, w2, g2, b2, g3, b3, tm):
    n = h.shape[0]
    row = lambda w: pl.BlockSpec((tm, w), lambda i: (i, 0))
    return pl.pallas_call(
        _tail_kernel,
        grid=(n // tm,),
        in_specs=[row(D_MODEL), row(D_MODEL), _wspec((D_MODEL, D_MODEL)), _wspec((D_MODEL, D_FF)),
                  _wspec((D_FF, D_MODEL)), _wspec((1, D_MODEL)), _wspec((1, D_MODEL)),
                  _wspec((1, D_MODEL)), _wspec((1, D_MODEL))],
        out_specs=row(D_MODEL),
        out_shape=jax.ShapeDtypeStruct((n, D_MODEL), F32),
        compiler_params=_params(("parallel",)),
        name="tail",
    )(att, h, wxo, w1, w2, g2, b2, g3, b3)


def _memkv_kernel(m_ref, w_ref, k_ref, v_ref, kb_ref, vb_ref):
    m = m_ref[...].astype(BF16)
    tm = m.shape[0]
    k = jnp.dot(m, w_ref[:, :D_MODEL], preferred_element_type=F32)
    v = jnp.dot(m, w_ref[:, D_MODEL:], preferred_element_type=F32)
    k_ref[...] = k.reshape(tm, X_HEADS, X_HEAD_DIM)
    v_ref[...] = v.reshape(tm, X_HEADS, X_HEAD_DIM)
    kb_ref[...] = k.astype(BF16)
    vb_ref[...] = v.astype(BF16)


def _memkv(mem, w, tm):
    n = mem.shape[0]
    row = pl.BlockSpec((tm, D_MODEL), lambda i: (i, 0))
    row4 = pl.BlockSpec((tm, X_HEADS, X_HEAD_DIM), lambda i: (i, 0, 0))
    return pl.pallas_call(
        _memkv_kernel,
        grid=(n // tm,),
        in_specs=[row, _wspec((D_MODEL, 2 * D_MODEL))],
        out_specs=[row4, row4, row, row],
        out_shape=[jax.ShapeDtypeStruct((n, X_HEADS, X_HEAD_DIM), F32)] * 2
        + [jax.ShapeDtypeStruct((n, D_MODEL), BF16)] * 2,
        compiler_params=_params(("parallel",)),
        name="memkv",
    )(mem, w)


def _softmax(sc):
    sc = sc - jnp.max(sc, -1, keepdims=True)
    e = jnp.exp(sc)
    return e / jnp.sum(e, -1, keepdims=True)


def _ptail_kernel(q_ref, k_ref, v_ref, h_ref, wxo_ref, w1_ref, w2_ref, g2_ref, b2_ref, g3_ref, b3_ref,
                  y_ref, att_s):
    scale = X_HEAD_DIM ** -0.5
    for hd in range(X_HEADS):
        sl = slice(hd * X_HEAD_DIM, (hd + 1) * X_HEAD_DIM)
        pr = _softmax(_dot_nt(q_ref[:, sl], k_ref[0, :, sl]) * scale)
        att_s[:, sl] = _dot(pr, v_ref[0, :, sl]).astype(BF16)
    _tail_kernel(att_s, h_ref, wxo_ref, w1_ref, w2_ref, g2_ref, b2_ref, g3_ref, b3_ref, y_ref)


def _ptail(xq, kb, vb, h, wxo, w1, w2, g2, b2, g3, b3, batch, seq, tq):
    nt = seq // tq
    row = lambda: pl.BlockSpec((tq, D_MODEL), lambda b, t: (b * nt + t, 0))
    mem = pl.BlockSpec((1, N_MEM, D_MODEL), lambda b, t: (b, 0, 0))
    return pl.pallas_call(
        _ptail_kernel,
        grid=(batch, nt),
        in_specs=[row(), mem, mem, row(), _wspec((D_MODEL, D_MODEL)), _wspec((D_MODEL, D_FF)),
                  _wspec((D_FF, D_MODEL)), _wspec((1, D_MODEL)), _wspec((1, D_MODEL)),
                  _wspec((1, D_MODEL)), _wspec((1, D_MODEL))],
        out_specs=row(),
        out_shape=jax.ShapeDtypeStruct((batch * seq, D_MODEL), F32),
        scratch_shapes=[pltpu.VMEM((tq, D_MODEL), BF16)],
        compiler_params=_params(("parallel", "parallel")),
        name="prompt_tail",
    )(xq, kb, vb, h, wxo, w1, w2, g2, b2, g3, b3)


def _attn_split_kernel(q_ref, k_ref, v_ref, o_ref, *, nseq):
    scale = X_HEAD_DIM ** -0.5
    nl = q_ref.shape[1]
    nk = N_MEM * X_HEADS
    row_head = lax.broadcasted_iota(jnp.int32, (X_HEADS * nl, nk), 0) // nl
    col_head = lax.broadcasted_iota(jnp.int32, (X_HEADS * nl, nk), 1) % X_HEADS
    own = row_head == col_head
    for s in range(nseq):
        k2 = k_ref[s].reshape(nk, X_HEAD_DIM)
        v2 = v_ref[s].reshape(nk, X_HEAD_DIM)
        q = q_ref[s]
        q4 = jnp.concatenate([q[:, h * X_HEAD_DIM:(h + 1) * X_HEAD_DIM] for h in range(X_HEADS)], axis=0)
        pr = _softmax(jnp.where(own, _dot_nt(q4, k2) * scale, -1e30))
        o4 = _dot(pr, v2)
        for h in range(X_HEADS):
            o_ref[s, :, h * X_HEAD_DIM:(h + 1) * X_HEAD_DIM] = o4[h * nl:(h + 1) * nl]


def _attn(q, k, v, nseq):
    b, l, _ = q.shape
    kv_spec = pl.BlockSpec((None, nseq, N_MEM, X_HEADS, X_HEAD_DIM), lambda i: (0, i, 0, 0, 0))
    q_spec = pl.BlockSpec((nseq, l, D_MODEL), lambda i: (i, 0, 0))
    return pl.pallas_call(
        functools.partial(_attn_split_kernel, nseq=nseq),
        grid=(b // nseq,),
        in_specs=[q_spec, kv_spec, kv_spec],
        out_specs=q_spec,
        out_shape=jax.ShapeDtypeStruct((b, l, D_MODEL), F32),
        compiler_params=_params(("parallel",)),
        name="attn",
    )(q, k, v)


PM_TL = 512
ROW_STRIDE = 4
CONV_RB = 8 * ROW_STRIDE
CONV_PAD = 32
SHORT_PAD = 8


def _split3(x):
    h = x.astype(BF16)
    r = x - h.astype(F32)
    m = r.astype(BF16)
    l = (r - m.astype(F32)).astype(BF16)
    return h, m, l


def _dot_exact_lhs(lhs_bf16, x):
    h, m, l = _split3(x)
    d = lambda p: jnp.dot(lhs_bf16, p, preferred_element_type=F32)
    return d(h) + d(m) + d(l)


def _bmm(a, b):
    return jnp.einsum('bij,bjk->bik', a.astype(BF16), b.astype(BF16), preferred_element_type=F32)


def _bmm_nt(a, b):
    return jnp.einsum('bik,bjk->bij', a.astype(BF16), b.astype(BF16), preferred_element_type=F32)


def _bmm_tn(a, b):
    return jnp.einsum('bki,bkj->bij', a.astype(BF16), b.astype(BF16), preferred_element_type=F32)


def _strided_conv(src_ref, w_ref, dst_ref, slab, dslab, nblk, ntaps, off, post):
    ws = [w_ref[slab, j] for j in range(ntaps)]

    def body(rb, carry):
        r0 = pl.multiple_of(rb * CONV_RB, CONV_RB)
        accs = [None] * ROW_STRIDE
        for q in range(off, off + ntaps + ROW_STRIDE - 1):
            x = src_ref[slab, pl.ds(r0 + q, 8, stride=ROW_STRIDE), :]
            for m in range(ROW_STRIDE):
                j = q - off - m
                if 0 <= j < ntaps:
                    term = ws[j] * x
                    accs[m] = term if accs[m] is None else accs[m] + term
        for m in range(ROW_STRIDE):
            dst_ref[dslab, pl.ds(r0 + m, 8, stride=ROW_STRIDE), :] = post(accs[m])
        return carry

    lax.fori_loop(0, nblk, body, 0, unroll=True)


def _pfront_kernel(x_ref, w_ref, bglu_ref, wdw_ref, bdw_ref, lcg_ref, lcb_ref, wsh_ref, alog_ref, dtb_ref,
                   dng_ref, wca_ref, wdn_ref, wo_ref, wxq_ref, g1_ref, b1_ref,
                   h_ref, xq_ref, pconv_ref, pqkv_ref, sout_ref,
                   xx, qx, ya, qs, ks, vs, S, ktw_s, n_s, q2_s, o1_s, sl_s, egc_s, cvn_s, on_s, og_s, ga_s, gb_s):
    t = pl.program_id(1)
    nt = pl.num_programs(1)
    tl = PM_TL
    c = CHUNK
    nc = tl // c
    n_cs = CONV_CH // LANES
    n_qs = QKV_W // LANES

    @pl.when(t == 0)
    def _():
        xx[:, 0:CONV_PAD, :] = jnp.zeros((n_cs, CONV_PAD, LANES), F32)
        qx[:, 0:SHORT_PAD, :] = jnp.zeros((n_qs, SHORT_PAD, LANES), F32)
        S[...] = jnp.zeros(S.shape, F32)

    xb = x_ref[...].astype(BF16)
    proj = lambda lo, hi: jnp.dot(xb, w_ref[:, lo:hi], preferred_element_type=F32)
    conv_a = lambda sl: _strided_conv(xx, wdw_ref, ya, sl, sl, tl // CONV_RB, CONV_K,
                                      CONV_PAD - (CONV_K - 1), lambda y: y)

    def conv_b(part):
        for hd in range(DN_HEADS):
            _strided_conv(qx, wsh_ref, (qs, ks, vs)[part], part * DN_HEADS + hd, hd, tl // CONV_RB, SHORT_K,
                          SHORT_PAD - (SHORT_K - 1), _silu)

    def qkv_part(part):
        y = proj(OFF_QKV + part * DN_QK, OFF_QKV + (part + 1) * DN_QK)
        for hd in range(DN_HEADS):
            qx[part * DN_HEADS + hd, SHORT_PAD:SHORT_PAD + tl, :] = y[:, hd * LANES:(hd + 1) * LANES]

    glu = proj(0, OFF_QKV) + bglu_ref[...]
    u = glu[:, :CONV_CH] * _sigmoid(glu[:, CONV_CH:])
    for sl in range(n_cs):
        xx[sl, CONV_PAD:CONV_PAD + tl, :] = u[:, sl * LANES:(sl + 1) * LANES]
    qkv_part(0)
    conv_a(0)
    qkv_part(1)
    conv_a(1)
    qkv_part(2)
    conv_a(2)
    og_s[...] = proj(OFF_OG, OFF_GA)
    conv_a(3)
    ga_s[:, :DN_V] = proj(OFF_GA, OFF_GA + DN_V)
    conv_b(0)
    ga_s[:, DN_V:] = proj(OFF_GA + DN_V, OFF_GB)
    conv_b(1)
    gb_s[:, :DN_V] = proj(OFF_GB, OFF_GB + DN_V)
    conv_b(2)
    gb_s[:, DN_V:] = proj(OFF_GB + DN_V, OFF_BD)
    bd = proj(OFF_BD, N_PACK)
    for rb in range(tl // CHUNK):
        r0 = rb * CHUNK
        cv = jnp.concatenate([ya[sl, r0:r0 + CHUNK, :] for sl in range(n_cs)], axis=1) + bdw_ref[...]
        cvn_s[r0:r0 + CHUNK, :] = _silu(_ln(cv, lcg_ref[...], lcb_ref[...])).astype(BF16)

    beta = _sigmoid(bd)
    ld = -jnp.exp(alog_ref[...]) * _softplus(bd + dtb_ref[...])
    rt = lax.broadcasted_iota(jnp.int32, (tl, tl), 0)
    ct = lax.broadcasted_iota(jnp.int32, (tl, tl), 1)
    blk_tril = jnp.where((rt >= ct) & ((rt // c) == (ct // c)), 1.0, 0.0).astype(BF16)
    gsum = _dot_exact_lhs(blk_tril, ld)
    beta3 = beta.reshape(nc, c, LANES)
    gsum3 = gsum.reshape(nc, c, LANES)
    gsum_t = gsum.T
    glast3 = gsum3[:, c - 1:c, :]
    eg3 = jnp.exp(gsum3)
    etail3 = jnp.exp(glast3 - gsum3)
    egc3 = jnp.exp(glast3)

    ri = lax.broadcasted_iota(jnp.int32, (c, c), 0)
    ci = lax.broadcasted_iota(jnp.int32, (c, c), 1)
    tril = (ri >= ci)[None]
    strict = (ri > ci)[None]
    eye = jnp.where(ri == ci, 1.0, 0.0)[None]

    for hd in range(DN_HEADS):
        lb = slice(hd, hd + 1)
        lg = slice(DN_HEADS + hd, DN_HEADS + hd + 1)
        q = _l2n(qs[hd].reshape(nc, c, DN_DK)) * (DN_DK ** -0.5)
        k = _l2n(ks[hd].reshape(nc, c, DN_DK))
        v = vs[hd].reshape(nc, c, DN_DV)
        b_col = beta3[:, :, lb]
        g_row = jnp.stack([gsum_t[DN_HEADS + hd:DN_HEADS + hd + 1, ch * c:(ch + 1) * c] for ch in range(nc)])
        diff = gsum3[:, :, lg] - g_row
        decay = jnp.where(tril, jnp.exp(jnp.where(tril, diff, 0.0)), 0.0)
        kq = _bmm_nt(jnp.concatenate([k, q], axis=1), k)
        a = jnp.where(strict, b_col * kq[:, :c] * decay, 0.0)
        aqk = kq[:, c:] * decay
        p = eye - a
        x = a
        for _ in range(5):
            x = _bmm(x, x)
            p = p + _bmm(p, x)
        rhs = jnp.concatenate([(b_col * eg3[:, :, lg]) * k, b_col * v], axis=2)
        sol = _bmm(p, rhs)
        qo = _bmm(aqk, sol)
        q2_s[hd] = (eg3[:, :, lg] * q - qo[:, :, :DN_DK]).astype(BF16)
        o1_s[hd] = qo[:, :, DN_DK:]
        kn = _bmm_tn(k * etail3[:, :, lg], sol)
        ktw_s[hd] = kn[:, :, :DN_DK].astype(BF16)
        n_s[hd] = kn[:, :, DN_DK:]
        egc_s[hd] = jnp.broadcast_to(egc3[:, :, lg], (nc, 1, DN_DV))

    def carry_body(ch, carry):
        for hd in range(DN_HEADS):
            s_h = S[hd]
            s_b = s_h.astype(BF16)
            sl_s[hd, ch] = s_b
            S[hd] = egc_s[hd, ch] * s_h + (n_s[hd, ch] - jnp.dot(ktw_s[hd, ch], s_b,
                                                              preferred_element_type=F32))
        return carry

    for ch in range(nc):
        carry_body(ch, 0)

    for hd in range(DN_HEADS):
        ls = slice(hd * DN_DK, (hd + 1) * DN_DK)
        o = o1_s[hd] + jnp.einsum('bck,bkv->bcv', q2_s[hd], sl_s[hd], preferred_element_type=F32)
        o = o * lax.rsqrt(jnp.mean(o * o, -1, keepdims=True) + RMS_EPS) * dng_ref[...]
        on_s[:, ls] = (o.reshape(tl, DN_DV) * _silu(og_s[:, ls])).astype(BF16)

    mixed = _sigmoid(ga_s[...]) * jnp.dot(cvn_s[...], wca_ref[...], preferred_element_type=F32)
    mixed = mixed + _sigmoid(gb_s[...]) * jnp.dot(on_s[...], wdn_ref[...], preferred_element_type=F32)
    h = _ln(DEEPNORM_ALPHA * x_ref[...] + _dot(mixed, wo_ref[...]), g1_ref[...], b1_ref[...])
    h_ref[...] = h
    xq_ref[...] = _dot(h, wxq_ref[...]).astype(BF16)

    @pl.when(t == nt - 1)
    def _():
        for sl in range(n_cs):
            pconv_ref[0, :, sl * LANES:(sl + 1) * LANES] = xx[sl, tl:tl + CONV_PAD, :]
        for s12 in range(n_qs):
            pqkv_ref[0, :, s12 * LANES:(s12 + 1) * LANES] = qx[s12, tl:tl + SHORT_PAD, :]
        sout_ref[0] = S[...]

    for sl in range(n_cs):
        xx[sl, 0:CONV_PAD, :] = xx[sl, tl:tl + CONV_PAD, :]
    for s12 in range(n_qs):
        qx[s12, 0:SHORT_PAD, :] = qx[s12, tl:tl + SHORT_PAD, :]


def _pfront(x, w_pack, bglu, wdw, bdw, lcg, lcb, wsh, alog, dtb, dng, wca, wdn, wo, wxq, g1, b1, batch, seq):
    tl = PM_TL
    nt = seq // tl
    nc = tl // CHUNK
    n = batch * seq
    row = lambda w: pl.BlockSpec((tl, w), lambda b, t: (b * nt + t, 0))
    per_seq = lambda *s: pl.BlockSpec((1,) + s, lambda b, t: (b,) + (0,) * len(s))
    return pl.pallas_call(
        _pfront_kernel,
        grid=(batch, nt),
        in_specs=[row(D_MODEL), _wspec((D_MODEL, N_PACK)), _wspec((1, 2 * CONV_CH)),
                  _wspec((CONV_CH // LANES, CONV_K, 8, LANES)), _wspec((1, CONV_CH)), _wspec((1, CONV_CH)),
                  _wspec((1, CONV_CH)), _wspec((QKV_W // LANES, SHORT_K, 8, LANES)), _wspec((1, LANES)),
                  _wspec((1, LANES)), _wspec((1, DN_DV)),
                  _wspec((CONV_CH, D_MODEL)), _wspec((DN_V, D_MODEL)), _wspec((D_MODEL, D_MODEL)),
                  _wspec((D_MODEL, D_MODEL)), _wspec((1, D_MODEL)), _wspec((1, D_MODEL))],
        out_specs=[row(D_MODEL), row(D_MODEL), per_seq(CONV_PAD, CONV_CH), per_seq(SHORT_PAD, QKV_W),
                   per_seq(DN_HEADS, DN_DK, DN_DV)],
        out_shape=[jax.ShapeDtypeStruct((n, D_MODEL), F32), jax.ShapeDtypeStruct((n, D_MODEL), BF16),
                   jax.ShapeDtypeStruct((batch, CONV_PAD, CONV_CH), F32),
                   jax.ShapeDtypeStruct((batch, SHORT_PAD, QKV_W), F32),
                   jax.ShapeDtypeStruct((batch, DN_HEADS, DN_DK, DN_DV), F32)],
        scratch_shapes=[pltpu.VMEM((CONV_CH // LANES, tl + CONV_PAD, LANES), F32),
                        pltpu.VMEM((QKV_W // LANES, tl + SHORT_PAD, LANES), F32),
                        pltpu.VMEM((CONV_CH // LANES, tl, LANES), F32),
                        pltpu.VMEM((DN_HEADS, tl, DN_DK), F32), pltpu.VMEM((DN_HEADS, tl, DN_DK), F32),
                        pltpu.VMEM((DN_HEADS, tl, DN_DV), F32),
                        pltpu.VMEM((DN_HEADS, DN_DK, DN_DV), F32),
                        pltpu.VMEM((DN_HEADS, nc, DN_DK, DN_DK), BF16),
                        pltpu.VMEM((DN_HEADS, nc, DN_DK, DN_DV), F32),
                        pltpu.VMEM((DN_HEADS, nc, CHUNK, DN_DK), BF16),
                        pltpu.VMEM((DN_HEADS, nc, CHUNK, DN_DV), F32),
                        pltpu.VMEM((DN_HEADS, nc, DN_DK, DN_DV), BF16),
                        pltpu.VMEM((DN_HEADS, nc, 1, DN_DV), F32),
                        pltpu.VMEM((tl, CONV_CH), BF16), pltpu.VMEM((tl, DN_V), BF16),
                        pltpu.VMEM((tl, DN_V), F32), pltpu.VMEM((tl, D_MODEL), F32),
                        pltpu.VMEM((tl, D_MODEL), F32)],
        compiler_params=_params(("parallel", "arbitrary")),
        name="prompt_front",
    )(x, w_pack, bglu, wdw, bdw, lcg, lcb, wsh, alog, dtb, dng, wca, wdn, wo, wxq, g1, b1)


SM_BT = 32
SM_BS = 8


def _smid1_kernel(u_ref, qkv_ref, bd_ref, hc_ref, hq_ref, wdw_ref, bdw_ref, lcg_ref, lcb_ref, wsh_ref,
                  alog_ref, dtb_ref,
                  cvn_ref, cnew_ref, qnew_ref, w_ref, q2_ref, uv_ref, o1_ref, kt_ref, egc_ref):
    nl = u_ref.shape[0]
    nh = CONV_K - 1
    xx = [hc_ref[i] for i in range(nh)] + [u_ref[i] for i in range(nl)]
    for t in range(nl):
        acc = wdw_ref[0:1, :] * xx[t]
        for j in range(1, CONV_K):
            acc = acc + wdw_ref[j:j + 1, :] * xx[t + j]
        cvn_ref[t] = _silu(_ln(acc + bdw_ref[...], lcg_ref[...], lcb_ref[...]))
    for i in range(nh):
        cnew_ref[i] = xx[nl + i]

    ns = SHORT_K - 1
    qq = [hq_ref[i] for i in range(ns)] + [qkv_ref[i] for i in range(nl)]
    for i in range(ns):
        qnew_ref[i] = qq[nl + i]
    qkv = []
    for t in range(nl):
        acc = wsh_ref[0:1, :] * qq[t]
        for j in range(1, SHORT_K):
            acc = acc + wsh_ref[j:j + 1, :] * qq[t + j]
        qkv.append(_silu(acc))

    beta = [_sigmoid(bd_ref[t]) for t in range(nl)]
    ld = [-jnp.exp(alog_ref[...]) * _softplus(bd_ref[t] + dtb_ref[...]) for t in range(nl)]

    for hd in range(DN_HEADS):
        ls = slice(hd * DN_DK, (hd + 1) * DN_DK)
        q = [_l2n(qkv[t][:, hd * DN_DK:(hd + 1) * DN_DK]) * (DN_DK ** -0.5) for t in range(nl)]
        k = [_l2n(qkv[t][:, DN_QK + hd * DN_DK:DN_QK + (hd + 1) * DN_DK]) for t in range(nl)]
        v = [qkv[t][:, 2 * DN_QK + hd * DN_DV:2 * DN_QK + (hd + 1) * DN_DV] for t in range(nl)]
        b = [beta[t][:, hd:hd + 1] for t in range(nl)]
        g = [ld[t][:, DN_HEADS + hd:DN_HEADS + hd + 1] for t in range(nl)]
        gc = [g[0]]
        for t in range(1, nl):
            gc.append(gc[t - 1] + g[t])
        w_l, uv_l = [], []
        for i in range(nl):
            wi = (b[i] * jnp.exp(gc[i])) * k[i]
            ui = b[i] * v[i]
            for j in range(i):
                a_ij = b[i] * jnp.sum(k[i] * k[j], -1, keepdims=True) * jnp.exp(gc[i] - gc[j])
                wi = wi - a_ij * w_l[j]
                ui = ui - a_ij * uv_l[j]
            w_l.append(wi)
            uv_l.append(ui)
        for i in range(nl):
            o1 = jnp.zeros_like(v[i])
            q2 = jnp.exp(gc[i]) * q[i]
            for j in range(i + 1):
                aqk = jnp.sum(q[i] * k[j], -1, keepdims=True) * jnp.exp(gc[i] - gc[j])
                o1 = o1 + aqk * uv_l[j]
                q2 = q2 - aqk * w_l[j]
            w_ref[i, :, ls] = w_l[i]
            uv_ref[i, :, ls] = uv_l[i]
            o1_ref[i, :, ls] = o1
            q2_ref[i, :, ls] = q2
            kt_ref[i, :, ls] = k[i] * jnp.exp(gc[nl - 1] - gc[i])
        egc_ref[hd] = jnp.broadcast_to(jnp.exp(gc[nl - 1]), (u_ref.shape[1], LANES))


def _smid1(u, qkv, bd, hc, hq, wdw, bdw, lcg, lcb, wsh, alog, dtb):
    nl, b, _ = u.shape
    bt = SM_BT
    tm = lambda r, w: pl.BlockSpec((r, bt, w), lambda i: (0, i, 0))
    outs = [(nl, CONV_CH), (CONV_K - 1, CONV_CH), (SHORT_K - 1, QKV_W), (nl, DN_QK), (nl, DN_QK),
            (nl, DN_V), (nl, DN_V), (nl, DN_QK), (DN_HEADS, LANES)]
    return pl.pallas_call(
        _smid1_kernel,
        grid=(b // bt,),
        in_specs=[tm(nl, CONV_CH), tm(nl, QKV_W), tm(nl, LANES), tm(CONV_K - 1, CONV_CH),
                  tm(SHORT_K - 1, QKV_W),
                  _wspec((CONV_PAD, CONV_CH)), _wspec((1, CONV_CH)), _wspec((1, CONV_CH)),
                  _wspec((1, CONV_CH)), _wspec((SHORT_K, QKV_W)), _wspec((1, LANES)),
                  _wspec((1, LANES))],
        out_specs=[tm(r, w) for r, w in outs],
        out_shape=[jax.ShapeDtypeStruct((r, b, w), F32) for r, w in outs],
        compiler_params=_params(("parallel",)),
        name="sample_mid1",
    )(u, qkv, bd, hc, hq, wdw, bdw, lcg, lcb, wsh, alog, dtb)


def _smid2_kernel(w_ref, q2_ref, uv_ref, o1_ref, kt_ref, og_ref, egc_ref, s_ref, dng_ref,
                  on_ref, snew_ref):
    nl = w_ref.shape[1]
    for hd in range(DN_HEADS):
        ls = slice(hd * DN_DK, (hd + 1) * DN_DK)
        s_h = s_ref[:, hd]
        lhs = jnp.concatenate([w_ref[:, :, ls], q2_ref[:, :, ls]], axis=1)
        r = jnp.einsum('bck,bkv->bcv', lhs.astype(BF16), s_h.astype(BF16),
                       preferred_element_type=F32)
        u_new = uv_ref[:, :, ls] - r[:, :nl]
        o = o1_ref[:, :, ls] + r[:, nl:]
        upd = jnp.einsum('bck,bcv->bkv', kt_ref[:, :, ls].astype(BF16), u_new.astype(BF16),
                         preferred_element_type=F32)
        snew_ref[:, hd] = egc_ref[:, hd] * s_h + upd
        o = o * lax.rsqrt(jnp.mean(o * o, -1, keepdims=True) + RMS_EPS) * dng_ref[...]
        on_ref[:, :, ls] = o * _silu(og_ref[:, :, ls])


def _smid2(w, q2, uv, o1, kt, og, egc, s, dng):
    b, nl, _ = w.shape
    bs = SM_BS
    sq = pl.BlockSpec((bs, nl, DN_V), lambda i: (i, 0, 0))
    st = pl.BlockSpec((bs, DN_HEADS, DN_DK, DN_DV), lambda i: (i, 0, 0, 0))
    return pl.pallas_call(
        _smid2_kernel,
        grid=(b // bs,),
        in_specs=[sq, sq, sq, sq, sq, sq,
                  pl.BlockSpec((bs, DN_HEADS, 1, LANES), lambda i: (i, 0, 0, 0)), st,
                  _wspec((1, DN_DV))],
        out_specs=[sq, st],
        out_shape=[jax.ShapeDtypeStruct((b, nl, DN_V), F32),
                   jax.ShapeDtypeStruct((b, DN_HEADS, DN_DK, DN_DV), F32)],
        compiler_params=_params(("parallel",)),
        name="sample_mid2",
    )(w, q2, uv, o1, kt, og, egc, s, dng)


def _tm(x):
    return jnp.swapaxes(x, 0, 1)


def kernel(x_prompt, x_sample, mem_prompt, state_conv, state_qkv_conv, state_delta, cache_mem_k, cache_mem_v, w_in, b_glu, w_dw, b_dw, ln_conv_g, ln_conv_b, w_conv_out, w_short, a_log, dt_bias, dn_norm_g, w_dn_out, w_o, ln1_g, ln1_b, w_xq, w_mem_kv, w_xo, ln2_g, ln2_b, w_ff1, w_ff2, ln3_g, ln3_b):
    assert w_in.shape[0] == DEPTH == 1
    bp, lp, _ = x_prompt.shape
    bsm, ls_, _ = x_sample.shape

    w = w_in[0]
    o = (2 * CONV_CH, 2 * CONV_CH + QKV_W, 2 * CONV_CH + QKV_W + DN_V)
    o_beta, o_dec, o_ga = o[2], o[2] + DN_HEADS, o[2] + 2 * DN_HEADS
    w_pack = jnp.concatenate(
        [w[:, :o[2]], w[:, o_ga:o_ga + 2 * D_MODEL], w[:, o_beta:o_ga],
         jnp.zeros((D_MODEL, LANES - 2 * DN_HEADS), w.dtype)], axis=1).astype(BF16)
    lane_pad = lambda a: jnp.concatenate(
        [jnp.zeros((DN_HEADS,), F32), a.astype(F32), jnp.zeros((LANES - 2 * DN_HEADS,), F32)])[None]
    alog = lane_pad(a_log[0])
    dtb = lane_pad(dt_bias[0])
    wdw = jnp.concatenate([w_dw[0], jnp.zeros((CONV_PAD - CONV_K, CONV_CH), F32)], axis=0)
    r2 = lambda a: a[0][None]
    bglu, bdw, lcg, lcb = r2(b_glu), r2(b_dw), r2(ln_conv_g), r2(ln_conv_b)
    dng = r2(dn_norm_g)
    wsh = w_short[0]
    slabs = lambda a: jnp.broadcast_to(
        jnp.swapaxes(a.reshape(a.shape[0], a.shape[1] // LANES, LANES), 0, 1)[:, :, None, :],
        (a.shape[1] // LANES, a.shape[0], 8, LANES))
    wdw_slab = slabs(w_dw[0])
    wsh_slab = slabs(wsh)
    wca, wdn, wo, wxq, wxo = (a[0].astype(BF16) for a in (w_conv_out, w_dn_out, w_o, w_xq, w_xo))
    w1, w2, wkv = w_ff1[0].astype(BF16), w_ff2[0].astype(BF16), w_mem_kv[0].astype(BF16)
    g1, b1, g2, b2, g3, b3 = (r2(a) for a in (ln1_g, ln1_b, ln2_g, ln2_b, ln3_g, ln3_b))

    n_p = bp * lp
    xp = x_prompt.reshape(n_p, D_MODEL)
    mk, mv, mk_b, mv_b = _memkv(mem_prompt.reshape(bp * N_MEM, D_MODEL), wkv, 512)
    h, xq, p_conv, p_qkv, p_delta = _pfront(xp, w_pack, bglu, wdw_slab, bdw, lcg, lcb, wsh_slab, alog, dtb, dng,
                                            wca, wdn, wo, wxq, g1, b1, bp, lp)
    y_p = _ptail(xq, mk_b.reshape(bp, N_MEM, D_MODEL), mv_b.reshape(bp, N_MEM, D_MODEL), h,
                 wxo, w1, w2, g2, b2, g3, b3, bp, lp, 512).reshape(bp, lp, D_MODEL)
    p_conv = p_conv[:, CONV_PAD - (CONV_K - 1):]
    p_qkv = p_qkv[:, SHORT_PAD - (SHORT_K - 1):]
    p_mk = mk.reshape(bp, N_MEM, X_HEADS, X_HEAD_DIM)
    p_mv = mv.reshape(bp, N_MEM, X_HEADS, X_HEAD_DIM)

    n_s = bsm * ls_
    xs = _tm(x_sample).reshape(n_s, D_MODEL)
    u, qkv, og, ga, gb, bd = _inproj(xs, w_pack, bglu, 256)
    t3 = lambda a: a.reshape(ls_, bsm, a.shape[-1])
    cvn, c_new, q_new, w_, q2, uv, o1, kt, egc = _smid1(
        t3(u), t3(qkv), t3(bd), _tm(state_conv[0]), _tm(state_qkv_conv[0]),
        wdw, bdw, lcg, lcb, wsh, alog, dtb)
    on, s_delta = _smid2(_tm(w_), _tm(q2), _tm(uv), _tm(o1), _tm(kt), _tm(t3(og)),
                         _tm(egc)[:, :, None, :], state_delta[0], dng)
    h, xq = _merge(cvn.reshape(n_s, CONV_CH), _tm(on).reshape(n_s, DN_V), ga, gb, xs,
                   wca, wdn, wo, wxq, g1, b1, 512)
    att = _attn(_tm(t3(xq)), cache_mem_k, cache_mem_v, 4)
    y_s = _tail(_tm(att).reshape(n_s, D_MODEL), h, wxo, w1, w2, g2, b2, g3, b3, 512)
    y_s = _tm(t3(y_s))

    return (y_p, y_s, p_conv[None], p_qkv[None], p_delta[None], p_mk[None], p_mv[None],
            _tm(c_new)[None], _tm(q_new)[None], s_delta[None])
```

```python
import functools

import jax
import jax.numpy as jnp
from jax import lax
from jax.experimental import pallas as pl
from jax.experimental.pallas import tpu as pltpu

F32 = jnp.float32
BF16 = jnp.bfloat16

D_MODEL = 1024
N_MEM = 256
CONV_CH = 512
CONV_K = 31
DN_HEADS = 4
DN_DK = 128
DN_DV = 128
DN_QK = DN_HEADS * DN_DK
DN_V = DN_HEADS * DN_DV
QKV_W = 2 * DN_QK + DN_V
SHORT_K = 4
CHUNK = 64
X_HEADS = 4
X_HEAD_DIM = D_MODEL // X_HEADS
D_FF = 4 * D_MODEL
DEPTH = 1
DEEPNORM_ALPHA = (2 * DEPTH) ** 0.25
LN_EPS = 1e-5
RMS_EPS = 1e-6
L2_EPS = 1e-6

LANES = 128
N_PACK = 2 * CONV_CH + QKV_W + DN_V + 2 * D_MODEL + LANES
OFF_QKV = 2 * CONV_CH
OFF_OG = OFF_QKV + QKV_W
OFF_GA = OFF_OG + DN_V
OFF_GB = OFF_GA + D_MODEL
OFF_BD = OFF_GB + D_MODEL
VMEM_LIMIT = 56 * 1024 * 1024


def _dot(a, b):
    return jnp.dot(a.astype(BF16), b.astype(BF16), preferred_element_type=F32)


def _dot_nt(a, b):
    return lax.dot_general(a.astype(BF16), b.astype(BF16), (((1,), (1,)), ((), ())),
                           preferred_element_type=F32)


def _sigmoid(x):
    return 1.0 / (1.0 + jnp.exp(-x))


def _silu(x):
    return x * _sigmoid(x)


def _softplus(x):
    return jnp.maximum(x, 0.0) + jnp.log(1.0 + jnp.exp(-jnp.abs(x)))


def _ln(x, g, b):
    mu = jnp.mean(x, -1, keepdims=True)
    xc = x - mu
    var = jnp.mean(xc * xc, -1, keepdims=True)
    return xc * lax.rsqrt(var + LN_EPS) * g + b


def _l2n(x):
    return x * lax.rsqrt(jnp.sum(x * x, -1, keepdims=True) + L2_EPS)


def _wspec(shape):
    return pl.BlockSpec(shape, lambda *_: (0,) * len(shape), pipeline_mode=pl.Buffered(1))


def _params(sem):
    return pltpu.CompilerParams(dimension_semantics=sem, vmem_limit_bytes=VMEM_LIMIT)


def _inproj_kernel(x_ref, w_ref, bglu_ref, u_ref, qkv_ref, og_ref, ga_ref, gb_ref, bd_ref):
    x = x_ref[...].astype(BF16)
    glu = jnp.dot(x, w_ref[:, 0:OFF_QKV], preferred_element_type=F32) + bglu_ref[...]
    u_ref[...] = glu[:, :CONV_CH] * _sigmoid(glu[:, CONV_CH:])
    qkv_ref[...] = jnp.dot(x, w_ref[:, OFF_QKV:OFF_OG], preferred_element_type=F32)
    og_ref[...] = jnp.dot(x, w_ref[:, OFF_OG:OFF_GA], preferred_element_type=F32)
    ga_ref[...] = jnp.dot(x, w_ref[:, OFF_GA:OFF_GB], preferred_element_type=F32)
    gb_ref[...] = jnp.dot(x, w_ref[:, OFF_GB:OFF_BD], preferred_element_type=F32)
    bd_ref[...] = jnp.dot(x, w_ref[:, OFF_BD:N_PACK], preferred_element_type=F32)


def _inproj(x, w_pack, b_glu, tm):
    n = x.shape[0]
    widths = (CONV_CH, QKV_W, DN_V, D_MODEL, D_MODEL, LANES)
    row = lambda w: pl.BlockSpec((tm, w), lambda i: (i, 0))
    return pl.pallas_call(
        _inproj_kernel,
        grid=(n // tm,),
        in_specs=[row(D_MODEL), _wspec((D_MODEL, N_PACK)), _wspec((1, 2 * CONV_CH))],
        out_specs=[row(w) for w in widths],
        out_shape=[jax.ShapeDtypeStruct((n, w), F32) for w in widths],
        compiler_params=_params(("parallel",)),
        name="inproj",
    )(x, w_pack, b_glu)


def _merge_kernel(cvn_ref, on_ref, ga_ref, gb_ref, x_ref, wca_ref, wdn_ref, wo_ref, wxq_ref,
                  g1_ref, b1_ref, h_ref, xq_ref):
    br_a = _dot(cvn_ref[...], wca_ref[...])
    br_b = _dot(on_ref[0], wdn_ref[0:DN_DV, :])
    for hd in range(1, DN_HEADS):
        br_b = br_b + _dot(on_ref[hd], wdn_ref[hd * DN_DV:(hd + 1) * DN_DV, :])
    mixed = _sigmoid(ga_ref[...]) * br_a + _sigmoid(gb_ref[...]) * br_b
    h = _ln(DEEPNORM_ALPHA * x_ref[...] + _dot(mixed, wo_ref[...]), g1_ref[...], b1_ref[...])
    h_ref[...] = h
    xq_ref[...] = _dot(h, wxq_ref[...])


def _merge(cvn, on, ga, gb, x, wca, wdn, wo, wxq, g1, b1, tm):
    n = x.shape[0]
    row = lambda w: pl.BlockSpec((tm, w), lambda i: (i, 0))
    return pl.pallas_call(
        _merge_kernel,
        grid=(n // tm,),
        in_specs=[row(CONV_CH), pl.BlockSpec((DN_HEADS, tm, DN_DV), lambda i: (0, i, 0)), row(D_MODEL),
                  row(D_MODEL), row(D_MODEL),
                  _wspec((CONV_CH, D_MODEL)), _wspec((DN_V, D_MODEL)), _wspec((D_MODEL, D_MODEL)),
                  _wspec((D_MODEL, D_MODEL)), _wspec((1, D_MODEL)), _wspec((1, D_MODEL))],
        out_specs=[row(D_MODEL), row(D_MODEL)],
        out_shape=[jax.ShapeDtypeStruct((n, D_MODEL), F32)] * 2,
        compiler_params=_params(("parallel",)),
        name="merge",
    )(cvn, on, ga, gb, x, wca, wdn, wo, wxq, g1, b1)


FF_BLOCK = 1024


def _tail_kernel(att_ref, h_ref, wxo_ref, w1_ref, w2_ref, g2_ref, b2_ref, g3_ref, b3_ref, y_ref):
    xo = _dot(att_ref[...], wxo_ref[...])
    h2 = _ln(DEEPNORM_ALPHA * h_ref[...] + xo, g2_ref[...], b2_ref[...])
    h2b = h2.astype(BF16)
    ff = jnp.zeros(h2.shape, F32)
    for c in range(D_FF // FF_BLOCK):
        a = jnp.dot(h2b, w1_ref[:, c * FF_BLOCK:(c + 1) * FF_BLOCK], preferred_element_type=F32)
        a = jnp.square(jnp.maximum(a, 0.0))
        ff = ff + jnp.dot(a.astype(BF16), w2_ref[c * FF_BLOCK:(c + 1) * FF_BLOCK, :],
                          preferred_element_type=F32)
    y_ref[...] = _ln(DEEPNORM_ALPHA * h2 + ff, g3_ref[...], b3_ref[...])


def _tail(att, h, wxo, w1, w2, g2, b2, g3, b3, tm):
    n = h.shape[0]
    row = lambda w: pl.BlockSpec((tm, w), lambda i: (i, 0))
    return pl.pallas_call(
        _tail_kernel,
        grid=(n // tm,),
        in_specs=[row(D_MODEL), row(D_MODEL), _wspec((D_MODEL, D_MODEL)), _wspec((D_MODEL, D_FF)),
                  _wspec((D_FF, D_MODEL)), _wspec((1, D_MODEL)), _wspec((1, D_MODEL)),
                  _wspec((1, D_MODEL)), _wspec((1, D_MODEL))],
        out_specs=row(D_MODEL),
        out_shape=jax.ShapeDtypeStruct((n, D_MODEL), F32),
        compiler_params=_params(("parallel",)),
        name="tail",
    )(att, h, wxo, w1, w2, g2, b2, g3, b3)


def _memkv_kernel(m_ref, w_ref, k_ref, v_ref, kb_ref, vb_ref):
    m = m_ref[...].astype(BF16)
    tm = m.shape[0]
    k = jnp.dot(m, w_ref[:, :D_MODEL], preferred_element_type=F32)
    v = jnp.dot(m, w_ref[:, D_MODEL:], preferred_element_type=F32)
    k_ref[...] = k.reshape(tm, X_HEADS, X_HEAD_DIM)
    v_ref[...] = v.reshape(tm, X_HEADS, X_HEAD_DIM)
    kb_ref[...] = k.astype(BF16)
    vb_ref[...] = v.astype(BF16)


def _memkv(mem, w, tm):
    n = mem.shape[0]
    row = pl.BlockSpec((tm, D_MODEL), lambda i: (i, 0))
    row4 = pl.BlockSpec((tm, X_HEADS, X_HEAD_DIM), lambda i: (i, 0, 0))
    return pl.pallas_call(
        _memkv_kernel,
        grid=(n // tm,),
        in_specs=[row, _wspec((D_MODEL, 2 * D_MODEL))],
        out_specs=[row4, row4, row, row],
        out_shape=[jax.ShapeDtypeStruct((n, X_HEADS, X_HEAD_DIM), F32)] * 2
        + [jax.ShapeDtypeStruct((n, D_MODEL), BF16)] * 2,
        compiler_params=_params(("parallel",)),
        name="memkv",
    )(mem, w)


def _softmax(sc):
    sc = sc - jnp.max(sc, -1, keepdims=True)
    e = jnp.exp(sc)
    return e / jnp.sum(e, -1, keepdims=True)


def _ptail_kernel(q_ref, k_ref, v_ref, h_ref, wxo_ref, w1_ref, w2_ref, g2_ref, b2_ref, g3_ref, b3_ref,
                  y_ref, att_s):
    scale = X_HEAD_DIM ** -0.5
    for hd in range(X_HEADS):
        sl = slice(hd * X_HEAD_DIM, (hd + 1) * X_HEAD_DIM)
        pr = _softmax(_dot_nt(q_ref[:, sl], k_ref[0, :, sl]) * scale)
        att_s[:, sl] = _dot(pr, v_ref[0, :, sl]).astype(BF16)
    _tail_kernel(att_s, h_ref, wxo_ref, w1_ref, w2_ref, g2_ref, b2_ref, g3_ref, b3_ref, y_ref)


def _ptail(xq, kb, vb, h, wxo, w1, w2, g2, b2, g3, b3, batch, seq, tq):
    nt = seq // tq
    row = lambda: pl.BlockSpec((tq, D_MODEL), lambda b, t: (b * nt + t, 0))
    mem = pl.BlockSpec((1, N_MEM, D_MODEL), lambda b, t: (b, 0, 0))
    return pl.pallas_call(
        _ptail_kernel,
        grid=(batch, nt),
        in_specs=[row(), mem, mem, row(), _wspec((D_MODEL, D_MODEL)), _wspec((D_MODEL, D_FF)),
                  _wspec((D_FF, D_MODEL)), _wspec((1, D_MODEL)), _wspec((1, D_MODEL)),
                  _wspec((1, D_MODEL)), _wspec((1, D_MODEL))],
        out_specs=row(),
        out_shape=jax.ShapeDtypeStruct((batch * seq, D_MODEL), F32),
        scratch_shapes=[pltpu.VMEM((tq, D_MODEL), BF16)],
        compiler_params=_params(("parallel", "parallel")),
        name="prompt_tail",
    )(xq, kb, vb, h, wxo, w1, w2, g2, b2, g3, b3)


def _attn_split_kernel(q_ref, k_ref, v_ref, o_ref, *, nseq):
    scale = X_HEAD_DIM ** -0.5
    nl = q_ref.shape[1]
    nk = N_MEM * X_HEADS
    row_head = lax.broadcasted_iota(jnp.int32, (X_HEADS * nl, nk), 0) // nl
    col_head = lax.broadcasted_iota(jnp.int32, (X_HEADS * nl, nk), 1) % X_HEADS
    own = row_head == col_head
    for s in range(nseq):
        k2 = k_ref[s].reshape(nk, X_HEAD_DIM)
        v2 = v_ref[s].reshape(nk, X_HEAD_DIM)
        q = q_ref[s]
        q4 = jnp.concatenate([q[:, h * X_HEAD_DIM:(h + 1) * X_HEAD_DIM] for h in range(X_HEADS)], axis=0)
        pr = _softmax(jnp.where(own, _dot_nt(q4, k2) * scale, -1e30))
        o4 = _dot(pr, v2)
        for h in range(X_HEADS):
            o_ref[s, :, h * X_HEAD_DIM:(h + 1) * X_HEAD_DIM] = o4[h * nl:(h + 1) * nl]


def _attn(q, k, v, nseq):
    b, l, _ = q.shape
    kv_spec = pl.BlockSpec((None, nseq, N_MEM, X_HEADS, X_HEAD_DIM), lambda i: (0, i, 0, 0, 0))
    q_spec = pl.BlockSpec((nseq, l, D_MODEL), lambda i: (i, 0, 0))
    return pl.pallas_call(
        functools.partial(_attn_split_kernel, nseq=nseq),
        grid=(b // nseq,),
        in_specs=[q_spec, kv_spec, kv_spec],
        out_specs=q_spec,
        out_shape=jax.ShapeDtypeStruct((b, l, D_MODEL), F32),
        compiler_params=_params(("parallel",)),
        name="attn",
    )(q, k, v)


PM_TL = 512
ROW_STRIDE = 4
CONV_RB = 8 * ROW_STRIDE
CONV_PAD = 32
SHORT_PAD = 8


def _split3(x):
    h = x.astype(BF16)
    r = x - h.astype(F32)
    m = r.astype(BF16)
    l = (r - m.astype(F32)).astype(BF16)
    return h, m, l


def _dot_exact_lhs(lhs_bf16, x):
    h, m, l = _split3(x)
    d = lambda p: jnp.dot(lhs_bf16, p, preferred_element_type=F32)
    return d(h) + d(m) + d(l)


def _bmm(a, b):
    return jnp.einsum('bij,bjk->bik', a.astype(BF16), b.astype(BF16), preferred_element_type=F32)


def _bmm_nt(a, b):
    return jnp.einsum('bik,bjk->bij', a.astype(BF16), b.astype(BF16), preferred_element_type=F32)


def _bmm_tn(a, b):
    return jnp.einsum('bki,bkj->bij', a.astype(BF16), b.astype(BF16), preferred_element_type=F32)


def _strided_conv(src_ref, w_ref, dst_ref, slab, dslab, nblk, ntaps, off, post):
    ws = [w_ref[slab, j] for j in range(ntaps)]

    def body(rb, carry):
        r0 = pl.multiple_of(rb * CONV_RB, CONV_RB)
        accs = [None] * ROW_STRIDE
        for q in range(off, off + ntaps + ROW_STRIDE - 1):
            x = src_ref[slab, pl.ds(r0 + q, 8, stride=ROW_STRIDE), :]
            for m in range(ROW_STRIDE):
                j = q - off - m
                if 0 <= j < ntaps:
                    term = ws[j] * x
                    accs[m] = term if accs[m] is None else accs[m] + term
        for m in range(ROW_STRIDE):
            dst_ref[dslab, pl.ds(r0 + m, 8, stride=ROW_STRIDE), :] = post(accs[m])
        return carry

    lax.fori_loop(0, nblk, body, 0, unroll=True)


def _pfront_kernel(x_ref, w_ref, bglu_ref, wdw_ref, bdw_ref, lcg_ref, lcb_ref, wsh_ref, alog_ref, dtb_ref,
                   dng_ref, wca_ref, wdn_ref, wo_ref, wxq_ref, g1_ref, b1_ref,
                   h_ref, xq_ref, pconv_ref, pqkv_ref, sout_ref,
                   xx, qx, ya, qs, ks, vs, S, ktw_s, n_s, q2_s, o1_s, sl_s, egc_s, cvn_s, on_s, og_s, ga_s, gb_s):
    t = pl.program_id(1)
    nt = pl.num_programs(1)
    tl = PM_TL
    c = CHUNK
    nc = tl // c
    n_cs = CONV_CH // LANES
    n_qs = QKV_W // LANES

    @pl.when(t == 0)
    def _():
        xx[:, 0:CONV_PAD, :] = jnp.zeros((n_cs, CONV_PAD, LANES), F32)
        qx[:, 0:SHORT_PAD, :] = jnp.zeros((n_qs, SHORT_PAD, LANES), F32)
        S[...] = jnp.zeros(S.shape, F32)

    xb = x_ref[...].astype(BF16)
    proj = lambda lo, hi: jnp.dot(xb, w_ref[:, lo:hi], preferred_element_type=F32)
    conv_a = lambda sl: _strided_conv(xx, wdw_ref, ya, sl, sl, tl // CONV_RB, CONV_K,
                                      CONV_PAD - (CONV_K - 1), lambda y: y)

    def conv_b(part):
        for hd in range(DN_HEADS):
            _strided_conv(qx, wsh_ref, (qs, ks, vs)[part], part * DN_HEADS + hd, hd, tl // CONV_RB, SHORT_K,
                          SHORT_PAD - (SHORT_K - 1), _silu)

    def qkv_part(part):
        y = proj(OFF_QKV + part * DN_QK, OFF_QKV + (part + 1) * DN_QK)
        for hd in range(DN_HEADS):
            qx[part * DN_HEADS + hd, SHORT_PAD:SHORT_PAD + tl, :] = y[:, hd * LANES:(hd + 1) * LANES]

    glu = proj(0, OFF_QKV) + bglu_ref[...]
    u = glu[:, :CONV_CH] * _sigmoid(glu[:, CONV_CH:])
    for sl in range(n_cs):
        xx[sl, CONV_PAD:CONV_PAD + tl, :] = u[:, sl * LANES:(sl + 1) * LANES]
    qkv_part(0)
    conv_a(0)
    qkv_part(1)
    conv_a(1)
    qkv_part(2)
    conv_a(2)
    og_s[...] = proj(OFF_OG, OFF_GA)
    conv_a(3)
    ga_s[:, :DN_V] = proj(OFF_GA, OFF_GA + DN_V)
    conv_b(0)
    ga_s[:, DN_V:] = proj(OFF_GA + DN_V, OFF_GB)
    conv_b(1)
    gb_s[:, :DN_V] = proj(OFF_GB, OFF_GB + DN_V)
    conv_b(2)
    gb_s[:, DN_V:] = proj(OFF_GB + DN_V, OFF_BD)
    bd = proj(OFF_BD, N_PACK)
    for rb in range(tl // CHUNK):
        r0 = rb * CHUNK
        cv = jnp.concatenate([ya[sl, r0:r0 + CHUNK, :] for sl in range(n_cs)], axis=1) + bdw_ref[...]
        cvn_s[r0:r0 + CHUNK, :] = _silu(_ln(cv, lcg_ref[...], lcb_ref[...])).astype(BF16)

    beta = _sigmoid(bd)
    ld = -jnp.exp(alog_ref[...]) * _softplus(bd + dtb_ref[...])
    rt = lax.broadcasted_iota(jnp.int32, (tl, tl), 0)
    ct = lax.broadcasted_iota(jnp.int32, (tl, tl), 1)
    blk_tril = jnp.where((rt >= ct) & ((rt // c) == (ct // c)), 1.0, 0.0).astype(BF16)
    gsum = _dot_exact_lhs(blk_tril, ld)
    beta3 = beta.reshape(nc, c, LANES)
    gsum3 = gsum.reshape(nc, c, LANES)
    gsum_t = gsum.T
    glast3 = gsum3[:, c - 1:c, :]
    eg3 = jnp.exp(gsum3)
    etail3 = jnp.exp(glast3 - gsum3)
    egc3 = jnp.exp(glast3)

    ri = lax.broadcasted_iota(jnp.int32, (c, c), 0)
    ci = lax.broadcasted_iota(jnp.int32, (c, c), 1)
    tril = (ri >= ci)[None]
    strict = (ri > ci)[None]
    eye = jnp.where(ri == ci, 1.0, 0.0)[None]

    for hd in range(DN_HEADS):
        lb = slice(hd, hd + 1)
        lg = slice(DN_HEADS + hd, DN_HEADS + hd + 1)
        q = _l2n(qs[hd].reshape(nc, c, DN_DK)) * (DN_DK ** -0.5)
        k = _l2n(ks[hd].reshape(nc, c, DN_DK))
        v = vs[hd].reshape(nc, c, DN_DV)
        b_col = beta3[:, :, lb]
        g_row = jnp.stack([gsum_t[DN_HEADS + hd:DN_HEADS + hd + 1, ch * c:(ch + 1) * c] for ch in range(nc)])
        diff = gsum3[:, :, lg] - g_row
        decay = jnp.where(tril, jnp.exp(jnp.where(tril, diff, 0.0)), 0.0)
        kq = _bmm_nt(jnp.concatenate([k, q], axis=1), k)
        a = jnp.where(strict, b_col * kq[:, :c] * decay, 0.0)
        aqk = kq[:, c:] * decay
        p = eye - a
        x = a
        for _ in range(5):
            x = _bmm(x, x)
            p = p + _bmm(p, x)
        rhs = jnp.concatenate([(b_col * eg3[:, :, lg]) * k, b_col * v], axis=2)
        sol = _bmm(p, rhs)
        qo = _bmm(aqk, sol)
        q2_s[hd] = (eg3[:, :, lg] * q - qo[:, :, :DN_DK]).astype(BF16)
        o1_s[hd] = qo[:, :, DN_DK:]
        kn = _bmm_tn(k * etail3[:, :, lg], sol)
        ktw_s[hd] = kn[:, :, :DN_DK].astype(BF16)
        n_s[hd] = kn[:, :, DN_DK:]
        egc_s[hd] = jnp.broadcast_to(egc3[:, :, lg], (nc, 1, DN_DV))

    def carry_body(ch, carry):
        for hd in range(DN_HEADS):
            s_h = S[hd]
            s_b = s_h.astype(BF16)
            sl_s[hd, ch] = s_b
            S[hd] = egc_s[hd, ch] * s_h + (n_s[hd, ch] - jnp.dot(ktw_s[hd, ch], s_b,
                                                              preferred_element_type=F32))
        return carry

    for ch in range(nc):
        carry_body(ch, 0)

    for hd in range(DN_HEADS):
        ls = slice(hd * DN_DK, (hd + 1) * DN_DK)
        o = o1_s[hd] + jnp.einsum('bck,bkv->bcv', q2_s[hd], sl_s[hd], preferred_element_type=F32)
        o = o * lax.rsqrt(jnp.mean(o * o, -1, keepdims=True) + RMS_EPS) * dng_ref[...]
        on_s[:, ls] = (o.reshape(tl, DN_DV) * _silu(og_s[:, ls])).astype(BF16)

    mixed = _sigmoid(ga_s[...]) * jnp.dot(cvn_s[...], wca_ref[...], preferred_element_type=F32)
    mixed = mixed + _sigmoid(gb_s[...]) * jnp.dot(on_s[...], wdn_ref[...], preferred_element_type=F32)
    h = _ln(DEEPNORM_ALPHA * x_ref[...] + _dot(mixed, wo_ref[...]), g1_ref[...], b1_ref[...])
    h_ref[...] = h
    xq_ref[...] = _dot(h, wxq_ref[...]).astype(BF16)

    @pl.when(t == nt - 1)
    def _():
        for sl in range(n_cs):
            pconv_ref[0, :, sl * LANES:(sl + 1) * LANES] = xx[sl, tl:tl + CONV_PAD, :]
        for s12 in range(n_qs):
            pqkv_ref[0, :, s12 * LANES:(s12 + 1) * LANES] = qx[s12, tl:tl + SHORT_PAD, :]
        sout_ref[0] = S[...]

    for sl in range(n_cs):
        xx[sl, 0:CONV_PAD, :] = xx[sl, tl:tl + CONV_PAD, :]
    for s12 in range(n_qs):
        qx[s12, 0:SHORT_PAD, :] = qx[s12, tl:tl + SHORT_PAD, :]


def _pfront(x, w_pack, bglu, wdw, bdw, lcg, lcb, wsh, alog, dtb, dng, wca, wdn, wo, wxq, g1, b1, batch, seq):
    tl = PM_TL
    nt = seq // tl
    nc = tl // CHUNK
    n = batch * seq
    row = lambda w: pl.BlockSpec((tl, w), lambda b, t: (b * nt + t, 0))
    per_seq = lambda *s: pl.BlockSpec((1,) + s, lambda b, t: (b,) + (0,) * len(s))
    return pl.pallas_call(
        _pfront_kernel,
        grid=(batch, nt),
        in_specs=[row(D_MODEL), _wspec((D_MODEL, N_PACK)), _wspec((1, 2 * CONV_CH)),
                  _wspec((CONV_CH // LANES, CONV_K, 8, LANES)), _wspec((1, CONV_CH)), _wspec((1, CONV_CH)),
                  _wspec((1, CONV_CH)), _wspec((QKV_W // LANES, SHORT_K, 8, LANES)), _wspec((1, LANES)),
                  _wspec((1, LANES)), _wspec((1, DN_DV)),
                  _wspec((CONV_CH, D_MODEL)), _wspec((DN_V, D_MODEL)), _wspec((D_MODEL, D_MODEL)),
                  _wspec((D_MODEL, D_MODEL)), _wspec((1, D_MODEL)), _wspec((1, D_MODEL))],
        out_specs=[row(D_MODEL), row(D_MODEL), per_seq(CONV_PAD, CONV_CH), per_seq(SHORT_PAD, QKV_W),
                   per_seq(DN_HEADS, DN_DK, DN_DV)],
        out_shape=[jax.ShapeDtypeStruct((n, D_MODEL), F32), jax.ShapeDtypeStruct((n, D_MODEL), BF16),
                   jax.ShapeDtypeStruct((batch, CONV_PAD, CONV_CH), F32),
                   jax.ShapeDtypeStruct((batch, SHORT_PAD, QKV_W), F32),
                   jax.ShapeDtypeStruct((batch, DN_HEADS, DN_DK, DN_DV), F32)],
        scratch_shapes=[pltpu.VMEM((CONV_CH // LANES, tl + CONV_PAD, LANES), F32),
                        pltpu.VMEM((QKV_W // LANES, tl + SHORT_PAD, LANES), F32),
                        pltpu.VMEM((CONV_CH // LANES, tl, LANES), F32),
                        pltpu.VMEM((DN_HEADS, tl, DN_DK), F32), pltpu.VMEM((DN_HEADS, tl, DN_DK), F32),
                        pltpu.VMEM((DN_HEADS, tl, DN_DV), F32),
                        pltpu.VMEM((DN_HEADS, DN_DK, DN_DV), F32),
                        pltpu.VMEM((DN_HEADS, nc, DN_DK, DN_DK), BF16),
                        pltpu.VMEM((DN_HEADS, nc, DN_DK, DN_DV), F32),
                        pltpu.VMEM((DN_HEADS, nc, CHUNK, DN_DK), BF16),
                        pltpu.VMEM((DN_HEADS, nc, CHUNK, DN_DV), F32),
                        pltpu.VMEM((DN_HEADS, nc, DN_DK, DN_DV), BF16),
                        pltpu.VMEM((DN_HEADS, nc, 1, DN_DV), F32),
                        pltpu.VMEM((tl, CONV_CH), BF16), pltpu.VMEM((tl, DN_V), BF16),
                        pltpu.VMEM((tl, DN_V), F32), pltpu.VMEM((tl, D_MODEL), F32),
                        pltpu.VMEM((tl, D_MODEL), F32)],
        compiler_params=_params(("parallel", "arbitrary")),
        name="prompt_front",
    )(x, w_pack, bglu, wdw, bdw, lcg, lcb, wsh, alog, dtb, dng, wca, wdn, wo, wxq, g1, b1)


SM_BT = 32
SM_BS = 8


def _smid1_kernel(u_ref, qkv_ref, bd_ref, og_ref, hc_ref, hq_ref, wdw_ref, bdw_ref, lcg_ref, lcb_ref, wsh_ref,
                  alog_ref, dtb_ref,
                  cvn_ref, cnew_ref, qnew_ref, w_ref, q2_ref, uv_ref, o1_ref, kt_ref, og4_ref, egc_ref):
    nl = u_ref.shape[0]
    nh = CONV_K - 1
    xx = [hc_ref[:, i, :] for i in range(nh)] + [u_ref[i] for i in range(nl)]
    for t in range(nl):
        acc = wdw_ref[0:1, :] * xx[t]
        for j in range(1, CONV_K):
            acc = acc + wdw_ref[j:j + 1, :] * xx[t + j]
        cvn_ref[t] = _silu(_ln(acc + bdw_ref[...], lcg_ref[...], lcb_ref[...]))
    cnew_ref[:, 0:nh - nl, :] = hc_ref[:, nl:nh, :]
    for t in range(nl):
        cnew_ref[:, nh - nl + t, :] = xx[nh + t]

    ns = SHORT_K - 1
    qq = [hq_ref[:, i, :] for i in range(ns)] + [qkv_ref[i] for i in range(nl)]
    for i in range(ns):
        qnew_ref[:, i, :] = qq[nl + i]
    qkv = []
    for t in range(nl):
        acc = wsh_ref[0:1, :] * qq[t]
        for j in range(1, SHORT_K):
            acc = acc + wsh_ref[j:j + 1, :] * qq[t + j]
        qkv.append(_silu(acc))

    beta = [_sigmoid(bd_ref[t]) for t in range(nl)]
    ld = [-jnp.exp(alog_ref[...]) * _softplus(bd_ref[t] + dtb_ref[...]) for t in range(nl)]

    for hd in range(DN_HEADS):
        q = [_l2n(qkv[t][:, hd * DN_DK:(hd + 1) * DN_DK]) * (DN_DK ** -0.5) for t in range(nl)]
        k = [_l2n(qkv[t][:, DN_QK + hd * DN_DK:DN_QK + (hd + 1) * DN_DK]) for t in range(nl)]
        v = [qkv[t][:, 2 * DN_QK + hd * DN_DV:2 * DN_QK + (hd + 1) * DN_DV] for t in range(nl)]
        b = [beta[t][:, hd:hd + 1] for t in range(nl)]
        g = [ld[t][:, DN_HEADS + hd:DN_HEADS + hd + 1] for t in range(nl)]
        gc = [g[0]]
        for t in range(1, nl):
            gc.append(gc[t - 1] + g[t])
        w_l, uv_l = [], []
        for i in range(nl):
            wi = (b[i] * jnp.exp(gc[i])) * k[i]
            ui = b[i] * v[i]
            for j in range(i):
                a_ij = b[i] * jnp.sum(k[i] * k[j], -1, keepdims=True) * jnp.exp(gc[i] - gc[j])
                wi = wi - a_ij * w_l[j]
                ui = ui - a_ij * uv_l[j]
            w_l.append(wi)
            uv_l.append(ui)
        for i in range(nl):
            o1 = jnp.zeros_like(v[i])
            q2 = jnp.exp(gc[i]) * q[i]
            for j in range(i + 1):
                aqk = jnp.sum(q[i] * k[j], -1, keepdims=True) * jnp.exp(gc[i] - gc[j])
                o1 = o1 + aqk * uv_l[j]
                q2 = q2 - aqk * w_l[j]
            w_ref[hd, i] = w_l[i]
            uv_ref[hd, i] = uv_l[i]
            o1_ref[hd, i] = o1
            q2_ref[hd, i] = q2
            kt_ref[hd, i] = k[i] * jnp.exp(gc[nl - 1] - gc[i])
            og4_ref[hd, i] = og_ref[i, :, hd * DN_DV:(hd + 1) * DN_DV]
        egc_ref[hd] = jnp.broadcast_to(jnp.exp(gc[nl - 1]), (u_ref.shape[1], LANES))


def _smid1(u, qkv, bd, og, hc, hq, wdw, bdw, lcg, lcb, wsh, alog, dtb):
    nl, b, _ = u.shape
    bt = SM_BT
    tm = lambda r, w: pl.BlockSpec((r, bt, w), lambda i: (0, i, 0))
    st = lambda r, w: pl.BlockSpec((None, bt, r, w), lambda i: (0, i, 0, 0))
    hm = pl.BlockSpec((DN_HEADS, nl, bt, DN_DK), lambda i: (0, 0, i, 0))
    hm_shape = jax.ShapeDtypeStruct((DN_HEADS, nl, b, DN_DK), F32)
    return pl.pallas_call(
        _smid1_kernel,
        grid=(b // bt,),
        in_specs=[tm(nl, CONV_CH), tm(nl, QKV_W), tm(nl, LANES), tm(nl, DN_V), st(CONV_K - 1, CONV_CH),
                  st(SHORT_K - 1, QKV_W),
                  _wspec((CONV_PAD, CONV_CH)), _wspec((1, CONV_CH)), _wspec((1, CONV_CH)),
                  _wspec((1, CONV_CH)), _wspec((SHORT_K, QKV_W)), _wspec((1, LANES)),
                  _wspec((1, LANES))],
        out_specs=[tm(nl, CONV_CH), st(CONV_K - 1, CONV_CH), st(SHORT_K - 1, QKV_W)] + [hm] * 6
        + [tm(DN_HEADS, LANES)],
        out_shape=[jax.ShapeDtypeStruct((nl, b, CONV_CH), F32),
                   jax.ShapeDtypeStruct((1, b, CONV_K - 1, CONV_CH), F32),
                   jax.ShapeDtypeStruct((1, b, SHORT_K - 1, QKV_W), F32)] + [hm_shape] * 6
        + [jax.ShapeDtypeStruct((DN_HEADS, b, LANES), F32)],
        compiler_params=_params(("parallel",)),
        name="sample_mid1",
    )(u, qkv, bd, og, hc, hq, wdw, bdw, lcg, lcb, wsh, alog, dtb)


def _smid2_kernel(w_ref, q2_ref, uv_ref, o1_ref, kt_ref, og_ref, egc_ref, s_ref, dng_ref,
                  on_ref, snew_ref):
    nl, bs = w_ref.shape[1], w_ref.shape[2]
    seq_major = lambda ref, hd: jnp.stack([ref[hd, :, b, :] for b in range(bs)])
    for hd in range(DN_HEADS):
        s_h = s_ref[:, hd]
        lhs = jnp.concatenate([seq_major(w_ref, hd), seq_major(q2_ref, hd)], axis=1)
        r = jnp.einsum('bck,bkv->bcv', lhs.astype(BF16), s_h.astype(BF16),
                       preferred_element_type=F32)
        u_new = seq_major(uv_ref, hd) - r[:, :nl]
        o = seq_major(o1_ref, hd) + r[:, nl:]
        upd = jnp.einsum('bck,bcv->bkv', seq_major(kt_ref, hd).astype(BF16), u_new.astype(BF16),
                         preferred_element_type=F32)
        snew_ref[:, hd] = egc_ref[:, hd] * s_h + upd
        o = o * lax.rsqrt(jnp.mean(o * o, -1, keepdims=True) + RMS_EPS) * dng_ref[...]
        o = o * _silu(seq_major(og_ref, hd))
        for b in range(bs):
            on_ref[hd, :, b, :] = o[b]


def _smid2(w, q2, uv, o1, kt, og, egc, s, dng):
    _, nl, b, _ = w.shape
    bs = SM_BS
    hm = pl.BlockSpec((DN_HEADS, nl, bs, DN_DK), lambda i: (0, 0, i, 0))
    st = pl.BlockSpec((None, bs, DN_HEADS, DN_DK, DN_DV), lambda i: (0, i, 0, 0, 0))
    return pl.pallas_call(
        _smid2_kernel,
        grid=(b // bs,),
        in_specs=[hm] * 6 + [pl.BlockSpec((bs, DN_HEADS, 1, LANES), lambda i: (i, 0, 0, 0)), st,
                             _wspec((1, DN_DV))],
        out_specs=[hm, st],
        out_shape=[jax.ShapeDtypeStruct((DN_HEADS, nl, b, DN_DV), F32),
                   jax.ShapeDtypeStruct((1, b, DN_HEADS, DN_DK, DN_DV), F32)],
        compiler_params=_params(("parallel",)),
        name="sample_mid2",
    )(w, q2, uv, o1, kt, og, egc, s, dng)


def _tm(x):
    return jnp.swapaxes(x, 0, 1)


def kernel(x_prompt, x_sample, mem_prompt, state_conv, state_qkv_conv, state_delta, cache_mem_k, cache_mem_v, w_in, b_glu, w_dw, b_dw, ln_conv_g, ln_conv_b, w_conv_out, w_short, a_log, dt_bias, dn_norm_g, w_dn_out, w_o, ln1_g, ln1_b, w_xq, w_mem_kv, w_xo, ln2_g, ln2_b, w_ff1, w_ff2, ln3_g, ln3_b):
    assert w_in.shape[0] == DEPTH == 1
    bp, lp, _ = x_prompt.shape
    bsm, ls_, _ = x_sample.shape

    w = w_in[0]
    o = (2 * CONV_CH, 2 * CONV_CH + QKV_W, 2 * CONV_CH + QKV_W + DN_V)
    o_beta, o_dec, o_ga = o[2], o[2] + DN_HEADS, o[2] + 2 * DN_HEADS
    w_pack = jnp.concatenate(
        [w[:, :o[2]], w[:, o_ga:o_ga + 2 * D_MODEL], w[:, o_beta:o_ga],
         jnp.zeros((D_MODEL, LANES - 2 * DN_HEADS), w.dtype)], axis=1).astype(BF16)
    lane_pad = lambda a: jnp.concatenate(
        [jnp.zeros((DN_HEADS,), F32), a.astype(F32), jnp.zeros((LANES - 2 * DN_HEADS,), F32)])[None]
    alog = lane_pad(a_log[0])
    dtb = lane_pad(dt_bias[0])
    wdw = jnp.concatenate([w_dw[0], jnp.zeros((CONV_PAD - CONV_K, CONV_CH), F32)], axis=0)
    r2 = lambda a: a[0][None]
    bglu, bdw, lcg, lcb = r2(b_glu), r2(b_dw), r2(ln_conv_g), r2(ln_conv_b)
    dng = r2(dn_norm_g)
    wsh = w_short[0]
    slabs = lambda a: jnp.broadcast_to(
        jnp.swapaxes(a.reshape(a.shape[0], a.shape[1] // LANES, LANES), 0, 1)[:, :, None, :],
        (a.shape[1] // LANES, a.shape[0], 8, LANES))
    wdw_slab = slabs(w_dw[0])
    wsh_slab = slabs(wsh)
    wca, wdn, wo, wxq, wxo = (a[0].astype(BF16) for a in (w_conv_out, w_dn_out, w_o, w_xq, w_xo))
    w1, w2, wkv = w_ff1[0].astype(BF16), w_ff2[0].astype(BF16), w_mem_kv[0].astype(BF16)
    g1, b1, g2, b2, g3, b3 = (r2(a) for a in (ln1_g, ln1_b, ln2_g, ln2_b, ln3_g, ln3_b))

    n_p = bp * lp
    xp = x_prompt.reshape(n_p, D_MODEL)
    mk, mv, mk_b, mv_b = _memkv(mem_prompt.reshape(bp * N_MEM, D_MODEL), wkv, 512)
    h, xq, p_conv, p_qkv, p_delta = _pfront(xp, w_pack, bglu, wdw_slab, bdw, lcg, lcb, wsh_slab, alog, dtb, dng,
                                            wca, wdn, wo, wxq, g1, b1, bp, lp)
    y_p = _ptail(xq, mk_b.reshape(bp, N_MEM, D_MODEL), mv_b.reshape(bp, N_MEM, D_MODEL), h,
                 wxo, w1, w2, g2, b2, g3, b3, bp, lp, 512).reshape(bp, lp, D_MODEL)
    p_conv = p_conv[:, CONV_PAD - (CONV_K - 1):]
    p_qkv = p_qkv[:, SHORT_PAD - (SHORT_K - 1):]
    p_mk = mk.reshape(bp, N_MEM, X_HEADS, X_HEAD_DIM)
    p_mv = mv.reshape(bp, N_MEM, X_HEADS, X_HEAD_DIM)

    n_s = bsm * ls_
    xs = _tm(x_sample).reshape(n_s, D_MODEL)
    u, qkv, og, ga, gb, bd = _inproj(xs, w_pack, bglu, 256)
    t3 = lambda a: a.reshape(ls_, bsm, a.shape[-1])
    cvn, c_new, q_new, w_, q2, uv, o1, kt, og4, egc = _smid1(
        t3(u), t3(qkv), t3(bd), t3(og), state_conv, state_qkv_conv, wdw, bdw, lcg, lcb, wsh, alog, dtb)
    on4, s_delta = _smid2(w_, q2, uv, o1, kt, og4, _tm(egc)[:, :, None, :], state_delta, dng)
    h, xq = _merge(cvn.reshape(n_s, CONV_CH), on4.reshape(DN_HEADS, n_s, DN_DV), ga, gb, xs,
                   wca, wdn, wo, wxq, g1, b1, 512)
    att = _attn(_tm(t3(xq)), cache_mem_k, cache_mem_v, 4)
    y_s = _tail(_tm(att).reshape(n_s, D_MODEL), h, wxo, w1, w2, g2, b2, g3, b3, 512)
    y_s = _tm(t3(y_s))

    return (y_p, y_s, p_conv[None], p_qkv[None], p_delta[None], p_mk[None], p_mv[None],
            c_new, q_new, s_delta)
```

```python
import functools

import jax
import jax.numpy as jnp
from jax import lax
from jax.experimental import pallas as pl
from jax.experimental.pallas import tpu as pltpu

F32 = jnp.float32
BF16 = jnp.bfloat16

D_MODEL = 1024
N_MEM = 256
CONV_CH = 512
CONV_K = 31
DN_HEADS = 4
DN_DK = 128
DN_DV = 128
DN_QK = DN_HEADS * DN_DK
DN_V = DN_HEADS * DN_DV
QKV_W = 2 * DN_QK + DN_V
SHORT_K = 4
CHUNK = 64
X_HEADS = 4
X_HEAD_DIM = D_MODEL // X_HEADS
D_FF = 4 * D_MODEL
DEPTH = 1
DEEPNORM_ALPHA = (2 * DEPTH) ** 0.25
LN_EPS = 1e-5
RMS_EPS = 1e-6
L2_EPS = 1e-6

LANES = 128
OFF_QKV = 2 * CONV_CH
OFF_OG = OFF_QKV + QKV_W
N_A = OFF_OG + DN_V
N_G = 2 * D_MODEL
VMEM_LIMIT = 56 * 1024 * 1024


def _dot(a, b):
    return jnp.dot(a.astype(BF16), b.astype(BF16), preferred_element_type=F32)


def _dot_nt(a, b):
    return lax.dot_general(a.astype(BF16), b.astype(BF16), (((1,), (1,)), ((), ())),
                           preferred_element_type=F32)


def _sigmoid(x):
    return 1.0 / (1.0 + jnp.exp(-x))


def _silu(x):
    return x * _sigmoid(x)


def _softplus(x):
    return jnp.maximum(x, 0.0) + jnp.log(1.0 + jnp.exp(-jnp.abs(x)))


def _ln(x, g, b):
    mu = jnp.mean(x, -1, keepdims=True)
    xc = x - mu
    var = jnp.mean(xc * xc, -1, keepdims=True)
    return xc * lax.rsqrt(var + LN_EPS) * g + b


def _l2n(x):
    return x * lax.rsqrt(jnp.sum(x * x, -1, keepdims=True) + L2_EPS)


def _wspec(shape):
    return pl.BlockSpec(shape, lambda *_: (0,) * len(shape), pipeline_mode=pl.Buffered(1))


def _params(sem):
    return pltpu.CompilerParams(dimension_semantics=sem, vmem_limit_bytes=VMEM_LIMIT)


def _inproj_kernel(x_ref, wa_ref, wg_ref, wbd_ref, bglu_ref, u_ref, qkv_ref, og_ref, ga_ref, gb_ref, bd_ref):
    x = x_ref[...].astype(BF16)
    glu = jnp.dot(x, wa_ref[:, 0:OFF_QKV], preferred_element_type=F32) + bglu_ref[...]
    u_ref[...] = glu[:, :CONV_CH] * _sigmoid(glu[:, CONV_CH:])
    qkv_ref[...] = jnp.dot(x, wa_ref[:, OFF_QKV:OFF_OG], preferred_element_type=F32)
    og_ref[...] = jnp.dot(x, wa_ref[:, OFF_OG:N_A], preferred_element_type=F32)
    ga_ref[...] = jnp.dot(x, wg_ref[:, :D_MODEL], preferred_element_type=F32)
    gb_ref[...] = jnp.dot(x, wg_ref[:, D_MODEL:], preferred_element_type=F32)
    bd_ref[...] = jnp.dot(x, wbd_ref[...], preferred_element_type=F32)


def _inproj(x, w_a, w_g, w_bd, b_glu, tm):
    n = x.shape[0]
    widths = (CONV_CH, QKV_W, DN_V, D_MODEL, D_MODEL, LANES)
    row = lambda w: pl.BlockSpec((tm, w), lambda i: (i, 0))
    return pl.pallas_call(
        _inproj_kernel,
        grid=(n // tm,),
        in_specs=[row(D_MODEL), _wspec((D_MODEL, N_A)), _wspec((D_MODEL, N_G)), _wspec((D_MODEL, LANES)),
                  _wspec((1, 2 * CONV_CH))],
        out_specs=[row(w) for w in widths],
        out_shape=[jax.ShapeDtypeStruct((n, w), F32) for w in widths],
        compiler_params=_params(("parallel",)),
        name="inproj",
    )(x, w_a, w_g, w_bd, b_glu)


def _merge_kernel(cvn_ref, on_ref, ga_ref, gb_ref, x_ref, wca_ref, wdn_ref, wo_ref, wxq_ref,
                  g1_ref, b1_ref, h_ref, xq_ref):
    br_a = _dot(cvn_ref[...], wca_ref[...])
    br_b = _dot(on_ref[...], wdn_ref[...])
    mixed = _sigmoid(ga_ref[...]) * br_a + _sigmoid(gb_ref[...]) * br_b
    h = _ln(DEEPNORM_ALPHA * x_ref[...] + _dot(mixed, wo_ref[...]), g1_ref[...], b1_ref[...])
    h_ref[...] = h
    xq_ref[...] = _dot(h, wxq_ref[...])


def _merge(cvn, on, ga, gb, x, wca, wdn, wo, wxq, g1, b1, tm):
    n = x.shape[0]
    row = lambda w: pl.BlockSpec((tm, w), lambda i: (i, 0))
    return pl.pallas_call(
        _merge_kernel,
        grid=(n // tm,),
        in_specs=[row(CONV_CH), row(DN_V), row(D_MODEL), row(D_MODEL), row(D_MODEL),
                  _wspec((CONV_CH, D_MODEL)), _wspec((DN_V, D_MODEL)), _wspec((D_MODEL, D_MODEL)),
                  _wspec((D_MODEL, D_MODEL)), _wspec((1, D_MODEL)), _wspec((1, D_MODEL))],
        out_specs=[row(D_MODEL), row(D_MODEL)],
        out_shape=[jax.ShapeDtypeStruct((n, D_MODEL), F32)] * 2,
        compiler_params=_params(("parallel",)),
        name="merge",
    )(cvn, on, ga, gb, x, wca, wdn, wo, wxq, g1, b1)


FF_BLOCK = 1024


def _tail_kernel(att_ref, h_ref, wxo_ref, w1_ref, w2_ref, g2_ref, b2_ref, g3_ref, b3_ref, y_ref):
    xo = _dot(att_ref[...], wxo_ref[...])
    h2 = _ln(DEEPNORM_ALPHA * h_ref[...] + xo, g2_ref[...], b2_ref[...])
    h2b = h2.astype(BF16)
    ff = jnp.zeros(h2.shape, F32)
    for c in range(D_FF // FF_BLOCK):
        a = jnp.dot(h2b, w1_ref[:, c * FF_BLOCK:(c + 1) * FF_BLOCK], preferred_element_type=F32)
        a = jnp.square(jnp.maximum(a, 0.0))
        ff = ff + jnp.dot(a.astype(BF16), w2_ref[c * FF_BLOCK:(c + 1) * FF_BLOCK, :],
                          preferred_element_type=F32)
    y_ref[...] = _ln(DEEPNORM_ALPHA * h2 + ff, g3_ref[...], b3_ref[...])


def _tail(att, h, wxo, w1, w2, g2, b2, g3, b3, tm):
    n = h.shape[0]
    row = lambda w: pl.BlockSpec((tm, w), lambda i: (i, 0))
    return pl.pallas_call(
        _tail_kernel,
        grid=(n // tm,),
        in_specs=[row(D_MODEL), row(D_MODEL), _wspec((D_MODEL, D_MODEL)), _wspec((D_MODEL, D_FF)),
                  _wspec((D_FF, D_MODEL)), _wspec((1, D_MODEL)), _wspec((1, D_MODEL)),
                  _wspec((1, D_MODEL)), _wspec((1, D_MODEL))],
        out_specs=row(D_MODEL),
        out_shape=jax.ShapeDtypeStruct((n, D_MODEL), F32),
        compiler_params=_params(("parallel",)),
        name="tail",
    )(att, h, wxo, w1, w2, g2, b2, g3, b3)


def _memkv_kernel(m_ref, w_ref, k_ref, v_ref, kb_ref, vb_ref):
    m = m_ref[...].astype(BF16)
    tm = m.shape[0]
    k = jnp.dot(m, w_ref[:, :D_MODEL], preferred_element_type=F32)
    v = jnp.dot(m, w_ref[:, D_MODEL:], preferred_element_type=F32)
    k_ref[...] = k.reshape(tm, X_HEADS, X_HEAD_DIM)
    v_ref[...] = v.reshape(tm, X_HEADS, X_HEAD_DIM)
    kb_ref[...] = k.astype(BF16)
    vb_ref[...] = v.astype(BF16)


def _memkv(mem, w, tm):
    n = mem.shape[0]
    row = pl.BlockSpec((tm, D_MODEL), lambda i: (i, 0))
    row4 = pl.BlockSpec((tm, X_HEADS, X_HEAD_DIM), lambda i: (i, 0, 0))
    return pl.pallas_call(
        _memkv_kernel,
        grid=(n // tm,),
        in_specs=[row, _wspec((D_MODEL, 2 * D_MODEL))],
        out_specs=[row4, row4, row, row],
        out_shape=[jax.ShapeDtypeStruct((n, X_HEADS, X_HEAD_DIM), F32)] * 2
        + [jax.ShapeDtypeStruct((n, D_MODEL), BF16)] * 2,
        compiler_params=_params(("parallel",)),
        name="memkv",
    )(mem, w)


def _softmax(sc):
    sc = sc - jnp.max(sc, -1, keepdims=True)
    e = jnp.exp(sc)
    return e / jnp.sum(e, -1, keepdims=True)


def _ptail_kernel(q_ref, k_ref, v_ref, h_ref, wxo_ref, w1_ref, w2_ref, g2_ref, b2_ref, g3_ref, b3_ref,
                  sq_ref, sk_ref, sv_ref, y_ref, so_ref, att_s, *, nseq):
    scale = X_HEAD_DIM ** -0.5
    for hd in range(X_HEADS):
        sl = slice(hd * X_HEAD_DIM, (hd + 1) * X_HEAD_DIM)
        pr = _softmax(_dot_nt(q_ref[:, sl], k_ref[0, :, sl]) * scale)
        att_s[:, sl] = _dot(pr, v_ref[0, :, sl]).astype(BF16)
    _attn_split_kernel(sq_ref, sk_ref, sv_ref, so_ref, nseq=nseq)
    _tail_kernel(att_s, h_ref, wxo_ref, w1_ref, w2_ref, g2_ref, b2_ref, g3_ref, b3_ref, y_ref)


def _ptail(xq, kb, vb, h, wxo, w1, w2, g2, b2, g3, b3, sq, sk, sv, batch, seq, tq):
    nt = seq // tq
    sb, sl_, _ = sq.shape
    nseq = sb // (batch * nt)
    row = lambda: pl.BlockSpec((tq, D_MODEL), lambda b, t: (b * nt + t, 0))
    mem = pl.BlockSpec((1, N_MEM, D_MODEL), lambda b, t: (b, 0, 0))
    s_q = pl.BlockSpec((nseq, sl_, D_MODEL), lambda b, t: (b * nt + t, 0, 0))
    s_kv = pl.BlockSpec((None, nseq, N_MEM, X_HEADS, X_HEAD_DIM), lambda b, t: (0, b * nt + t, 0, 0, 0))
    return pl.pallas_call(
        functools.partial(_ptail_kernel, nseq=nseq),
        grid=(batch, nt),
        in_specs=[row(), mem, mem, row(), _wspec((D_MODEL, D_MODEL)), _wspec((D_MODEL, D_FF)),
                  _wspec((D_FF, D_MODEL)), _wspec((1, D_MODEL)), _wspec((1, D_MODEL)),
                  _wspec((1, D_MODEL)), _wspec((1, D_MODEL)), s_q, s_kv, s_kv],
        out_specs=[row(), s_q],
        out_shape=[jax.ShapeDtypeStruct((batch * seq, D_MODEL), F32),
                   jax.ShapeDtypeStruct((sb, sl_, D_MODEL), F32)],
        scratch_shapes=[pltpu.VMEM((tq, D_MODEL), BF16)],
        compiler_params=_params(("parallel", "parallel")),
        name="prompt_tail",
    )(xq, kb, vb, h, wxo, w1, w2, g2, b2, g3, b3, sq, sk, sv)


def _attn_split_kernel(q_ref, k_ref, v_ref, o_ref, *, nseq):
    scale = X_HEAD_DIM ** -0.5
    nl = q_ref.shape[1]
    nk = N_MEM * X_HEADS
    row_head = lax.broadcasted_iota(jnp.int32, (X_HEADS * nl, nk), 0) // nl
    col_head = lax.broadcasted_iota(jnp.int32, (X_HEADS * nl, nk), 1) % X_HEADS
    own = row_head == col_head
    for s in range(nseq):
        k2 = k_ref[s].reshape(nk, X_HEAD_DIM)
        v2 = v_ref[s].reshape(nk, X_HEAD_DIM)
        q = q_ref[s]
        q4 = jnp.concatenate([q[:, h * X_HEAD_DIM:(h + 1) * X_HEAD_DIM] for h in range(X_HEADS)], axis=0)
        pr = _softmax(jnp.where(own, _dot_nt(q4, k2) * scale, -1e30))
        o4 = _dot(pr, v2)
        for h in range(X_HEADS):
            o_ref[s, :, h * X_HEAD_DIM:(h + 1) * X_HEAD_DIM] = o4[h * nl:(h + 1) * nl]


PM_TL = 512
ROW_STRIDE = 4
CONV_RB = 8 * ROW_STRIDE
CONV_PAD = 32
SHORT_PAD = 8


def _split3(x):
    h = x.astype(BF16)
    r = x - h.astype(F32)
    m = r.astype(BF16)
    l = (r - m.astype(F32)).astype(BF16)
    return h, m, l


def _dot_exact_lhs(lhs_bf16, x):
    h, m, l = _split3(x)
    d = lambda p: jnp.dot(lhs_bf16, p, preferred_element_type=F32)
    return d(h) + d(m) + d(l)


def _bmm(a, b):
    return jnp.einsum('bij,bjk->bik', a.astype(BF16), b.astype(BF16), preferred_element_type=F32)


def _bmm_nt(a, b):
    return jnp.einsum('bik,bjk->bij', a.astype(BF16), b.astype(BF16), preferred_element_type=F32)


def _bmm_tn(a, b):
    return jnp.einsum('bki,bkj->bij', a.astype(BF16), b.astype(BF16), preferred_element_type=F32)


def _strided_conv(src_ref, w_ref, dst_ref, slab, dslab, nblk, ntaps, off, post):
    ws = [w_ref[slab, j] for j in range(ntaps)]

    def body(rb, carry):
        r0 = pl.multiple_of(rb * CONV_RB, CONV_RB)
        accs = [None] * ROW_STRIDE
        for q in range(off, off + ntaps + ROW_STRIDE - 1):
            x = src_ref[slab, pl.ds(r0 + q, 8, stride=ROW_STRIDE), :]
            for m in range(ROW_STRIDE):
                j = q - off - m
                if 0 <= j < ntaps:
                    term = ws[j] * x
                    accs[m] = term if accs[m] is None else accs[m] + term
        for m in range(ROW_STRIDE):
            dst_ref[dslab, pl.ds(r0 + m, 8, stride=ROW_STRIDE), :] = post(accs[m])
        return carry

    lax.fori_loop(0, nblk, body, 0, unroll=True)


def _pfront_kernel(x_ref, wa_ref, wg_ref, wbd_ref, bglu_ref, wdw_ref, bdw_ref, lcg_ref, lcb_ref, wsh_ref, alog_ref, dtb_ref,
                   dng_ref, wca_ref, wdn_ref, wo_ref, wxq_ref, g1_ref, b1_ref,
                   h_ref, xq_ref, pconv_ref, pqkv_ref, sout_ref,
                   xx, qx, ya, qs, ks, vs, S, ktw_s, n_s, q2_s, o1_s, sl_s, egc_s, cvn_s, on_s, og_s, ga_s, gb_s):
    t = pl.program_id(1)
    nt = pl.num_programs(1)
    tl = PM_TL
    c = CHUNK
    nc = tl // c
    n_cs = CONV_CH // LANES
    n_qs = QKV_W // LANES

    @pl.when(t == 0)
    def _():
        xx[:, 0:CONV_PAD, :] = jnp.zeros((n_cs, CONV_PAD, LANES), F32)
        qx[:, 0:SHORT_PAD, :] = jnp.zeros((n_qs, SHORT_PAD, LANES), F32)
        S[...] = jnp.zeros(S.shape, F32)

    xb = x_ref[...].astype(BF16)
    proj = lambda lo, hi: jnp.dot(xb, wa_ref[:, lo:hi], preferred_element_type=F32)
    gate = lambda lo, hi: jnp.dot(xb, wg_ref[:, lo:hi], preferred_element_type=F32)
    conv_a = lambda sl: _strided_conv(xx, wdw_ref, ya, sl, sl, tl // CONV_RB, CONV_K,
                                      CONV_PAD - (CONV_K - 1), lambda y: y)

    def conv_b(part):
        for hd in range(DN_HEADS):
            _strided_conv(qx, wsh_ref, (qs, ks, vs)[part], part * DN_HEADS + hd, hd, tl // CONV_RB, SHORT_K,
                          SHORT_PAD - (SHORT_K - 1), _silu)

    def qkv_part(part):
        y = proj(OFF_QKV + part * DN_QK, OFF_QKV + (part + 1) * DN_QK)
        for hd in range(DN_HEADS):
            qx[part * DN_HEADS + hd, SHORT_PAD:SHORT_PAD + tl, :] = y[:, hd * LANES:(hd + 1) * LANES]

    glu = proj(0, OFF_QKV) + bglu_ref[...]
    u = glu[:, :CONV_CH] * _sigmoid(glu[:, CONV_CH:])
    for sl in range(n_cs):
        xx[sl, CONV_PAD:CONV_PAD + tl, :] = u[:, sl * LANES:(sl + 1) * LANES]
    qkv_part(0)
    conv_a(0)
    qkv_part(1)
    conv_a(1)
    qkv_part(2)
    conv_a(2)
    og_s[...] = proj(OFF_OG, N_A)
    conv_a(3)
    ga_s[:, :DN_V] = gate(0, DN_V)
    conv_b(0)
    ga_s[:, DN_V:] = gate(DN_V, D_MODEL)
    conv_b(1)
    gb_s[:, :DN_V] = gate(D_MODEL, D_MODEL + DN_V)
    conv_b(2)
    gb_s[:, DN_V:] = gate(D_MODEL + DN_V, N_G)
    bd = jnp.dot(xb, wbd_ref[...], preferred_element_type=F32)
    for rb in range(tl // CHUNK):
        r0 = rb * CHUNK
        cv = jnp.concatenate([ya[sl, r0:r0 + CHUNK, :] for sl in range(n_cs)], axis=1) + bdw_ref[...]
        cvn_s[r0:r0 + CHUNK, :] = _silu(_ln(cv, lcg_ref[...], lcb_ref[...])).astype(BF16)

    beta = _sigmoid(bd)
    ld = -jnp.exp(alog_ref[...]) * _softplus(bd + dtb_ref[...])
    rt = lax.broadcasted_iota(jnp.int32, (tl, tl), 0)
    ct = lax.broadcasted_iota(jnp.int32, (tl, tl), 1)
    blk_tril = jnp.where((rt >= ct) & ((rt // c) == (ct // c)), 1.0, 0.0).astype(BF16)
    gsum = _dot_exact_lhs(blk_tril, ld)
    beta3 = beta.reshape(nc, c, LANES)
    gsum3 = gsum.reshape(nc, c, LANES)
    gsum_t = gsum.T
    glast3 = gsum3[:, c - 1:c, :]
    eg3 = jnp.exp(gsum3)
    etail3 = jnp.exp(glast3 - gsum3)
    egc3 = jnp.exp(glast3)

    ri = lax.broadcasted_iota(jnp.int32, (c, c), 0)
    ci = lax.broadcasted_iota(jnp.int32, (c, c), 1)
    tril = (ri >= ci)[None]
    strict = (ri > ci)[None]
    eye = jnp.where(ri == ci, 1.0, 0.0)[None]

    for hd in range(DN_HEADS):
        lb = slice(hd, hd + 1)
        lg = slice(DN_HEADS + hd, DN_HEADS + hd + 1)
        q = _l2n(qs[hd].reshape(nc, c, DN_DK)) * (DN_DK ** -0.5)
        k = _l2n(ks[hd].reshape(nc, c, DN_DK))
        v = vs[hd].reshape(nc, c, DN_DV)
        b_col = beta3[:, :, lb]
        g_row = jnp.stack([gsum_t[DN_HEADS + hd:DN_HEADS + hd + 1, ch * c:(ch + 1) * c] for ch in range(nc)])
        diff = gsum3[:, :, lg] - g_row
        decay = jnp.where(tril, jnp.exp(jnp.where(tril, diff, 0.0)), 0.0)
        kq = _bmm_nt(jnp.concatenate([k, q], axis=1), k)
        a = jnp.where(strict, b_col * kq[:, :c] * decay, 0.0)
        aqk = kq[:, c:] * decay
        p = eye - a
        x = a
        for _ in range(5):
            x = _bmm(x, x)
            p = p + _bmm(p, x)
        rhs = jnp.concatenate([(b_col * eg3[:, :, lg]) * k, b_col * v], axis=2)
        sol = _bmm(p, rhs)
        qo = _bmm(aqk, sol)
        q2_s[hd] = (eg3[:, :, lg] * q - qo[:, :, :DN_DK]).astype(BF16)
        o1_s[hd] = qo[:, :, DN_DK:]
        kn = _bmm_tn(k * etail3[:, :, lg], sol)
        ktw_s[hd] = kn[:, :, :DN_DK].astype(BF16)
        n_s[hd] = kn[:, :, DN_DK:]
        egc_s[hd] = jnp.broadcast_to(egc3[:, :, lg], (nc, 1, DN_DV))

    def carry_body(ch, carry):
        for hd in range(DN_HEADS):
            s_h = S[hd]
            s_b = s_h.astype(BF16)
            sl_s[hd, ch] = s_b
            S[hd] = egc_s[hd, ch] * s_h + (n_s[hd, ch] - jnp.dot(ktw_s[hd, ch], s_b,
                                                              preferred_element_type=F32))
        return carry

    for ch in range(nc):
        carry_body(ch, 0)

    for hd in range(DN_HEADS):
        ls = slice(hd * DN_DK, (hd + 1) * DN_DK)
        o = o1_s[hd] + jnp.einsum('bck,bkv->bcv', q2_s[hd], sl_s[hd], preferred_element_type=F32)
        o = o * lax.rsqrt(jnp.mean(o * o, -1, keepdims=True) + RMS_EPS) * dng_ref[...]
        on_s[:, ls] = (o.reshape(tl, DN_DV) * _silu(og_s[:, ls])).astype(BF16)

    mixed = _sigmoid(ga_s[...]) * jnp.dot(cvn_s[...], wca_ref[...], preferred_element_type=F32)
    mixed = mixed + _sigmoid(gb_s[...]) * jnp.dot(on_s[...], wdn_ref[...], preferred_element_type=F32)
    h = _ln(DEEPNORM_ALPHA * x_ref[...] + _dot(mixed, wo_ref[...]), g1_ref[...], b1_ref[...])
    h_ref[...] = h
    xq_ref[...] = _dot(h, wxq_ref[...]).astype(BF16)

    @pl.when(t == nt - 1)
    def _():
        for sl in range(n_cs):
            pconv_ref[0, :, sl * LANES:(sl + 1) * LANES] = xx[sl, tl:tl + CONV_PAD, :]
        for s12 in range(n_qs):
            pqkv_ref[0, :, s12 * LANES:(s12 + 1) * LANES] = qx[s12, tl:tl + SHORT_PAD, :]
        sout_ref[0] = S[...]

    for sl in range(n_cs):
        xx[sl, 0:CONV_PAD, :] = xx[sl, tl:tl + CONV_PAD, :]
    for s12 in range(n_qs):
        qx[s12, 0:SHORT_PAD, :] = qx[s12, tl:tl + SHORT_PAD, :]


def _pfront(x, w_a, w_g, w_bd, bglu, wdw, bdw, lcg, lcb, wsh, alog, dtb, dng, wca, wdn, wo, wxq, g1, b1, batch, seq):
    tl = PM_TL
    nt = seq // tl
    nc = tl // CHUNK
    n = batch * seq
    row = lambda w: pl.BlockSpec((tl, w), lambda b, t: (b * nt + t, 0))
    per_seq = lambda *s: pl.BlockSpec((1,) + s, lambda b, t: (b,) + (0,) * len(s))
    return pl.pallas_call(
        _pfront_kernel,
        grid=(batch, nt),
        in_specs=[row(D_MODEL), _wspec((D_MODEL, N_A)), _wspec((D_MODEL, N_G)), _wspec((D_MODEL, LANES)),
                  _wspec((1, 2 * CONV_CH)),
                  _wspec((CONV_CH // LANES, CONV_K, 8, LANES)), _wspec((1, CONV_CH)), _wspec((1, CONV_CH)),
                  _wspec((1, CONV_CH)), _wspec((QKV_W // LANES, SHORT_K, 8, LANES)), _wspec((1, LANES)),
                  _wspec((1, LANES)), _wspec((1, DN_DV)),
                  _wspec((CONV_CH, D_MODEL)), _wspec((DN_V, D_MODEL)), _wspec((D_MODEL, D_MODEL)),
                  _wspec((D_MODEL, D_MODEL)), _wspec((1, D_MODEL)), _wspec((1, D_MODEL))],
        out_specs=[row(D_MODEL), row(D_MODEL), per_seq(CONV_PAD, CONV_CH), per_seq(SHORT_PAD, QKV_W),
                   per_seq(DN_HEADS, DN_DK, DN_DV)],
        out_shape=[jax.ShapeDtypeStruct((n, D_MODEL), F32), jax.ShapeDtypeStruct((n, D_MODEL), BF16),
                   jax.ShapeDtypeStruct((batch, CONV_PAD, CONV_CH), F32),
                   jax.ShapeDtypeStruct((batch, SHORT_PAD, QKV_W), F32),
                   jax.ShapeDtypeStruct((batch, DN_HEADS, DN_DK, DN_DV), F32)],
        scratch_shapes=[pltpu.VMEM((CONV_CH // LANES, tl + CONV_PAD, LANES), F32),
                        pltpu.VMEM((QKV_W // LANES, tl + SHORT_PAD, LANES), F32),
                        pltpu.VMEM((CONV_CH // LANES, tl, LANES), F32),
                        pltpu.VMEM((DN_HEADS, tl, DN_DK), F32), pltpu.VMEM((DN_HEADS, tl, DN_DK), F32),
                        pltpu.VMEM((DN_HEADS, tl, DN_DV), F32),
                        pltpu.VMEM((DN_HEADS, DN_DK, DN_DV), F32),
                        pltpu.VMEM((DN_HEADS, nc, DN_DK, DN_DK), BF16),
                        pltpu.VMEM((DN_HEADS, nc, DN_DK, DN_DV), F32),
                        pltpu.VMEM((DN_HEADS, nc, CHUNK, DN_DK), BF16),
                        pltpu.VMEM((DN_HEADS, nc, CHUNK, DN_DV), F32),
                        pltpu.VMEM((DN_HEADS, nc, DN_DK, DN_DV), BF16),
                        pltpu.VMEM((DN_HEADS, nc, 1, DN_DV), F32),
                        pltpu.VMEM((tl, CONV_CH), BF16), pltpu.VMEM((tl, DN_V), BF16),
                        pltpu.VMEM((tl, DN_V), F32), pltpu.VMEM((tl, D_MODEL), F32),
                        pltpu.VMEM((tl, D_MODEL), F32)],
        compiler_params=_params(("parallel", "arbitrary")),
        name="prompt_front",
    )(x, w_a, w_g, w_bd, bglu, wdw, bdw, lcg, lcb, wsh, alog, dtb, dng, wca, wdn, wo, wxq, g1, b1)


SM_BT = 32
SM_BS = 8


def _smid1_kernel(u_ref, qkv_ref, bd_ref, hc_ref, hq_ref, wdw_ref, bdw_ref, lcg_ref, lcb_ref, wsh_ref,
                  alog_ref, dtb_ref,
                  cvn_ref, cnew_ref, qnew_ref, w_ref, q2_ref, uv_ref, o1_ref, kt_ref, egc_ref):
    nl = u_ref.shape[0]
    nh = CONV_K - 1
    xx = [hc_ref[i] for i in range(nh)] + [u_ref[i] for i in range(nl)]
    for t in range(nl):
        acc = wdw_ref[0:1, :] * xx[t]
        for j in range(1, CONV_K):
            acc = acc + wdw_ref[j:j + 1, :] * xx[t + j]
        cvn_ref[t] = _silu(_ln(acc + bdw_ref[...], lcg_ref[...], lcb_ref[...]))
    for i in range(nh):
        cnew_ref[i] = xx[nl + i]

    ns = SHORT_K - 1
    qq = [hq_ref[i] for i in range(ns)] + [qkv_ref[i] for i in range(nl)]
    for i in range(ns):
        qnew_ref[i] = qq[nl + i]
    qkv = []
    for t in range(nl):
        acc = wsh_ref[0:1, :] * qq[t]
        for j in range(1, SHORT_K):
            acc = acc + wsh_ref[j:j + 1, :] * qq[t + j]
        qkv.append(_silu(acc))

    beta = [_sigmoid(bd_ref[t]) for t in range(nl)]
    ld = [-jnp.exp(alog_ref[...]) * _softplus(bd_ref[t] + dtb_ref[...]) for t in range(nl)]

    for hd in range(DN_HEADS):
        ls = slice(hd * DN_DK, (hd + 1) * DN_DK)
        q = [_l2n(qkv[t][:, hd * DN_DK:(hd + 1) * DN_DK]) * (DN_DK ** -0.5) for t in range(nl)]
        k = [_l2n(qkv[t][:, DN_QK + hd * DN_DK:DN_QK + (hd + 1) * DN_DK]) for t in range(nl)]
        v = [qkv[t][:, 2 * DN_QK + hd * DN_DV:2 * DN_QK + (hd + 1) * DN_DV] for t in range(nl)]
        b = [beta[t][:, hd:hd + 1] for t in range(nl)]
        g = [ld[t][:, DN_HEADS + hd:DN_HEADS + hd + 1] for t in range(nl)]
        gc = [g[0]]
        for t in range(1, nl):
            gc.append(gc[t - 1] + g[t])
        w_l, uv_l = [], []
        for i in range(nl):
            wi = (b[i] * jnp.exp(gc[i])) * k[i]
            ui = b[i] * v[i]
            for j in range(i):
                a_ij = b[i] * jnp.sum(k[i] * k[j], -1, keepdims=True) * jnp.exp(gc[i] - gc[j])
                wi = wi - a_ij * w_l[j]
                ui = ui - a_ij * uv_l[j]
            w_l.append(wi)
            uv_l.append(ui)
        for i in range(nl):
            o1 = jnp.zeros_like(v[i])
            q2 = jnp.exp(gc[i]) * q[i]
            for j in range(i + 1):
                aqk = jnp.sum(q[i] * k[j], -1, keepdims=True) * jnp.exp(gc[i] - gc[j])
                o1 = o1 + aqk * uv_l[j]
                q2 = q2 - aqk * w_l[j]
            w_ref[i, :, ls] = w_l[i]
            uv_ref[i, :, ls] = uv_l[i]
            o1_ref[i, :, ls] = o1
            q2_ref[i, :, ls] = q2
            kt_ref[i, :, ls] = k[i] * jnp.exp(gc[nl - 1] - gc[i])
        egc_ref[hd] = jnp.broadcast_to(jnp.exp(gc[nl - 1]), (u_ref.shape[1], LANES))


def _smid1(u, qkv, bd, hc, hq, wdw, bdw, lcg, lcb, wsh, alog, dtb):
    nl, b, _ = u.shape
    bt = SM_BT
    tm = lambda r, w: pl.BlockSpec((r, bt, w), lambda i: (0, i, 0))
    outs = [(nl, CONV_CH), (CONV_K - 1, CONV_CH), (SHORT_K - 1, QKV_W), (nl, DN_QK), (nl, DN_QK),
            (nl, DN_V), (nl, DN_V), (nl, DN_QK), (DN_HEADS, LANES)]
    return pl.pallas_call(
        _smid1_kernel,
        grid=(b // bt,),
        in_specs=[tm(nl, CONV_CH), tm(nl, QKV_W), tm(nl, LANES), tm(CONV_K - 1, CONV_CH),
                  tm(SHORT_K - 1, QKV_W),
                  _wspec((CONV_PAD, CONV_CH)), _wspec((1, CONV_CH)), _wspec((1, CONV_CH)),
                  _wspec((1, CONV_CH)), _wspec((SHORT_K, QKV_W)), _wspec((1, LANES)),
                  _wspec((1, LANES))],
        out_specs=[tm(r, w) for r, w in outs],
        out_shape=[jax.ShapeDtypeStruct((r, b, w), F32) for r, w in outs],
        compiler_params=_params(("parallel",)),
        name="sample_mid1",
    )(u, qkv, bd, hc, hq, wdw, bdw, lcg, lcb, wsh, alog, dtb)


def _smid2_kernel(w_ref, q2_ref, uv_ref, o1_ref, kt_ref, og_ref, egc_ref, s_ref, dng_ref,
                  on_ref, snew_ref):
    nl = w_ref.shape[1]
    for hd in range(DN_HEADS):
        ls = slice(hd * DN_DK, (hd + 1) * DN_DK)
        s_h = s_ref[:, hd]
        lhs = jnp.concatenate([w_ref[:, :, ls], q2_ref[:, :, ls]], axis=1)
        r = jnp.einsum('bck,bkv->bcv', lhs.astype(BF16), s_h.astype(BF16),
                       preferred_element_type=F32)
        u_new = uv_ref[:, :, ls] - r[:, :nl]
        o = o1_ref[:, :, ls] + r[:, nl:]
        upd = jnp.einsum('bck,bcv->bkv', kt_ref[:, :, ls].astype(BF16), u_new.astype(BF16),
                         preferred_element_type=F32)
        snew_ref[:, hd] = egc_ref[:, hd] * s_h + upd
        o = o * lax.rsqrt(jnp.mean(o * o, -1, keepdims=True) + RMS_EPS) * dng_ref[...]
        on_ref[:, :, ls] = o * _silu(og_ref[:, :, ls])


def _smid2(w, q2, uv, o1, kt, og, egc, s, dng):
    b, nl, _ = w.shape
    bs = SM_BS
    sq = pl.BlockSpec((bs, nl, DN_V), lambda i: (i, 0, 0))
    st = pl.BlockSpec((bs, DN_HEADS, DN_DK, DN_DV), lambda i: (i, 0, 0, 0))
    return pl.pallas_call(
        _smid2_kernel,
        grid=(b // bs,),
        in_specs=[sq, sq, sq, sq, sq, sq,
                  pl.BlockSpec((bs, DN_HEADS, 1, LANES), lambda i: (i, 0, 0, 0)), st,
                  _wspec((1, DN_DV))],
        out_specs=[sq, st],
        out_shape=[jax.ShapeDtypeStruct((b, nl, DN_V), F32),
                   jax.ShapeDtypeStruct((b, DN_HEADS, DN_DK, DN_DV), F32)],
        compiler_params=_params(("parallel",)),
        name="sample_mid2",
    )(w, q2, uv, o1, kt, og, egc, s, dng)


def _tm(x):
    return jnp.swapaxes(x, 0, 1)


def kernel(x_prompt, x_sample, mem_prompt, state_conv, state_qkv_conv, state_delta, cache_mem_k, cache_mem_v, w_in, b_glu, w_dw, b_dw, ln_conv_g, ln_conv_b, w_conv_out, w_short, a_log, dt_bias, dn_norm_g, w_dn_out, w_o, ln1_g, ln1_b, w_xq, w_mem_kv, w_xo, ln2_g, ln2_b, w_ff1, w_ff2, ln3_g, ln3_b):
    assert w_in.shape[0] == DEPTH == 1
    bp, lp, _ = x_prompt.shape
    bsm, ls_, _ = x_sample.shape

    w = w_in[0]
    o_ga = N_A + 2 * DN_HEADS
    w_a = w[:, :N_A].astype(BF16)
    w_g = w[:, o_ga:o_ga + N_G].astype(BF16)
    w_bd = jnp.concatenate([w[:, N_A:o_ga], jnp.zeros((D_MODEL, LANES - 2 * DN_HEADS), w.dtype)],
                           axis=1).astype(BF16)
    lane_pad = lambda a: jnp.concatenate(
        [jnp.zeros((DN_HEADS,), F32), a.astype(F32), jnp.zeros((LANES - 2 * DN_HEADS,), F32)])[None]
    alog = lane_pad(a_log[0])
    dtb = lane_pad(dt_bias[0])
    wdw = jnp.concatenate([w_dw[0], jnp.zeros((CONV_PAD - CONV_K, CONV_CH), F32)], axis=0)
    r2 = lambda a: a[0][None]
    bglu, bdw, lcg, lcb = r2(b_glu), r2(b_dw), r2(ln_conv_g), r2(ln_conv_b)
    dng = r2(dn_norm_g)
    wsh = w_short[0]
    slabs = lambda a: jnp.broadcast_to(
        jnp.swapaxes(a.reshape(a.shape[0], a.shape[1] // LANES, LANES), 0, 1)[:, :, None, :],
        (a.shape[1] // LANES, a.shape[0], 8, LANES))
    wdw_slab = slabs(w_dw[0])
    wsh_slab = slabs(wsh)
    wca, wdn, wo, wxq, wxo = (a[0].astype(BF16) for a in (w_conv_out, w_dn_out, w_o, w_xq, w_xo))
    w1, w2, wkv = w_ff1[0].astype(BF16), w_ff2[0].astype(BF16), w_mem_kv[0].astype(BF16)
    g1, b1, g2, b2, g3, b3 = (r2(a) for a in (ln1_g, ln1_b, ln2_g, ln2_b, ln3_g, ln3_b))

    n_s = bsm * ls_
    xs = _tm(x_sample).reshape(n_s, D_MODEL)
    u, qkv, og, ga, gb, bd = _inproj(xs, w_a, w_g, w_bd, bglu, 256)
    t3 = lambda a: a.reshape(ls_, bsm, a.shape[-1])
    cvn, c_new, q_new, w_, q2, uv, o1, kt, egc = _smid1(
        t3(u), t3(qkv), t3(bd), _tm(state_conv[0]), _tm(state_qkv_conv[0]),
        wdw, bdw, lcg, lcb, wsh, alog, dtb)
    on, s_delta = _smid2(_tm(w_), _tm(q2), _tm(uv), _tm(o1), _tm(kt), _tm(t3(og)),
                         _tm(egc)[:, :, None, :], state_delta[0], dng)
    h_s, xq_s = _merge(cvn.reshape(n_s, CONV_CH), _tm(on).reshape(n_s, DN_V), ga, gb, xs,
                       wca, wdn, wo, wxq, g1, b1, 512)

    n_p = bp * lp
    xp = x_prompt.reshape(n_p, D_MODEL)
    mk, mv, mk_b, mv_b = _memkv(mem_prompt.reshape(bp * N_MEM, D_MODEL), wkv, 512)
    h, xq, p_conv, p_qkv, p_delta = _pfront(xp, w_a, w_g, w_bd, bglu, wdw_slab, bdw, lcg, lcb, wsh_slab,
                                            alog, dtb, dng, wca, wdn, wo, wxq, g1, b1, bp, lp)
    y_p, att_s = _ptail(xq, mk_b.reshape(bp, N_MEM, D_MODEL), mv_b.reshape(bp, N_MEM, D_MODEL), h,
                        wxo, w1, w2, g2, b2, g3, b3, _tm(t3(xq_s)), cache_mem_k, cache_mem_v, bp, lp, 512)
    y_p = y_p.reshape(bp, lp, D_MODEL)
    p_conv = p_conv[:, CONV_PAD - (CONV_K - 1):]
    p_qkv = p_qkv[:, SHORT_PAD - (SHORT_K - 1):]
    p_mk = mk.reshape(bp, N_MEM, X_HEADS, X_HEAD_DIM)
    p_mv = mv.reshape(bp, N_MEM, X_HEADS, X_HEAD_DIM)

    y_s = _tail(_tm(att_s).reshape(n_s, D_MODEL), h_s, wxo, w1, w2, g2, b2, g3, b3, 512)
    y_s = _tm(t3(y_s))

    return (y_p, y_s, p_conv[None], p_qkv[None], p_delta[None], p_mk[None], p_mv[None],
            _tm(c_new)[None], _tm(q_new)[None], s_delta[None])
```

```python
import functools

import jax
import jax.numpy as jnp
from jax import lax
from jax.experimental import pallas as pl
from jax.experimental.pallas import tpu as pltpu

F32 = jnp.float32
BF16 = jnp.bfloat16

D_MODEL = 1024
N_MEM = 256
CONV_CH = 512
CONV_K = 31
DN_HEADS = 4
DN_DK = 128
DN_DV = 128
DN_QK = DN_HEADS * DN_DK
DN_V = DN_HEADS * DN_DV
QKV_W = 2 * DN_QK + DN_V
SHORT_K = 4
CHUNK = 64
X_HEADS = 4
X_HEAD_DIM = D_MODEL // X_HEADS
D_FF = 4 * D_MODEL
DEPTH = 1
DEEPNORM_ALPHA = (2 * DEPTH) ** 0.25
LN_EPS = 1e-5
RMS_EPS = 1e-6
L2_EPS = 1e-6

LANES = 128
OFF_QKV = 2 * CONV_CH
OFF_OG = OFF_QKV + QKV_W
N_A = OFF_OG + DN_V
N_G = 2 * D_MODEL
VMEM_LIMIT = 56 * 1024 * 1024


def _dot(a, b):
    return jnp.dot(a.astype(BF16), b.astype(BF16), preferred_element_type=F32)


def _dot_nt(a, b):
    return lax.dot_general(a.astype(BF16), b.astype(BF16), (((1,), (1,)), ((), ())),
                           preferred_element_type=F32)


def _sigmoid(x):
    return 1.0 / (1.0 + jnp.exp(-x))


def _silu(x):
    return x * _sigmoid(x)


def _softplus(x):
    return jnp.maximum(x, 0.0) + jnp.log(1.0 + jnp.exp(-jnp.abs(x)))


def _ln(x, g, b):
    mu = jnp.mean(x, -1, keepdims=True)
    xc = x - mu
    var = jnp.mean(xc * xc, -1, keepdims=True)
    return xc * lax.rsqrt(var + LN_EPS) * g + b


def _l2n(x):
    return x * lax.rsqrt(jnp.sum(x * x, -1, keepdims=True) + L2_EPS)


def _wspec(shape):
    return pl.BlockSpec(shape, lambda *_: (0,) * len(shape), pipeline_mode=pl.Buffered(1))


def _params(sem):
    return pltpu.CompilerParams(dimension_semantics=sem, vmem_limit_bytes=VMEM_LIMIT)


def _inproj_kernel(x_ref, wa_ref, wg_ref, wbd_ref, bglu_ref, u_ref, qkv_ref, og_ref, ga_ref, gb_ref, bd_ref):
    x = x_ref[...].astype(BF16)
    glu = jnp.dot(x, wa_ref[:, 0:OFF_QKV], preferred_element_type=F32) + bglu_ref[...]
    u_ref[...] = glu[:, :CONV_CH] * _sigmoid(glu[:, CONV_CH:])
    qkv_ref[...] = jnp.dot(x, wa_ref[:, OFF_QKV:OFF_OG], preferred_element_type=F32)
    og_ref[...] = jnp.dot(x, wa_ref[:, OFF_OG:N_A], preferred_element_type=F32)
    ga_ref[...] = jnp.dot(x, wg_ref[:, :D_MODEL], preferred_element_type=F32)
    gb_ref[...] = jnp.dot(x, wg_ref[:, D_MODEL:], preferred_element_type=F32)
    bd_ref[...] = jnp.dot(x, wbd_ref[...], preferred_element_type=F32)


def _inproj(x, w_a, w_g, w_bd, b_glu, tm):
    n = x.shape[0]
    widths = (CONV_CH, QKV_W, DN_V, D_MODEL, D_MODEL, LANES)
    row = lambda w: pl.BlockSpec((tm, w), lambda i: (i, 0))
    return pl.pallas_call(
        _inproj_kernel,
        grid=(n // tm,),
        in_specs=[row(D_MODEL), _wspec((D_MODEL, N_A)), _wspec((D_MODEL, N_G)), _wspec((D_MODEL, LANES)),
                  _wspec((1, 2 * CONV_CH))],
        out_specs=[row(w) for w in widths],
        out_shape=[jax.ShapeDtypeStruct((n, w), F32) for w in widths],
        compiler_params=_params(("parallel",)),
        name="inproj",
    )(x, w_a, w_g, w_bd, b_glu)


def _merge_kernel(cvn_ref, on_ref, ga_ref, gb_ref, x_ref, wca_ref, wdn_ref, wo_ref, wxq_ref,
                  g1_ref, b1_ref, h_ref, xq_ref):
    br_a = _dot(cvn_ref[...], wca_ref[...])
    br_b = _dot(on_ref[...], wdn_ref[...])
    mixed = _sigmoid(ga_ref[...]) * br_a + _sigmoid(gb_ref[...]) * br_b
    h = _ln(DEEPNORM_ALPHA * x_ref[...] + _dot(mixed, wo_ref[...]), g1_ref[...], b1_ref[...])
    h_ref[...] = h
    xq_ref[...] = _dot(h, wxq_ref[...])


def _merge(cvn, on, ga, gb, x, wca, wdn, wo, wxq, g1, b1, tm):
    n = x.shape[0]
    row = lambda w: pl.BlockSpec((tm, w), lambda i: (i, 0))
    return pl.pallas_call(
        _merge_kernel,
        grid=(n // tm,),
        in_specs=[row(CONV_CH), row(DN_V), row(D_MODEL), row(D_MODEL), row(D_MODEL),
                  _wspec((CONV_CH, D_MODEL)), _wspec((DN_V, D_MODEL)), _wspec((D_MODEL, D_MODEL)),
                  _wspec((D_MODEL, D_MODEL)), _wspec((1, D_MODEL)), _wspec((1, D_MODEL))],
        out_specs=[row(D_MODEL), row(D_MODEL)],
        out_shape=[jax.ShapeDtypeStruct((n, D_MODEL), F32)] * 2,
        compiler_params=_params(("parallel",)),
        name="merge",
    )(cvn, on, ga, gb, x, wca, wdn, wo, wxq, g1, b1)


FF_BLOCK = 1024


def _tail_kernel(att_ref, h_ref, wxo_ref, w1_ref, w2_ref, g2_ref, b2_ref, g3_ref, b3_ref, y_ref):
    xo = _dot(att_ref[...], wxo_ref[...])
    h2 = _ln(DEEPNORM_ALPHA * h_ref[...] + xo, g2_ref[...], b2_ref[...])
    h2b = h2.astype(BF16)
    ff = jnp.zeros(h2.shape, F32)
    for c in range(D_FF // FF_BLOCK):
        a = jnp.dot(h2b, w1_ref[:, c * FF_BLOCK:(c + 1) * FF_BLOCK], preferred_element_type=F32)
        a = jnp.square(jnp.maximum(a, 0.0))
        ff = ff + jnp.dot(a.astype(BF16), w2_ref[c * FF_BLOCK:(c + 1) * FF_BLOCK, :],
                          preferred_element_type=F32)
    y_ref[...] = _ln(DEEPNORM_ALPHA * h2 + ff, g3_ref[...], b3_ref[...])


def _tail(att, h, wxo, w1, w2, g2, b2, g3, b3, tm):
    n = h.shape[0]
    row = lambda w: pl.BlockSpec((tm, w), lambda i: (i, 0))
    return pl.pallas_call(
        _tail_kernel,
        grid=(n // tm,),
        in_specs=[row(D_MODEL), row(D_MODEL), _wspec((D_MODEL, D_MODEL)), _wspec((D_MODEL, D_FF)),
                  _wspec((D_FF, D_MODEL)), _wspec((1, D_MODEL)), _wspec((1, D_MODEL)),
                  _wspec((1, D_MODEL)), _wspec((1, D_MODEL))],
        out_specs=row(D_MODEL),
        out_shape=jax.ShapeDtypeStruct((n, D_MODEL), F32),
        compiler_params=_params(("parallel",)),
        name="tail",
    )(att, h, wxo, w1, w2, g2, b2, g3, b3)


def _memkv_kernel(m_ref, w_ref, k_ref, v_ref, kb_ref, vb_ref):
    m = m_ref[...].astype(BF16)
    tm = m.shape[0]
    k = jnp.dot(m, w_ref[:, :D_MODEL], preferred_element_type=F32)
    v = jnp.dot(m, w_ref[:, D_MODEL:], preferred_element_type=F32)
    k_ref[...] = k.reshape(tm, X_HEADS, X_HEAD_DIM)
    v_ref[...] = v.reshape(tm, X_HEADS, X_HEAD_DIM)
    kb_ref[...] = k.astype(BF16)
    vb_ref[...] = v.astype(BF16)


def _memkv(mem, w, tm):
    n = mem.shape[0]
    row = pl.BlockSpec((tm, D_MODEL), lambda i: (i, 0))
    row4 = pl.BlockSpec((tm, X_HEADS, X_HEAD_DIM), lambda i: (i, 0, 0))
    return pl.pallas_call(
        _memkv_kernel,
        grid=(n // tm,),
        in_specs=[row, _wspec((D_MODEL, 2 * D_MODEL))],
        out_specs=[row4, row4, row, row],
        out_shape=[jax.ShapeDtypeStruct((n, X_HEADS, X_HEAD_DIM), F32)] * 2
        + [jax.ShapeDtypeStruct((n, D_MODEL), BF16)] * 2,
        compiler_params=_params(("parallel",)),
        name="memkv",
    )(mem, w)


def _softmax(sc):
    sc = sc - jnp.max(sc, -1, keepdims=True)
    e = jnp.exp(sc)
    return e / jnp.sum(e, -1, keepdims=True)


def _ptail_kernel(q_ref, k_ref, v_ref, h_ref, wxo_ref, w1_ref, w2_ref, g2_ref, b2_ref, g3_ref, b3_ref,
                  sq_ref, sk_ref, sv_ref, y_ref, so_ref, att_s, *, nseq):
    scale = X_HEAD_DIM ** -0.5
    for hd in range(X_HEADS):
        sl = slice(hd * X_HEAD_DIM, (hd + 1) * X_HEAD_DIM)
        pr = _softmax(_dot_nt(q_ref[:, sl], k_ref[0, :, sl]) * scale)
        att_s[:, sl] = _dot(pr, v_ref[0, :, sl]).astype(BF16)
    _attn_split_kernel(sq_ref, sk_ref, sv_ref, so_ref, nseq=nseq)
    _tail_kernel(att_s, h_ref, wxo_ref, w1_ref, w2_ref, g2_ref, b2_ref, g3_ref, b3_ref, y_ref)


def _ptail(xq, kb, vb, h, wxo, w1, w2, g2, b2, g3, b3, sq, sk, sv, batch, seq, tq):
    nt = seq // tq
    sb, sl_, _ = sq.shape
    nseq = sb // (batch * nt)
    row = lambda: pl.BlockSpec((tq, D_MODEL), lambda b, t: (b * nt + t, 0))
    mem = pl.BlockSpec((1, N_MEM, D_MODEL), lambda b, t: (b, 0, 0))
    s_q = pl.BlockSpec((nseq, sl_, D_MODEL), lambda b, t: (b * nt + t, 0, 0))
    s_kv = pl.BlockSpec((None, nseq, N_MEM, X_HEADS, X_HEAD_DIM), lambda b, t: (0, b * nt + t, 0, 0, 0))
    return pl.pallas_call(
        functools.partial(_ptail_kernel, nseq=nseq),
        grid=(batch, nt),
        in_specs=[row(), mem, mem, row(), _wspec((D_MODEL, D_MODEL)), _wspec((D_MODEL, D_FF)),
                  _wspec((D_FF, D_MODEL)), _wspec((1, D_MODEL)), _wspec((1, D_MODEL)),
                  _wspec((1, D_MODEL)), _wspec((1, D_MODEL)), s_q, s_kv, s_kv],
        out_specs=[row(), s_q],
        out_shape=[jax.ShapeDtypeStruct((batch * seq, D_MODEL), F32),
                   jax.ShapeDtypeStruct((sb, sl_, D_MODEL), F32)],
        scratch_shapes=[pltpu.VMEM((tq, D_MODEL), BF16)],
        compiler_params=_params(("parallel", "parallel")),
        name="prompt_tail",
    )(xq, kb, vb, h, wxo, w1, w2, g2, b2, g3, b3, sq, sk, sv)


def _attn_split_kernel(q_ref, k_ref, v_ref, o_ref, *, nseq):
    scale = X_HEAD_DIM ** -0.5
    nl = q_ref.shape[1]
    nk = N_MEM * X_HEADS
    row_head = lax.broadcasted_iota(jnp.int32, (X_HEADS * nl, nk), 0) // nl
    col_head = lax.broadcasted_iota(jnp.int32, (X_HEADS * nl, nk), 1) % X_HEADS
    own = row_head == col_head
    for s in range(nseq):
        k2 = k_ref[s].reshape(nk, X_HEAD_DIM)
        v2 = v_ref[s].reshape(nk, X_HEAD_DIM)
        q = q_ref[s]
        q4 = jnp.concatenate([q[:, h * X_HEAD_DIM:(h + 1) * X_HEAD_DIM] for h in range(X_HEADS)], axis=0)
        pr = _softmax(jnp.where(own, _dot_nt(q4, k2) * scale, -1e30))
        o4 = _dot(pr, v2)
        for h in range(X_HEADS):
            o_ref[s, :, h * X_HEAD_DIM:(h + 1) * X_HEAD_DIM] = o4[h * nl:(h + 1) * nl]


PM_TL = 512
ROW_STRIDE = 4
CONV_RB = 8 * ROW_STRIDE
CONV_PAD = 32
SHORT_PAD = 8


def _split3(x):
    h = x.astype(BF16)
    r = x - h.astype(F32)
    m = r.astype(BF16)
    l = (r - m.astype(F32)).astype(BF16)
    return h, m, l


def _dot_exact_lhs(lhs_bf16, x):
    h, m, l = _split3(x)
    d = lambda p: jnp.dot(lhs_bf16, p, preferred_element_type=F32)
    return d(h) + d(m) + d(l)


def _bmm(a, b):
    return jnp.einsum('bij,bjk->bik', a.astype(BF16), b.astype(BF16), preferred_element_type=F32)


def _bmm_nt(a, b):
    return jnp.einsum('bik,bjk->bij', a.astype(BF16), b.astype(BF16), preferred_element_type=F32)


def _bmm_tn(a, b):
    return jnp.einsum('bki,bkj->bij', a.astype(BF16), b.astype(BF16), preferred_element_type=F32)


def _strided_conv(src_ref, w_ref, dst_ref, slab, dslab, nblk, ntaps, off, post):
    ws = [w_ref[slab, j] for j in range(ntaps)]

    def body(rb, carry):
        r0 = pl.multiple_of(rb * CONV_RB, CONV_RB)
        accs = [None] * ROW_STRIDE
        for q in range(off, off + ntaps + ROW_STRIDE - 1):
            x = src_ref[slab, pl.ds(r0 + q, 8, stride=ROW_STRIDE), :]
            for m in range(ROW_STRIDE):
                j = q - off - m
                if 0 <= j < ntaps:
                    term = ws[j] * x
                    accs[m] = term if accs[m] is None else accs[m] + term
        for m in range(ROW_STRIDE):
            dst_ref[dslab, pl.ds(r0 + m, 8, stride=ROW_STRIDE), :] = post(accs[m])
        return carry

    lax.fori_loop(0, nblk, body, 0, unroll=True)


def _pfront_kernel(x_ref, wa_ref, wg_ref, wbd_ref, bglu_ref, wdw_ref, bdw_ref, lcg_ref, lcb_ref, wsh_ref, alog_ref, dtb_ref,
                   dng_ref, wca_ref, wdn_ref, wo_ref, wxq_ref, g1_ref, b1_ref,
                   h_ref, xq_ref, pconv_ref, pqkv_ref, sout_ref,
                   xx, qx, ya, qs, ks, vs, S, ktw_s, n_s, q2_s, o1_s, sl_s, egc_s, cvn_s, on_s, og_s, ga_s, gb_s):
    t = pl.program_id(1)
    nt = pl.num_programs(1)
    tl = PM_TL
    c = CHUNK
    nc = tl // c
    n_cs = CONV_CH // LANES
    n_qs = QKV_W // LANES

    @pl.when(t == 0)
    def _():
        xx[:, 0:CONV_PAD, :] = jnp.zeros((n_cs, CONV_PAD, LANES), F32)
        qx[:, 0:SHORT_PAD, :] = jnp.zeros((n_qs, SHORT_PAD, LANES), F32)
        S[...] = jnp.zeros(S.shape, F32)

    xb = x_ref[...].astype(BF16)
    proj = lambda lo, hi: jnp.dot(xb, wa_ref[:, lo:hi], preferred_element_type=F32)
    gate = lambda lo, hi: jnp.dot(xb, wg_ref[:, lo:hi], preferred_element_type=F32)
    conv_a = lambda sl: _strided_conv(xx, wdw_ref, ya, sl, sl, tl // CONV_RB, CONV_K,
                                      CONV_PAD - (CONV_K - 1), lambda y: y)

    def conv_b(part):
        for hd in range(DN_HEADS):
            _strided_conv(qx, wsh_ref, (qs, ks, vs)[part], part * DN_HEADS + hd, hd, tl // CONV_RB, SHORT_K,
                          SHORT_PAD - (SHORT_K - 1), _silu)

    def qkv_part(part):
        y = proj(OFF_QKV + part * DN_QK, OFF_QKV + (part + 1) * DN_QK)
        for hd in range(DN_HEADS):
            qx[part * DN_HEADS + hd, SHORT_PAD:SHORT_PAD + tl, :] = y[:, hd * LANES:(hd + 1) * LANES]

    glu = proj(0, OFF_QKV) + bglu_ref[...]
    u = glu[:, :CONV_CH] * _sigmoid(glu[:, CONV_CH:])
    for sl in range(n_cs):
        xx[sl, CONV_PAD:CONV_PAD + tl, :] = u[:, sl * LANES:(sl + 1) * LANES]
    qkv_part(0)
    conv_a(0)
    qkv_part(1)
    conv_a(1)
    qkv_part(2)
    conv_a(2)
    og_s[...] = proj(OFF_OG, N_A)
    conv_a(3)
    ga_s[:, :DN_V] = gate(0, DN_V)
    conv_b(0)
    ga_s[:, DN_V:] = gate(DN_V, D_MODEL)
    conv_b(1)
    gb_s[:, :DN_V] = gate(D_MODEL, D_MODEL + DN_V)
    conv_b(2)
    gb_s[:, DN_V:] = gate(D_MODEL + DN_V, N_G)
    bd = jnp.dot(xb, wbd_ref[...], preferred_element_type=F32)
    for rb in range(tl // CHUNK):
        r0 = rb * CHUNK
        cv = jnp.concatenate([ya[sl, r0:r0 + CHUNK, :] for sl in range(n_cs)], axis=1) + bdw_ref[...]
        cvn_s[r0:r0 + CHUNK, :] = _silu(_ln(cv, lcg_ref[...], lcb_ref[...])).astype(BF16)

    beta = _sigmoid(bd)
    ld = -jnp.exp(alog_ref[...]) * _softplus(bd + dtb_ref[...])
    rt = lax.broadcasted_iota(jnp.int32, (tl, tl), 0)
    ct = lax.broadcasted_iota(jnp.int32, (tl, tl), 1)
    blk_tril = jnp.where((rt >= ct) & ((rt // c) == (ct // c)), 1.0, 0.0).astype(BF16)
    gsum = _dot_exact_lhs(blk_tril, ld)
    beta3 = beta.reshape(nc, c, LANES)
    gsum3 = gsum.reshape(nc, c, LANES)
    gsum_t = gsum.T
    glast3 = gsum3[:, c - 1:c, :]
    eg3 = jnp.exp(gsum3)
    etail3 = jnp.exp(glast3 - gsum3)
    egc3 = jnp.exp(glast3)

    ri = lax.broadcasted_iota(jnp.int32, (c, c), 0)
    ci = lax.broadcasted_iota(jnp.int32, (c, c), 1)
    tril = (ri >= ci)[None]
    strict = (ri > ci)[None]
    eye = jnp.where(ri == ci, 1.0, 0.0)[None]

    for hd in range(DN_HEADS):
        lb = slice(hd, hd + 1)
        lg = slice(DN_HEADS + hd, DN_HEADS + hd + 1)
        q = _l2n(qs[hd].reshape(nc, c, DN_DK)) * (DN_DK ** -0.5)
        k = _l2n(ks[hd].reshape(nc, c, DN_DK))
        v = vs[hd].reshape(nc, c, DN_DV)
        b_col = beta3[:, :, lb]
        g_row = jnp.stack([gsum_t[DN_HEADS + hd:DN_HEADS + hd + 1, ch * c:(ch + 1) * c] for ch in range(nc)])
        diff = gsum3[:, :, lg] - g_row
        decay = jnp.where(tril, jnp.exp(jnp.where(tril, diff, 0.0)), 0.0)
        kq = _bmm_nt(jnp.concatenate([k, q], axis=1), k)
        a = jnp.where(strict, b_col * kq[:, :c] * decay, 0.0)
        aqk = kq[:, c:] * decay
        p = eye - a
        x = a
        for _ in range(5):
            x = _bmm(x, x)
            p = p + _bmm(p, x)
        rhs = jnp.concatenate([(b_col * eg3[:, :, lg]) * k, b_col * v], axis=2)
        sol = _bmm(p, rhs)
        qo = _bmm(aqk, sol)
        q2_s[hd] = (eg3[:, :, lg] * q - qo[:, :, :DN_DK]).astype(BF16)
        o1_s[hd] = qo[:, :, DN_DK:]
        kn = _bmm_tn(k * etail3[:, :, lg], sol)
        ktw_s[hd] = kn[:, :, :DN_DK].astype(BF16)
        n_s[hd] = kn[:, :, DN_DK:]
        egc_s[hd] = jnp.broadcast_to(egc3[:, :, lg], (nc, 1, DN_DV))

    def carry_body(ch, carry):
        for hd in range(DN_HEADS):
            s_h = S[hd]
            s_b = s_h.astype(BF16)
            sl_s[hd, ch] = s_b
            S[hd] = egc_s[hd, ch] * s_h + (n_s[hd, ch] - jnp.dot(ktw_s[hd, ch], s_b,
                                                              preferred_element_type=F32))
        return carry

    for ch in range(nc):
        carry_body(ch, 0)

    for hd in range(DN_HEADS):
        ls = slice(hd * DN_DK, (hd + 1) * DN_DK)
        o = o1_s[hd] + jnp.einsum('bck,bkv->bcv', q2_s[hd], sl_s[hd], preferred_element_type=F32)
        o = o * lax.rsqrt(jnp.mean(o * o, -1, keepdims=True) + RMS_EPS) * dng_ref[...]
        on_s[:, ls] = (o.reshape(tl, DN_DV) * _silu(og_s[:, ls])).astype(BF16)

    mixed = _sigmoid(ga_s[...]) * jnp.dot(cvn_s[...], wca_ref[...], preferred_element_type=F32)
    mixed = mixed + _sigmoid(gb_s[...]) * jnp.dot(on_s[...], wdn_ref[...], preferred_element_type=F32)
    h = _ln(DEEPNORM_ALPHA * x_ref[...] + _dot(mixed, wo_ref[...]), g1_ref[...], b1_ref[...])
    h_ref[...] = h
    xq_ref[...] = _dot(h, wxq_ref[...]).astype(BF16)

    @pl.when(t == nt - 1)
    def _():
        for sl in range(n_cs):
            pconv_ref[0, :, sl * LANES:(sl + 1) * LANES] = xx[sl, tl:tl + CONV_PAD, :]
        for s12 in range(n_qs):
            pqkv_ref[0, :, s12 * LANES:(s12 + 1) * LANES] = qx[s12, tl:tl + SHORT_PAD, :]
        sout_ref[0] = S[...]

    for sl in range(n_cs):
        xx[sl, 0:CONV_PAD, :] = xx[sl, tl:tl + CONV_PAD, :]
    for s12 in range(n_qs):
        qx[s12, 0:SHORT_PAD, :] = qx[s12, tl:tl + SHORT_PAD, :]


def _pfront(x, w_a, w_g, w_bd, bglu, wdw, bdw, lcg, lcb, wsh, alog, dtb, dng, wca, wdn, wo, wxq, g1, b1, batch, seq):
    tl = PM_TL
    nt = seq // tl
    nc = tl // CHUNK
    n = batch * seq
    row = lambda w: pl.BlockSpec((tl, w), lambda b, t: (b * nt + t, 0))
    per_seq = lambda *s: pl.BlockSpec((1,) + s, lambda b, t: (b,) + (0,) * len(s))
    return pl.pallas_call(
        _pfront_kernel,
        grid=(batch, nt),
        in_specs=[row(D_MODEL), _wspec((D_MODEL, N_A)), _wspec((D_MODEL, N_G)), _wspec((D_MODEL, LANES)),
                  _wspec((1, 2 * CONV_CH)),
                  _wspec((CONV_CH // LANES, CONV_K, 8, LANES)), _wspec((1, CONV_CH)), _wspec((1, CONV_CH)),
                  _wspec((1, CONV_CH)), _wspec((QKV_W // LANES, SHORT_K, 8, LANES)), _wspec((1, LANES)),
                  _wspec((1, LANES)), _wspec((1, DN_DV)),
                  _wspec((CONV_CH, D_MODEL)), _wspec((DN_V, D_MODEL)), _wspec((D_MODEL, D_MODEL)),
                  _wspec((D_MODEL, D_MODEL)), _wspec((1, D_MODEL)), _wspec((1, D_MODEL))],
        out_specs=[row(D_MODEL), row(D_MODEL), per_seq(CONV_PAD, CONV_CH), per_seq(SHORT_PAD, QKV_W),
                   per_seq(DN_HEADS, DN_DK, DN_DV)],
        out_shape=[jax.ShapeDtypeStruct((n, D_MODEL), F32), jax.ShapeDtypeStruct((n, D_MODEL), BF16),
                   jax.ShapeDtypeStruct((batch, CONV_PAD, CONV_CH), F32),
                   jax.ShapeDtypeStruct((batch, SHORT_PAD, QKV_W), F32),
                   jax.ShapeDtypeStruct((batch, DN_HEADS, DN_DK, DN_DV), F32)],
        scratch_shapes=[pltpu.VMEM((CONV_CH // LANES, tl + CONV_PAD, LANES), F32),
                        pltpu.VMEM((QKV_W // LANES, tl + SHORT_PAD, LANES), F32),
                        pltpu.VMEM((CONV_CH // LANES, tl, LANES), F32),
                        pltpu.VMEM((DN_HEADS, tl, DN_DK), F32), pltpu.VMEM((DN_HEADS, tl, DN_DK), F32),
                        pltpu.VMEM((DN_HEADS, tl, DN_DV), F32),
                        pltpu.VMEM((DN_HEADS, DN_DK, DN_DV), F32),
                        pltpu.VMEM((DN_HEADS, nc, DN_DK, DN_DK), BF16),
                        pltpu.VMEM((DN_HEADS, nc, DN_DK, DN_DV), F32),
                        pltpu.VMEM((DN_HEADS, nc, CHUNK, DN_DK), BF16),
                        pltpu.VMEM((DN_HEADS, nc, CHUNK, DN_DV), F32),
                        pltpu.VMEM((DN_HEADS, nc, DN_DK, DN_DV), BF16),
                        pltpu.VMEM((DN_HEADS, nc, 1, DN_DV), F32),
                        pltpu.VMEM((tl, CONV_CH), BF16), pltpu.VMEM((tl, DN_V), BF16),
                        pltpu.VMEM((tl, DN_V), F32), pltpu.VMEM((tl, D_MODEL), F32),
                        pltpu.VMEM((tl, D_MODEL), F32)],
        compiler_params=_params(("parallel", "arbitrary")),
        name="prompt_front",
    )(x, w_a, w_g, w_bd, bglu, wdw, bdw, lcg, lcb, wsh, alog, dtb, dng, wca, wdn, wo, wxq, g1, b1)


SM_BT = 32
SM_BS = 8


def _smid1_kernel(u_ref, qkv_ref, bd_ref, og_ref, hc_ref, hq_ref, wdw_ref, bdw_ref, lcg_ref, lcb_ref, wsh_ref,
                  alog_ref, dtb_ref,
                  cvn_ref, cnew_ref, qnew_ref, w_ref, q2_ref, uv_ref, o1_ref, kt_ref, ogs_ref, egc_ref):
    nl = u_ref.shape[0]
    nh = CONV_K - 1
    xx = [hc_ref[i] for i in range(nh)] + [u_ref[i] for i in range(nl)]
    for t in range(nl):
        acc = wdw_ref[0:1, :] * xx[t]
        for j in range(1, CONV_K):
            acc = acc + wdw_ref[j:j + 1, :] * xx[t + j]
        cvn_ref[t] = _silu(_ln(acc + bdw_ref[...], lcg_ref[...], lcb_ref[...]))
    for i in range(nh):
        cnew_ref[i] = xx[nl + i]

    ns = SHORT_K - 1
    qq = [hq_ref[i] for i in range(ns)] + [qkv_ref[i] for i in range(nl)]
    for i in range(ns):
        qnew_ref[i] = qq[nl + i]
    qkv = []
    for t in range(nl):
        acc = wsh_ref[0:1, :] * qq[t]
        for j in range(1, SHORT_K):
            acc = acc + wsh_ref[j:j + 1, :] * qq[t + j]
        qkv.append(_silu(acc))

    beta = [_sigmoid(bd_ref[t]) for t in range(nl)]
    ld = [-jnp.exp(alog_ref[...]) * _softplus(bd_ref[t] + dtb_ref[...]) for t in range(nl)]

    out_refs = (w_ref, uv_ref, o1_ref, q2_ref, kt_ref)
    rows = [[[None] * DN_HEADS for _ in range(nl)] for _ in out_refs]
    for hd in range(DN_HEADS):
        q = [_l2n(qkv[t][:, hd * DN_DK:(hd + 1) * DN_DK]) * (DN_DK ** -0.5) for t in range(nl)]
        k = [_l2n(qkv[t][:, DN_QK + hd * DN_DK:DN_QK + (hd + 1) * DN_DK]) for t in range(nl)]
        v = [qkv[t][:, 2 * DN_QK + hd * DN_DV:2 * DN_QK + (hd + 1) * DN_DV] for t in range(nl)]
        b = [beta[t][:, hd:hd + 1] for t in range(nl)]
        g = [ld[t][:, DN_HEADS + hd:DN_HEADS + hd + 1] for t in range(nl)]
        gc = [g[0]]
        for t in range(1, nl):
            gc.append(gc[t - 1] + g[t])
        w_l, uv_l = [], []
        for i in range(nl):
            wi = (b[i] * jnp.exp(gc[i])) * k[i]
            ui = b[i] * v[i]
            for j in range(i):
                a_ij = b[i] * jnp.sum(k[i] * k[j], -1, keepdims=True) * jnp.exp(gc[i] - gc[j])
                wi = wi - a_ij * w_l[j]
                ui = ui - a_ij * uv_l[j]
            w_l.append(wi)
            uv_l.append(ui)
        for i in range(nl):
            o1 = jnp.zeros_like(v[i])
            q2 = jnp.exp(gc[i]) * q[i]
            for j in range(i + 1):
                aqk = jnp.sum(q[i] * k[j], -1, keepdims=True) * jnp.exp(gc[i] - gc[j])
                o1 = o1 + aqk * uv_l[j]
                q2 = q2 - aqk * w_l[j]
            for r, val in zip(rows, (w_l[i], uv_l[i], o1, q2, k[i] * jnp.exp(gc[nl - 1] - gc[i]))):
                r[i][hd] = val
        egc_ref[hd] = jnp.broadcast_to(jnp.exp(gc[nl - 1]), (u_ref.shape[1], LANES))
    for i in range(nl):
        for ref, r in zip(out_refs, rows):
            ref[:, i, :] = jnp.concatenate(r[i], axis=1)
        ogs_ref[:, i, :] = og_ref[i]


def _smid1(u, qkv, bd, og, hc, hq, wdw, bdw, lcg, lcb, wsh, alog, dtb):
    nl, b, _ = u.shape
    bt = SM_BT
    tm = lambda r, w: pl.BlockSpec((r, bt, w), lambda i: (0, i, 0))
    sm = pl.BlockSpec((bt, nl, DN_V), lambda i: (i, 0, 0))
    sm_shape = jax.ShapeDtypeStruct((b, nl, DN_V), F32)
    tm_outs = [(nl, CONV_CH), (CONV_K - 1, CONV_CH), (SHORT_K - 1, QKV_W)]
    return pl.pallas_call(
        _smid1_kernel,
        grid=(b // bt,),
        in_specs=[tm(nl, CONV_CH), tm(nl, QKV_W), tm(nl, LANES), tm(nl, DN_V), tm(CONV_K - 1, CONV_CH),
                  tm(SHORT_K - 1, QKV_W),
                  _wspec((CONV_PAD, CONV_CH)), _wspec((1, CONV_CH)), _wspec((1, CONV_CH)),
                  _wspec((1, CONV_CH)), _wspec((SHORT_K, QKV_W)), _wspec((1, LANES)),
                  _wspec((1, LANES))],
        out_specs=[tm(r, w) for r, w in tm_outs] + [sm] * 6 + [tm(DN_HEADS, LANES)],
        out_shape=[jax.ShapeDtypeStruct((r, b, w), F32) for r, w in tm_outs] + [sm_shape] * 6
        + [jax.ShapeDtypeStruct((DN_HEADS, b, LANES), F32)],
        compiler_params=_params(("parallel",)),
        name="sample_mid1",
    )(u, qkv, bd, og, hc, hq, wdw, bdw, lcg, lcb, wsh, alog, dtb)


def _smid2_kernel(w_ref, q2_ref, uv_ref, o1_ref, kt_ref, og_ref, egc_ref, s_ref, dng_ref,
                  on_ref, snew_ref):
    nl = w_ref.shape[1]
    for hd in range(DN_HEADS):
        ls = slice(hd * DN_DK, (hd + 1) * DN_DK)
        s_h = s_ref[:, hd]
        lhs = jnp.concatenate([w_ref[:, :, ls], q2_ref[:, :, ls]], axis=1)
        r = jnp.einsum('bck,bkv->bcv', lhs.astype(BF16), s_h.astype(BF16),
                       preferred_element_type=F32)
        u_new = uv_ref[:, :, ls] - r[:, :nl]
        o = o1_ref[:, :, ls] + r[:, nl:]
        upd = jnp.einsum('bck,bcv->bkv', kt_ref[:, :, ls].astype(BF16), u_new.astype(BF16),
                         preferred_element_type=F32)
        snew_ref[:, hd] = egc_ref[:, hd] * s_h + upd
        o = o * lax.rsqrt(jnp.mean(o * o, -1, keepdims=True) + RMS_EPS) * dng_ref[...]
        on_ref[:, :, ls] = o * _silu(og_ref[:, :, ls])


def _smid2(w, q2, uv, o1, kt, og, egc, s, dng):
    b, nl, _ = w.shape
    bs = SM_BS
    sq = pl.BlockSpec((bs, nl, DN_V), lambda i: (i, 0, 0))
    st = pl.BlockSpec((bs, DN_HEADS, DN_DK, DN_DV), lambda i: (i, 0, 0, 0))
    return pl.pallas_call(
        _smid2_kernel,
        grid=(b // bs,),
        in_specs=[sq, sq, sq, sq, sq, sq,
                  pl.BlockSpec((bs, DN_HEADS, 1, LANES), lambda i: (i, 0, 0, 0)), st,
                  _wspec((1, DN_DV))],
        out_specs=[sq, st],
        out_shape=[jax.ShapeDtypeStruct((b, nl, DN_V), F32),
                   jax.ShapeDtypeStruct((b, DN_HEADS, DN_DK, DN_DV), F32)],
        compiler_params=_params(("parallel",)),
        name="sample_mid2",
    )(w, q2, uv, o1, kt, og, egc, s, dng)


def _tm(x):
    return jnp.swapaxes(x, 0, 1)


def kernel(x_prompt, x_sample, mem_prompt, state_conv, state_qkv_conv, state_delta, cache_mem_k, cache_mem_v, w_in, b_glu, w_dw, b_dw, ln_conv_g, ln_conv_b, w_conv_out, w_short, a_log, dt_bias, dn_norm_g, w_dn_out, w_o, ln1_g, ln1_b, w_xq, w_mem_kv, w_xo, ln2_g, ln2_b, w_ff1, w_ff2, ln3_g, ln3_b):
    assert w_in.shape[0] == DEPTH == 1
    bp, lp, _ = x_prompt.shape
    bsm, ls_, _ = x_sample.shape

    w = w_in[0]
    o_ga = N_A + 2 * DN_HEADS
    w_a = w[:, :N_A].astype(BF16)
    w_g = w[:, o_ga:o_ga + N_G].astype(BF16)
    w_bd = jnp.concatenate([w[:, N_A:o_ga], jnp.zeros((D_MODEL, LANES - 2 * DN_HEADS), w.dtype)],
                           axis=1).astype(BF16)
    lane_pad = lambda a: jnp.concatenate(
        [jnp.zeros((DN_HEADS,), F32), a.astype(F32), jnp.zeros((LANES - 2 * DN_HEADS,), F32)])[None]
    alog = lane_pad(a_log[0])
    dtb = lane_pad(dt_bias[0])
    wdw = jnp.concatenate([w_dw[0], jnp.zeros((CONV_PAD - CONV_K, CONV_CH), F32)], axis=0)
    r2 = lambda a: a[0][None]
    bglu, bdw, lcg, lcb = r2(b_glu), r2(b_dw), r2(ln_conv_g), r2(ln_conv_b)
    dng = r2(dn_norm_g)
    wsh = w_short[0]
    slabs = lambda a: jnp.broadcast_to(
        jnp.swapaxes(a.reshape(a.shape[0], a.shape[1] // LANES, LANES), 0, 1)[:, :, None, :],
        (a.shape[1] // LANES, a.shape[0], 8, LANES))
    wdw_slab = slabs(w_dw[0])
    wsh_slab = slabs(wsh)
    wca, wdn, wo, wxq, wxo = (a[0].astype(BF16) for a in (w_conv_out, w_dn_out, w_o, w_xq, w_xo))
    w1, w2, wkv = w_ff1[0].astype(BF16), w_ff2[0].astype(BF16), w_mem_kv[0].astype(BF16)
    g1, b1, g2, b2, g3, b3 = (r2(a) for a in (ln1_g, ln1_b, ln2_g, ln2_b, ln3_g, ln3_b))

    n_s = bsm * ls_
    xs = _tm(x_sample).reshape(n_s, D_MODEL)
    u, qkv, og, ga, gb, bd = _inproj(xs, w_a, w_g, w_bd, bglu, 256)
    t3 = lambda a: a.reshape(ls_, bsm, a.shape[-1])
    cvn, c_new, q_new, w_, q2, uv, o1, kt, og_s, egc = _smid1(
        t3(u), t3(qkv), t3(bd), t3(og), _tm(state_conv[0]), _tm(state_qkv_conv[0]),
        wdw, bdw, lcg, lcb, wsh, alog, dtb)
    on, s_delta = _smid2(w_, q2, uv, o1, kt, og_s, _tm(egc)[:, :, None, :], state_delta[0], dng)
    h_s, xq_s = _merge(cvn.reshape(n_s, CONV_CH), _tm(on).reshape(n_s, DN_V), ga, gb, xs,
                       wca, wdn, wo, wxq, g1, b1, 512)

    n_p = bp * lp
    xp = x_prompt.reshape(n_p, D_MODEL)
    mk, mv, mk_b, mv_b = _memkv(mem_prompt.reshape(bp * N_MEM, D_MODEL), wkv, 512)
    h, xq, p_conv, p_qkv, p_delta = _pfront(xp, w_a, w_g, w_bd, bglu, wdw_slab, bdw, lcg, lcb, wsh_slab,
                                            alog, dtb, dng, wca, wdn, wo, wxq, g1, b1, bp, lp)
    y_p, att_s = _ptail(xq, mk_b.reshape(bp, N_MEM, D_MODEL), mv_b.reshape(bp, N_MEM, D_MODEL), h,
                        wxo, w1, w2, g2, b2, g3, b3, _tm(t3(xq_s)), cache_mem_k, cache_mem_v, bp, lp, 512)
    y_p = y_p.reshape(bp, lp, D_MODEL)
    p_conv = p_conv[:, CONV_PAD - (CONV_K - 1):]
    p_qkv = p_qkv[:, SHORT_PAD - (SHORT_K - 1):]
    p_mk = mk.reshape(bp, N_MEM, X_HEADS, X_HEAD_DIM)
    p_mv = mv.reshape(bp, N_MEM, X_HEADS, X_HEAD_DIM)

    y_s = _tail(_tm(att_s).reshape(n_s, D_MODEL), h_s, wxo, w1, w2, g2, b2, g3, b3, 512)
    y_s = _tm(t3(y_s))

    return (y_p, y_s, p_conv[None], p_qkv[None], p_delta[None], p_mk[None], p_mv[None],
            _tm(c_new)[None], _tm(q_new)[None], s_delta[None])
```

```python
import functools

import jax
import jax.numpy as jnp
from jax import lax
from jax.experimental import pallas as pl
from jax.experimental.pallas import tpu as pltpu

F32 = jnp.float32
BF16 = jnp.bfloat16

D_MODEL = 1024
N_MEM = 256
CONV_CH = 512
CONV_K = 31
DN_HEADS = 4
DN_DK = 128
DN_DV = 128
DN_QK = DN_HEADS * DN_DK
DN_V = DN_HEADS * DN_DV
QKV_W = 2 * DN_QK + DN_V
SHORT_K = 4
CHUNK = 64
X_HEADS = 4
X_HEAD_DIM = D_MODEL // X_HEADS
D_FF = 4 * D_MODEL
DEPTH = 1
DEEPNORM_ALPHA = (2 * DEPTH) ** 0.25
LN_EPS = 1e-5
RMS_EPS = 1e-6
L2_EPS = 1e-6

LANES = 128
OFF_QKV = 2 * CONV_CH
OFF_OG = OFF_QKV + QKV_W
N_A = OFF_OG + DN_V
N_G = 2 * D_MODEL
VMEM_LIMIT = 56 * 1024 * 1024


def _dot(a, b):
    return jnp.dot(a.astype(BF16), b.astype(BF16), preferred_element_type=F32)


def _dot_nt(a, b):
    return lax.dot_general(a.astype(BF16), b.astype(BF16), (((1,), (1,)), ((), ())),
                           preferred_element_type=F32)


def _sigmoid(x):
    return 1.0 / (1.0 + jnp.exp(-x))


def _silu(x):
    return x * _sigmoid(x)


def _softplus(x):
    return jnp.maximum(x, 0.0) + jnp.log(1.0 + jnp.exp(-jnp.abs(x)))


def _ln(x, g, b):
    mu = jnp.mean(x, -1, keepdims=True)
    xc = x - mu
    var = jnp.mean(xc * xc, -1, keepdims=True)
    return xc * lax.rsqrt(var + LN_EPS) * g + b


def _l2n(x):
    return x * lax.rsqrt(jnp.sum(x * x, -1, keepdims=True) + L2_EPS)


def _wspec(shape):
    return pl.BlockSpec(shape, lambda *_: (0,) * len(shape), pipeline_mode=pl.Buffered(1))


def _params(sem):
    return pltpu.CompilerParams(dimension_semantics=sem, vmem_limit_bytes=VMEM_LIMIT)


def _inproj_kernel(x_ref, wa_ref, wg_ref, wbd_ref, bglu_ref, u_ref, qkv_ref, og_ref, ga_ref, gb_ref, bd_ref):
    x = x_ref[...].astype(BF16)
    glu = jnp.dot(x, wa_ref[:, 0:OFF_QKV], preferred_element_type=F32) + bglu_ref[...]
    u_ref[...] = glu[:, :CONV_CH] * _sigmoid(glu[:, CONV_CH:])
    qkv_ref[...] = jnp.dot(x, wa_ref[:, OFF_QKV:OFF_OG], preferred_element_type=F32)
    og_ref[...] = jnp.dot(x, wa_ref[:, OFF_OG:N_A], preferred_element_type=F32)
    ga_ref[...] = jnp.dot(x, wg_ref[:, :D_MODEL], preferred_element_type=F32)
    gb_ref[...] = jnp.dot(x, wg_ref[:, D_MODEL:], preferred_element_type=F32)
    bd_ref[...] = jnp.dot(x, wbd_ref[...], preferred_element_type=F32)


def _inproj(x, w_a, w_g, w_bd, b_glu, tm):
    n = x.shape[0]
    widths = (CONV_CH, QKV_W, DN_V, D_MODEL, D_MODEL, LANES)
    row = lambda w: pl.BlockSpec((tm, w), lambda i: (i, 0))
    return pl.pallas_call(
        _inproj_kernel,
        grid=(n // tm,),
        in_specs=[row(D_MODEL), _wspec((D_MODEL, N_A)), _wspec((D_MODEL, N_G)), _wspec((D_MODEL, LANES)),
                  _wspec((1, 2 * CONV_CH))],
        out_specs=[row(w) for w in widths],
        out_shape=[jax.ShapeDtypeStruct((n, w), F32) for w in widths],
        compiler_params=_params(("parallel",)),
        name="inproj",
    )(x, w_a, w_g, w_bd, b_glu)


def _merge_kernel(cvn_ref, on_ref, ga_ref, gb_ref, x_ref, wca_ref, wdn_ref, wo_ref, wxq_ref,
                  g1_ref, b1_ref, h_ref, xq_ref):
    br_a = _dot(cvn_ref[...], wca_ref[...])
    br_b = _dot(on_ref[...], wdn_ref[...])
    mixed = _sigmoid(ga_ref[...]) * br_a + _sigmoid(gb_ref[...]) * br_b
    h = _ln(DEEPNORM_ALPHA * x_ref[...] + _dot(mixed, wo_ref[...]), g1_ref[...], b1_ref[...])
    h_ref[...] = h
    xq_ref[...] = _dot(h, wxq_ref[...])


def _merge(cvn, on, ga, gb, x, wca, wdn, wo, wxq, g1, b1, tm):
    n = x.shape[0]
    row = lambda w: pl.BlockSpec((tm, w), lambda i: (i, 0))
    return pl.pallas_call(
        _merge_kernel,
        grid=(n // tm,),
        in_specs=[row(CONV_CH), row(DN_V), row(D_MODEL), row(D_MODEL), row(D_MODEL),
                  _wspec((CONV_CH, D_MODEL)), _wspec((DN_V, D_MODEL)), _wspec((D_MODEL, D_MODEL)),
                  _wspec((D_MODEL, D_MODEL)), _wspec((1, D_MODEL)), _wspec((1, D_MODEL))],
        out_specs=[row(D_MODEL), row(D_MODEL)],
        out_shape=[jax.ShapeDtypeStruct((n, D_MODEL), F32)] * 2,
        compiler_params=_params(("parallel",)),
        name="merge",
    )(cvn, on, ga, gb, x, wca, wdn, wo, wxq, g1, b1)


FF_BLOCK = 1024


def _tail_kernel(att_ref, h_ref, wxo_ref, w1_ref, w2_ref, g2_ref, b2_ref, g3_ref, b3_ref, y_ref):
    xo = _dot(att_ref[...], wxo_ref[...])
    h2 = _ln(DEEPNORM_ALPHA * h_ref[...] + xo, g2_ref[...], b2_ref[...])
    h2b = h2.astype(BF16)
    ff = jnp.zeros(h2.shape, F32)
    for c in range(D_FF // FF_BLOCK):
        a = jnp.dot(h2b, w1_ref[:, c * FF_BLOCK:(c + 1) * FF_BLOCK], preferred_element_type=F32)
        a = jnp.square(jnp.maximum(a, 0.0))
        ff = ff + jnp.dot(a.astype(BF16), w2_ref[c * FF_BLOCK:(c + 1) * FF_BLOCK, :],
                          preferred_element_type=F32)
    y_ref[...] = _ln(DEEPNORM_ALPHA * h2 + ff, g3_ref[...], b3_ref[...])


def _tail(att, h, wxo, w1, w2, g2, b2, g3, b3, tm):
    n = h.shape[0]
    row = lambda w: pl.BlockSpec((tm, w), lambda i: (i, 0))
    return pl.pallas_call(
        _tail_kernel,
        grid=(n // tm,),
        in_specs=[row(D_MODEL), row(D_MODEL), _wspec((D_MODEL, D_MODEL)), _wspec((D_MODEL, D_FF)),
                  _wspec((D_FF, D_MODEL)), _wspec((1, D_MODEL)), _wspec((1, D_MODEL)),
                  _wspec((1, D_MODEL)), _wspec((1, D_MODEL))],
        out_specs=row(D_MODEL),
        out_shape=jax.ShapeDtypeStruct((n, D_MODEL), F32),
        compiler_params=_params(("parallel",)),
        name="tail",
    )(att, h, wxo, w1, w2, g2, b2, g3, b3)


def _memkv_kernel(m_ref, w_ref, k_ref, v_ref, kb_ref, vb_ref):
    m = m_ref[...].astype(BF16)
    tm = m.shape[0]
    k = jnp.dot(m, w_ref[:, :D_MODEL], preferred_element_type=F32)
    v = jnp.dot(m, w_ref[:, D_MODEL:], preferred_element_type=F32)
    k_ref[...] = k.reshape(tm, X_HEADS, X_HEAD_DIM)
    v_ref[...] = v.reshape(tm, X_HEADS, X_HEAD_DIM)
    kb_ref[...] = k.astype(BF16)
    vb_ref[...] = v.astype(BF16)


def _memkv(mem, w, tm):
    n = mem.shape[0]
    row = pl.BlockSpec((tm, D_MODEL), lambda i: (i, 0))
    row4 = pl.BlockSpec((tm, X_HEADS, X_HEAD_DIM), lambda i: (i, 0, 0))
    return pl.pallas_call(
        _memkv_kernel,
        grid=(n // tm,),
        in_specs=[row, _wspec((D_MODEL, 2 * D_MODEL))],
        out_specs=[row4, row4, row, row],
        out_shape=[jax.ShapeDtypeStruct((n, X_HEADS, X_HEAD_DIM), F32)] * 2
        + [jax.ShapeDtypeStruct((n, D_MODEL), BF16)] * 2,
        compiler_params=_params(("parallel",)),
        name="memkv",
    )(mem, w)


def _softmax(sc):
    sc = sc - jnp.max(sc, -1, keepdims=True)
    e = jnp.exp(sc)
    return e / jnp.sum(e, -1, keepdims=True)


def _ptail_kernel(q_ref, k_ref, v_ref, h_ref, wxo_ref, w1_ref, w2_ref, g2_ref, b2_ref, g3_ref, b3_ref,
                  sq_ref, sk_ref, sv_ref, y_ref, so_ref, att_s, *, nseq):
    scale = X_HEAD_DIM ** -0.5
    for hd in range(X_HEADS):
        sl = slice(hd * X_HEAD_DIM, (hd + 1) * X_HEAD_DIM)
        pr = _softmax(_dot_nt(q_ref[:, sl], k_ref[0, :, sl]) * scale)
        att_s[:, sl] = _dot(pr, v_ref[0, :, sl]).astype(BF16)
    _attn_split_kernel(sq_ref, sk_ref, sv_ref, so_ref, nseq=nseq)
    _tail_kernel(att_s, h_ref, wxo_ref, w1_ref, w2_ref, g2_ref, b2_ref, g3_ref, b3_ref, y_ref)


def _ptail(xq, kb, vb, h, wxo, w1, w2, g2, b2, g3, b3, sq, sk, sv, batch, seq, tq):
    nt = seq // tq
    sb, sl_, _ = sq.shape
    nseq = sb // (batch * nt)
    row = lambda: pl.BlockSpec((tq, D_MODEL), lambda b, t: (b * nt + t, 0))
    mem = pl.BlockSpec((1, N_MEM, D_MODEL), lambda b, t: (b, 0, 0))
    s_q = pl.BlockSpec((nseq, sl_, D_MODEL), lambda b, t: (b * nt + t, 0, 0))
    s_kv = pl.BlockSpec((None, nseq, N_MEM, X_HEADS, X_HEAD_DIM), lambda b, t: (0, b * nt + t, 0, 0, 0))
    return pl.pallas_call(
        functools.partial(_ptail_kernel, nseq=nseq),
        grid=(batch, nt),
        in_specs=[row(), mem, mem, row(), _wspec((D_MODEL, D_MODEL)), _wspec((D_MODEL, D_FF)),
                  _wspec((D_FF, D_MODEL)), _wspec((1, D_MODEL)), _wspec((1, D_MODEL)),
                  _wspec((1, D_MODEL)), _wspec((1, D_MODEL)), s_q, s_kv, s_kv],
        out_specs=[row(), s_q],
        out_shape=[jax.ShapeDtypeStruct((batch * seq, D_MODEL), F32),
                   jax.ShapeDtypeStruct((sb, sl_, D_MODEL), F32)],
        scratch_shapes=[pltpu.VMEM((tq, D_MODEL), BF16)],
        compiler_params=_params(("parallel", "parallel")),
        name="prompt_tail",
    )(xq, kb, vb, h, wxo, w1, w2, g2, b2, g3, b3, sq, sk, sv)


def _attn_split_kernel(q_ref, k_ref, v_ref, o_ref, *, nseq):
    scale = X_HEAD_DIM ** -0.5
    nl = q_ref.shape[1]
    nk = N_MEM * X_HEADS
    row_head = lax.broadcasted_iota(jnp.int32, (X_HEADS * nl, nk), 0) // nl
    col_head = lax.broadcasted_iota(jnp.int32, (X_HEADS * nl, nk), 1) % X_HEADS
    own = row_head == col_head
    for s in range(nseq):
        k2 = k_ref[s].reshape(nk, X_HEAD_DIM)
        v2 = v_ref[s].reshape(nk, X_HEAD_DIM)
        q = q_ref[s]
        q4 = jnp.concatenate([q[:, h * X_HEAD_DIM:(h + 1) * X_HEAD_DIM] for h in range(X_HEADS)], axis=0)
        pr = _softmax(jnp.where(own, _dot_nt(q4, k2) * scale, -1e30))
        o4 = _dot(pr, v2)
        for h in range(X_HEADS):
            o_ref[s, :, h * X_HEAD_DIM:(h + 1) * X_HEAD_DIM] = o4[h * nl:(h + 1) * nl]


PM_TL = 512
ROW_STRIDE = 4
CONV_RB = 8 * ROW_STRIDE
CONV_PAD = 32
SHORT_PAD = 8


def _split3(x):
    h = x.astype(BF16)
    r = x - h.astype(F32)
    m = r.astype(BF16)
    l = (r - m.astype(F32)).astype(BF16)
    return h, m, l


def _dot_exact_lhs(lhs_bf16, x):
    h, m, l = _split3(x)
    d = lambda p: jnp.dot(lhs_bf16, p, preferred_element_type=F32)
    return d(h) + d(m) + d(l)


def _bmm(a, b):
    return jnp.einsum('bij,bjk->bik', a.astype(BF16), b.astype(BF16), preferred_element_type=F32)


def _bmm_nt(a, b):
    return jnp.einsum('bik,bjk->bij', a.astype(BF16), b.astype(BF16), preferred_element_type=F32)


def _bmm_tn(a, b):
    return jnp.einsum('bki,bkj->bij', a.astype(BF16), b.astype(BF16), preferred_element_type=F32)


def _strided_conv(src_ref, w_ref, dst_ref, slab, dslab, nblk, ntaps, off, post):
    ws = [w_ref[slab, j] for j in range(ntaps)]

    def body(rb, carry):
        r0 = pl.multiple_of(rb * CONV_RB, CONV_RB)
        accs = [None] * ROW_STRIDE
        for q in range(off, off + ntaps + ROW_STRIDE - 1):
            x = src_ref[slab, pl.ds(r0 + q, 8, stride=ROW_STRIDE), :]
            for m in range(ROW_STRIDE):
                j = q - off - m
                if 0 <= j < ntaps:
                    term = ws[j] * x
                    accs[m] = term if accs[m] is None else accs[m] + term
        for m in range(ROW_STRIDE):
            dst_ref[dslab, pl.ds(r0 + m, 8, stride=ROW_STRIDE), :] = post(accs[m])
        return carry

    lax.fori_loop(0, nblk, body, 0, unroll=True)


def _pfront_kernel(x_ref, wa_ref, wg_ref, wbd_ref, bglu_ref, wdw_ref, bdw_ref, lcg_ref, lcb_ref, wsh_ref, alog_ref, dtb_ref,
                   dng_ref, wca_ref, wdn_ref, wo_ref, wxq_ref, g1_ref, b1_ref,
                   h_ref, xq_ref, pconv_ref, pqkv_ref, sout_ref,
                   xx, qx, ya, qs, ks, vs, S, ktw_s, n_s, q2_s, o1_s, sl_s, egc_s, cvn_s, on_s, og_s, ga_s, gb_s):
    t = pl.program_id(1)
    nt = pl.num_programs(1)
    tl = PM_TL
    c = CHUNK
    nc = tl // c
    n_cs = CONV_CH // LANES
    n_qs = QKV_W // LANES

    @pl.when(t == 0)
    def _():
        xx[:, 0:CONV_PAD, :] = jnp.zeros((n_cs, CONV_PAD, LANES), F32)
        qx[:, 0:SHORT_PAD, :] = jnp.zeros((n_qs, SHORT_PAD, LANES), F32)
        S[...] = jnp.zeros(S.shape, F32)

    xb = x_ref[...].astype(BF16)
    proj = lambda lo, hi: jnp.dot(xb, wa_ref[:, lo:hi], preferred_element_type=F32)
    gate = lambda lo, hi: jnp.dot(xb, wg_ref[:, lo:hi], preferred_element_type=F32)
    conv_a = lambda sl: _strided_conv(xx, wdw_ref, ya, sl, sl, tl // CONV_RB, CONV_K,
                                      CONV_PAD - (CONV_K - 1), lambda y: y)

    def conv_b(part):
        for hd in range(DN_HEADS):
            _strided_conv(qx, wsh_ref, (qs, ks, vs)[part], part * DN_HEADS + hd, hd, tl // CONV_RB, SHORT_K,
                          SHORT_PAD - (SHORT_K - 1), _silu)

    def qkv_part(part):
        y = proj(OFF_QKV + part * DN_QK, OFF_QKV + (part + 1) * DN_QK)
        for hd in range(DN_HEADS):
            qx[part * DN_HEADS + hd, SHORT_PAD:SHORT_PAD + tl, :] = y[:, hd * LANES:(hd + 1) * LANES]

    glu = proj(0, OFF_QKV) + bglu_ref[...]
    u = glu[:, :CONV_CH] * _sigmoid(glu[:, CONV_CH:])
    for sl in range(n_cs):
        xx[sl, CONV_PAD:CONV_PAD + tl, :] = u[:, sl * LANES:(sl + 1) * LANES]
    qkv_part(0)
    conv_a(0)
    qkv_part(1)
    conv_a(1)
    qkv_part(2)
    conv_a(2)
    og_s[...] = proj(OFF_OG, N_A)
    conv_a(3)
    ga_s[:, :DN_V] = gate(0, DN_V)
    conv_b(0)
    ga_s[:, DN_V:] = gate(DN_V, D_MODEL)
    conv_b(1)
    gb_s[:, :DN_V] = gate(D_MODEL, D_MODEL + DN_V)
    conv_b(2)
    gb_s[:, DN_V:] = gate(D_MODEL + DN_V, N_G)
    bd = jnp.dot(xb, wbd_ref[...], preferred_element_type=F32)
    for rb in range(tl // CHUNK):
        r0 = rb * CHUNK
        cv = jnp.concatenate([ya[sl, r0:r0 + CHUNK, :] for sl in range(n_cs)], axis=1) + bdw_ref[...]
        cvn_s[r0:r0 + CHUNK, :] = _silu(_ln(cv, lcg_ref[...], lcb_ref[...])).astype(BF16)

    beta = _sigmoid(bd)
    ld = -jnp.exp(alog_ref[...]) * _softplus(bd + dtb_ref[...])
    rt = lax.broadcasted_iota(jnp.int32, (tl, tl), 0)
    ct = lax.broadcasted_iota(jnp.int32, (tl, tl), 1)
    blk_tril = jnp.where((rt >= ct) & ((rt // c) == (ct // c)), 1.0, 0.0).astype(BF16)
    gsum = _dot_exact_lhs(blk_tril, ld)
    beta3 = beta.reshape(nc, c, LANES)
    gsum3 = gsum.reshape(nc, c, LANES)
    gsum_t = gsum.T
    glast3 = gsum3[:, c - 1:c, :]
    eg3 = jnp.exp(gsum3)
    etail3 = jnp.exp(glast3 - gsum3)
    egc3 = jnp.exp(glast3)

    ri = lax.broadcasted_iota(jnp.int32, (c, c), 0)
    ci = lax.broadcasted_iota(jnp.int32, (c, c), 1)
    tril = (ri >= ci)[None]
    strict = (ri > ci)[None]
    eye = jnp.where(ri == ci, 1.0, 0.0)[None]

    lane2 = lax.broadcasted_iota(jnp.int32, (c, 2 * c), 1)
    left = (lane2 < c)[None]
    eye2 = jnp.concatenate([eye, eye], axis=2)

    def blockdiag2(m):
        return jnp.concatenate([jnp.where(left, m, 0.0), jnp.where(left, 0.0, m)], axis=1).astype(BF16)

    for h0 in range(0, DN_HEADS, 2):
        per = []
        for hd in (h0, h0 + 1):
            lb = slice(hd, hd + 1)
            lg = slice(DN_HEADS + hd, DN_HEADS + hd + 1)
            q = _l2n(qs[hd].reshape(nc, c, DN_DK)) * (DN_DK ** -0.5)
            k = _l2n(ks[hd].reshape(nc, c, DN_DK))
            v = vs[hd].reshape(nc, c, DN_DV)
            b_col = beta3[:, :, lb]
            g_row = jnp.stack([gsum_t[DN_HEADS + hd:DN_HEADS + hd + 1, ch * c:(ch + 1) * c] for ch in range(nc)])
            diff = gsum3[:, :, lg] - g_row
            decay = jnp.where(tril, jnp.exp(jnp.where(tril, diff, 0.0)), 0.0)
            kq = _bmm_nt(jnp.concatenate([k, q], axis=1), k)
            a = jnp.where(strict, b_col * kq[:, :c] * decay, 0.0)
            aqk = kq[:, c:] * decay
            rhs = jnp.concatenate([(b_col * eg3[:, :, lg]) * k, b_col * v], axis=2)
            per.append((hd, lg, q, k, a, aqk, rhs))
        x = jnp.concatenate([per[0][4], per[1][4]], axis=2)
        p = eye2 - x
        for _ in range(5):
            x = jnp.einsum('bij,bjk->bik', x.astype(BF16), blockdiag2(x), preferred_element_type=F32)
            p = p + jnp.einsum('bij,bjk->bik', p.astype(BF16), blockdiag2(x), preferred_element_type=F32)
        for n_, (hd, lg, q, k, a, aqk, rhs) in enumerate(per):
            sol = _bmm(p[:, :, n_ * c:(n_ + 1) * c], rhs)
            qo = _bmm(aqk, sol)
            q2_s[hd] = (eg3[:, :, lg] * q - qo[:, :, :DN_DK]).astype(BF16)
            o1_s[hd] = qo[:, :, DN_DK:]
            kn = _bmm_tn(k * etail3[:, :, lg], sol)
            ktw_s[hd] = kn[:, :, :DN_DK].astype(BF16)
            n_s[hd] = kn[:, :, DN_DK:]
            egc_s[hd] = jnp.broadcast_to(egc3[:, :, lg], (nc, 1, DN_DV))

    def carry_body(ch, carry):
        for hd in range(DN_HEADS):
            s_h = S[hd]
            s_b = s_h.astype(BF16)
            sl_s[hd, ch] = s_b
            S[hd] = egc_s[hd, ch] * s_h + (n_s[hd, ch] - jnp.dot(ktw_s[hd, ch], s_b,
                                                              preferred_element_type=F32))
        return carry

    for ch in range(nc):
        carry_body(ch, 0)

    for hd in range(DN_HEADS):
        ls = slice(hd * DN_DK, (hd + 1) * DN_DK)
        o = o1_s[hd] + jnp.einsum('bck,bkv->bcv', q2_s[hd], sl_s[hd], preferred_element_type=F32)
        o = o * lax.rsqrt(jnp.mean(o * o, -1, keepdims=True) + RMS_EPS) * dng_ref[...]
        on_s[:, ls] = (o.reshape(tl, DN_DV) * _silu(og_s[:, ls])).astype(BF16)

    mixed = _sigmoid(ga_s[...]) * jnp.dot(cvn_s[...], wca_ref[...], preferred_element_type=F32)
    mixed = mixed + _sigmoid(gb_s[...]) * jnp.dot(on_s[...], wdn_ref[...], preferred_element_type=F32)
    h = _ln(DEEPNORM_ALPHA * x_ref[...] + _dot(mixed, wo_ref[...]), g1_ref[...], b1_ref[...])
    h_ref[...] = h
    xq_ref[...] = _dot(h, wxq_ref[...]).astype(BF16)

    @pl.when(t == nt - 1)
    def _():
        for sl in range(n_cs):
            pconv_ref[0, :, sl * LANES:(sl + 1) * LANES] = xx[sl, tl:tl + CONV_PAD, :]
        for s12 in range(n_qs):
            pqkv_ref[0, :, s12 * LANES:(s12 + 1) * LANES] = qx[s12, tl:tl + SHORT_PAD, :]
        sout_ref[0] = S[...]

    for sl in range(n_cs):
        xx[sl, 0:CONV_PAD, :] = xx[sl, tl:tl + CONV_PAD, :]
    for s12 in range(n_qs):
        qx[s12, 0:SHORT_PAD, :] = qx[s12, tl:tl + SHORT_PAD, :]


def _pfront(x, w_a, w_g, w_bd, bglu, wdw, bdw, lcg, lcb, wsh, alog, dtb, dng, wca, wdn, wo, wxq, g1, b1, batch, seq):
    tl = PM_TL
    nt = seq // tl
    nc = tl // CHUNK
    n = batch * seq
    row = lambda w: pl.BlockSpec((tl, w), lambda b, t: (b * nt + t, 0))
    per_seq = lambda *s: pl.BlockSpec((1,) + s, lambda b, t: (b,) + (0,) * len(s))
    return pl.pallas_call(
        _pfront_kernel,
        grid=(batch, nt),
        in_specs=[row(D_MODEL), _wspec((D_MODEL, N_A)), _wspec((D_MODEL, N_G)), _wspec((D_MODEL, LANES)),
                  _wspec((1, 2 * CONV_CH)),
                  _wspec((CONV_CH // LANES, CONV_K, 8, LANES)), _wspec((1, CONV_CH)), _wspec((1, CONV_CH)),
                  _wspec((1, CONV_CH)), _wspec((QKV_W // LANES, SHORT_K, 8, LANES)), _wspec((1, LANES)),
                  _wspec((1, LANES)), _wspec((1, DN_DV)),
                  _wspec((CONV_CH, D_MODEL)), _wspec((DN_V, D_MODEL)), _wspec((D_MODEL, D_MODEL)),
                  _wspec((D_MODEL, D_MODEL)), _wspec((1, D_MODEL)), _wspec((1, D_MODEL))],
        out_specs=[row(D_MODEL), row(D_MODEL), per_seq(CONV_PAD, CONV_CH), per_seq(SHORT_PAD, QKV_W),
                   per_seq(DN_HEADS, DN_DK, DN_DV)],
        out_shape=[jax.ShapeDtypeStruct((n, D_MODEL), F32), jax.ShapeDtypeStruct((n, D_MODEL), BF16),
                   jax.ShapeDtypeStruct((batch, CONV_PAD, CONV_CH), F32),
                   jax.ShapeDtypeStruct((batch, SHORT_PAD, QKV_W), F32),
                   jax.ShapeDtypeStruct((batch, DN_HEADS, DN_DK, DN_DV), F32)],
        scratch_shapes=[pltpu.VMEM((CONV_CH // LANES, tl + CONV_PAD, LANES), F32),
                        pltpu.VMEM((QKV_W // LANES, tl + SHORT_PAD, LANES), F32),
                        pltpu.VMEM((CONV_CH // LANES, tl, LANES), F32),
                        pltpu.VMEM((DN_HEADS, tl, DN_DK), F32), pltpu.VMEM((DN_HEADS, tl, DN_DK), F32),
                        pltpu.VMEM((DN_HEADS, tl, DN_DV), F32),
                        pltpu.VMEM((DN_HEADS, DN_DK, DN_DV), F32),
                        pltpu.VMEM((DN_HEADS, nc, DN_DK, DN_DK), BF16),
                        pltpu.VMEM((DN_HEADS, nc, DN_DK, DN_DV), F32),
                        pltpu.VMEM((DN_HEADS, nc, CHUNK, DN_DK), BF16),
                        pltpu.VMEM((DN_HEADS, nc, CHUNK, DN_DV), F32),
                        pltpu.VMEM((DN_HEADS, nc, DN_DK, DN_DV), BF16),
                        pltpu.VMEM((DN_HEADS, nc, 1, DN_DV), F32),
                        pltpu.VMEM((tl, CONV_CH), BF16), pltpu.VMEM((tl, DN_V), BF16),
                        pltpu.VMEM((tl, DN_V), F32), pltpu.VMEM((tl, D_MODEL), F32),
                        pltpu.VMEM((tl, D_MODEL), F32)],
        compiler_params=_params(("parallel", "arbitrary")),
        name="prompt_front",
    )(x, w_a, w_g, w_bd, bglu, wdw, bdw, lcg, lcb, wsh, alog, dtb, dng, wca, wdn, wo, wxq, g1, b1)


SM_BT = 32
SM_BS = 8


def _smid1_kernel(u_ref, qkv_ref, bd_ref, og_ref, hc_ref, hq_ref, wdw_ref, bdw_ref, lcg_ref, lcb_ref, wsh_ref,
                  alog_ref, dtb_ref,
                  cvn_ref, cnew_ref, qnew_ref, w_ref, q2_ref, uv_ref, o1_ref, kt_ref, ogs_ref, egc_ref):
    nl = u_ref.shape[0]
    nh = CONV_K - 1
    xx = [hc_ref[i] for i in range(nh)] + [u_ref[i] for i in range(nl)]
    for t in range(nl):
        acc = wdw_ref[0:1, :] * xx[t]
        for j in range(1, CONV_K):
            acc = acc + wdw_ref[j:j + 1, :] * xx[t + j]
        cvn_ref[t] = _silu(_ln(acc + bdw_ref[...], lcg_ref[...], lcb_ref[...]))
    for i in range(nh):
        cnew_ref[i] = xx[nl + i]

    ns = SHORT_K - 1
    qq = [hq_ref[i] for i in range(ns)] + [qkv_ref[i] for i in range(nl)]
    for i in range(ns):
        qnew_ref[i] = qq[nl + i]
    qkv = []
    for t in range(nl):
        acc = wsh_ref[0:1, :] * qq[t]
        for j in range(1, SHORT_K):
            acc = acc + wsh_ref[j:j + 1, :] * qq[t + j]
        qkv.append(_silu(acc))

    beta = [_sigmoid(bd_ref[t]) for t in range(nl)]
    ld = [-jnp.exp(alog_ref[...]) * _softplus(bd_ref[t] + dtb_ref[...]) for t in range(nl)]

    out_refs = (w_ref, uv_ref, o1_ref, q2_ref, kt_ref)
    rows = [[[None] * DN_HEADS for _ in range(nl)] for _ in out_refs]
    for hd in range(DN_HEADS):
        q = [_l2n(qkv[t][:, hd * DN_DK:(hd + 1) * DN_DK]) * (DN_DK ** -0.5) for t in range(nl)]
        k = [_l2n(qkv[t][:, DN_QK + hd * DN_DK:DN_QK + (hd + 1) * DN_DK]) for t in range(nl)]
        v = [qkv[t][:, 2 * DN_QK + hd * DN_DV:2 * DN_QK + (hd + 1) * DN_DV] for t in range(nl)]
        b = [beta[t][:, hd:hd + 1] for t in range(nl)]
        g = [ld[t][:, DN_HEADS + hd:DN_HEADS + hd + 1] for t in range(nl)]
        gc = [g[0]]
        for t in range(1, nl):
            gc.append(gc[t - 1] + g[t])
        w_l, uv_l = [], []
        for i in range(nl):
            wi = (b[i] * jnp.exp(gc[i])) * k[i]
            ui = b[i] * v[i]
            for j in range(i):
                a_ij = b[i] * jnp.sum(k[i] * k[j], -1, keepdims=True) * jnp.exp(gc[i] - gc[j])
                wi = wi - a_ij * w_l[j]
                ui = ui - a_ij * uv_l[j]
            w_l.append(wi)
            uv_l.append(ui)
        for i in range(nl):
            o1 = jnp.zeros_like(v[i])
            q2 = jnp.exp(gc[i]) * q[i]
            for j in range(i + 1):
                aqk = jnp.sum(q[i] * k[j], -1, keepdims=True) * jnp.exp(gc[i] - gc[j])
                o1 = o1 + aqk * uv_l[j]
                q2 = q2 - aqk * w_l[j]
            for r, val in zip(rows, (w_l[i], uv_l[i], o1, q2, k[i] * jnp.exp(gc[nl - 1] - gc[i]))):
                r[i][hd] = val
        egc_ref[hd] = jnp.broadcast_to(jnp.exp(gc[nl - 1]), (u_ref.shape[1], LANES))
    for i in range(nl):
        for ref, r in zip(out_refs, rows):
            ref[:, i, :] = jnp.concatenate(r[i], axis=1)
        ogs_ref[:, i, :] = og_ref[i]


def _smid1(u, qkv, bd, og, hc, hq, wdw, bdw, lcg, lcb, wsh, alog, dtb):
    nl, b, _ = u.shape
    bt = SM_BT
    tm = lambda r, w: pl.BlockSpec((r, bt, w), lambda i: (0, i, 0))
    sm = pl.BlockSpec((bt, nl, DN_V), lambda i: (i, 0, 0))
    sm_shape = jax.ShapeDtypeStruct((b, nl, DN_V), F32)
    tm_outs = [(nl, CONV_CH), (CONV_K - 1, CONV_CH), (SHORT_K - 1, QKV_W)]
    return pl.pallas_call(
        _smid1_kernel,
        grid=(b // bt,),
        in_specs=[tm(nl, CONV_CH), tm(nl, QKV_W), tm(nl, LANES), tm(nl, DN_V), tm(CONV_K - 1, CONV_CH),
                  tm(SHORT_K - 1, QKV_W),
                  _wspec((CONV_PAD, CONV_CH)), _wspec((1, CONV_CH)), _wspec((1, CONV_CH)),
                  _wspec((1, CONV_CH)), _wspec((SHORT_K, QKV_W)), _wspec((1, LANES)),
                  _wspec((1, LANES))],
        out_specs=[tm(r, w) for r, w in tm_outs] + [sm] * 6 + [tm(DN_HEADS, LANES)],
        out_shape=[jax.ShapeDtypeStruct((r, b, w), F32) for r, w in tm_outs] + [sm_shape] * 6
        + [jax.ShapeDtypeStruct((DN_HEADS, b, LANES), F32)],
        compiler_params=_params(("parallel",)),
        name="sample_mid1",
    )(u, qkv, bd, og, hc, hq, wdw, bdw, lcg, lcb, wsh, alog, dtb)


def _smid2_kernel(w_ref, q2_ref, uv_ref, o1_ref, kt_ref, og_ref, egc_ref, s_ref, dng_ref,
                  on_ref, snew_ref):
    nl = w_ref.shape[1]
    for hd in range(DN_HEADS):
        ls = slice(hd * DN_DK, (hd + 1) * DN_DK)
        s_h = s_ref[:, hd]
        lhs = jnp.concatenate([w_ref[:, :, ls], q2_ref[:, :, ls]], axis=1)
        r = jnp.einsum('bck,bkv->bcv', lhs.astype(BF16), s_h.astype(BF16),
                       preferred_element_type=F32)
        u_new = uv_ref[:, :, ls] - r[:, :nl]
        o = o1_ref[:, :, ls] + r[:, nl:]
        upd = jnp.einsum('bck,bcv->bkv', kt_ref[:, :, ls].astype(BF16), u_new.astype(BF16),
                         preferred_element_type=F32)
        snew_ref[:, hd] = egc_ref[:, hd] * s_h + upd
        o = o * lax.rsqrt(jnp.mean(o * o, -1, keepdims=True) + RMS_EPS) * dng_ref[...]
        on_ref[:, :, ls] = o * _silu(og_ref[:, :, ls])


def _smid2(w, q2, uv, o1, kt, og, egc, s, dng):
    b, nl, _ = w.shape
    bs = SM_BS
    sq = pl.BlockSpec((bs, nl, DN_V), lambda i: (i, 0, 0))
    st = pl.BlockSpec((bs, DN_HEADS, DN_DK, DN_DV), lambda i: (i, 0, 0, 0))
    return pl.pallas_call(
        _smid2_kernel,
        grid=(b // bs,),
        in_specs=[sq, sq, sq, sq, sq, sq,
                  pl.BlockSpec((bs, DN_HEADS, 1, LANES), lambda i: (i, 0, 0, 0)), st,
                  _wspec((1, DN_DV))],
        out_specs=[sq, st],
        out_shape=[jax.ShapeDtypeStruct((b, nl, DN_V), F32),
                   jax.ShapeDtypeStruct((b, DN_HEADS, DN_DK, DN_DV), F32)],
        compiler_params=_params(("parallel",)),
        name="sample_mid2",
    )(w, q2, uv, o1, kt, og, egc, s, dng)


def _tm(x):
    return jnp.swapaxes(x, 0, 1)


def kernel(x_prompt, x_sample, mem_prompt, state_conv, state_qkv_conv, state_delta, cache_mem_k, cache_mem_v, w_in, b_glu, w_dw, b_dw, ln_conv_g, ln_conv_b, w_conv_out, w_short, a_log, dt_bias, dn_norm_g, w_dn_out, w_o, ln1_g, ln1_b, w_xq, w_mem_kv, w_xo, ln2_g, ln2_b, w_ff1, w_ff2, ln3_g, ln3_b):
    assert w_in.shape[0] == DEPTH == 1
    bp, lp, _ = x_prompt.shape
    bsm, ls_, _ = x_sample.shape

    w = w_in[0]
    o_ga = N_A + 2 * DN_HEADS
    w_a = w[:, :N_A].astype(BF16)
    w_g = w[:, o_ga:o_ga + N_G].astype(BF16)
    w_bd = jnp.concatenate([w[:, N_A:o_ga], jnp.zeros((D_MODEL, LANES - 2 * DN_HEADS), w.dtype)],
                           axis=1).astype(BF16)
    lane_pad = lambda a: jnp.concatenate(
        [jnp.zeros((DN_HEADS,), F32), a.astype(F32), jnp.zeros((LANES - 2 * DN_HEADS,), F32)])[None]
    alog = lane_pad(a_log[0])
    dtb = lane_pad(dt_bias[0])
    wdw = jnp.concatenate([w_dw[0], jnp.zeros((CONV_PAD - CONV_K, CONV_CH), F32)], axis=0)
    r2 = lambda a: a[0][None]
    bglu, bdw, lcg, lcb = r2(b_glu), r2(b_dw), r2(ln_conv_g), r2(ln_conv_b)
    dng = r2(dn_norm_g)
    wsh = w_short[0]
    slabs = lambda a: jnp.broadcast_to(
        jnp.swapaxes(a.reshape(a.shape[0], a.shape[1] // LANES, LANES), 0, 1)[:, :, None, :],
        (a.shape[1] // LANES, a.shape[0], 8, LANES))
    wdw_slab = slabs(w_dw[0])
    wsh_slab = slabs(wsh)
    wca, wdn, wo, wxq, wxo = (a[0].astype(BF16) for a in (w_conv_out, w_dn_out, w_o, w_xq, w_xo))
    w1, w2, wkv = w_ff1[0].astype(BF16), w_ff2[0].astype(BF16), w_mem_kv[0].astype(BF16)
    g1, b1, g2, b2, g3, b3 = (r2(a) for a in (ln1_g, ln1_b, ln2_g, ln2_b, ln3_g, ln3_b))

    n_s = bsm * ls_
    xs = _tm(x_sample).reshape(n_s, D_MODEL)
    u, qkv, og, ga, gb, bd = _inproj(xs, w_a, w_g, w_bd, bglu, 256)
    t3 = lambda a: a.reshape(ls_, bsm, a.shape[-1])
    cvn, c_new, q_new, w_, q2, uv, o1, kt, og_s, egc = _smid1(
        t3(u), t3(qkv), t3(bd), t3(og), _tm(state_conv[0]), _tm(state_qkv_conv[0]),
        wdw, bdw, lcg, lcb, wsh, alog, dtb)
    on, s_delta = _smid2(w_, q2, uv, o1, kt, og_s, _tm(egc)[:, :, None, :], state_delta[0], dng)
    h_s, xq_s = _merge(cvn.reshape(n_s, CONV_CH), _tm(on).reshape(n_s, DN_V), ga, gb, xs,
                       wca, wdn, wo, wxq, g1, b1, 512)

    n_p = bp * lp
    xp = x_prompt.reshape(n_p, D_MODEL)
    mk, mv, mk_b, mv_b = _memkv(mem_prompt.reshape(bp * N_MEM, D_MODEL), wkv, 512)
    h, xq, p_conv, p_qkv, p_delta = _pfront(xp, w_a, w_g, w_bd, bglu, wdw_slab, bdw, lcg, lcb, wsh_slab,
                                            alog, dtb, dng, wca, wdn, wo, wxq, g1, b1, bp, lp)
    y_p, att_s = _ptail(xq, mk_b.reshape(bp, N_MEM, D_MODEL), mv_b.reshape(bp, N_MEM, D_MODEL), h,
                        wxo, w1, w2, g2, b2, g3, b3, _tm(t3(xq_s)), cache_mem_k, cache_mem_v, bp, lp, 512)
    y_p = y_p.reshape(bp, lp, D_MODEL)
    p_conv = p_conv[:, CONV_PAD - (CONV_K - 1):]
    p_qkv = p_qkv[:, SHORT_PAD - (SHORT_K - 1):]
    p_mk = mk.reshape(bp, N_MEM, X_HEADS, X_HEAD_DIM)
    p_mv = mv.reshape(bp, N_MEM, X_HEADS, X_HEAD_DIM)

    y_s = _tail(_tm(att_s).reshape(n_s, D_MODEL), h_s, wxo, w1, w2, g2, b2, g3, b3, 512)
    y_s = _tm(t3(y_s))

    return (y_p, y_s, p_conv[None], p_qkv[None], p_delta[None], p_mk[None], p_mv[None],
            _tm(c_new)[None], _tm(q_new)[None], s_delta[None])
```

```python
import functools

import jax
import jax.numpy as jnp
from jax import lax
from jax.experimental import pallas as pl
from jax.experimental.pallas import tpu as pltpu

F32 = jnp.float32
BF16 = jnp.bfloat16

D_MODEL = 1024
N_MEM = 256
CONV_CH = 512
CONV_K = 31
DN_HEADS = 4
DN_DK = 128
DN_DV = 128
DN_QK = DN_HEADS * DN_DK
DN_V = DN_HEADS * DN_DV
QKV_W = 2 * DN_QK + DN_V
SHORT_K = 4
CHUNK = 64
X_HEADS = 4
X_HEAD_DIM = D_MODEL // X_HEADS
D_FF = 4 * D_MODEL
DEPTH = 1
DEEPNORM_ALPHA = (2 * DEPTH) ** 0.25
LN_EPS = 1e-5
RMS_EPS = 1e-6
L2_EPS = 1e-6

LANES = 128
OFF_QKV = 2 * CONV_CH
OFF_OG = OFF_QKV + QKV_W
N_A = OFF_OG + DN_V
N_G = 2 * D_MODEL
VMEM_LIMIT = 56 * 1024 * 1024


def _dot(a, b):
    return jnp.dot(a.astype(BF16), b.astype(BF16), preferred_element_type=F32)


def _dot_nt(a, b):
    return lax.dot_general(a.astype(BF16), b.astype(BF16), (((1,), (1,)), ((), ())),
                           preferred_element_type=F32)


def _sigmoid(x):
    return 1.0 / (1.0 + jnp.exp(-x))


def _silu(x):
    return x * _sigmoid(x)


def _softplus(x):
    return jnp.maximum(x, 0.0) + jnp.log(1.0 + jnp.exp(-jnp.abs(x)))


def _ln(x, g, b):
    mu = jnp.mean(x, -1, keepdims=True)
    xc = x - mu
    var = jnp.mean(xc * xc, -1, keepdims=True)
    return xc * lax.rsqrt(var + LN_EPS) * g + b


def _l2n(x):
    return x * lax.rsqrt(jnp.sum(x * x, -1, keepdims=True) + L2_EPS)


def _wspec(shape):
    return pl.BlockSpec(shape, lambda *_: (0,) * len(shape), pipeline_mode=pl.Buffered(1))


def _params(sem):
    return pltpu.CompilerParams(dimension_semantics=sem, vmem_limit_bytes=VMEM_LIMIT)


def _inproj_kernel(x_ref, wa_ref, wg_ref, wbd_ref, bglu_ref, u_ref, qkv_ref, og_ref, ga_ref, gb_ref, bd_ref):
    x = x_ref[...].astype(BF16)
    glu = jnp.dot(x, wa_ref[:, 0:OFF_QKV], preferred_element_type=F32) + bglu_ref[...]
    u_ref[...] = glu[:, :CONV_CH] * _sigmoid(glu[:, CONV_CH:])
    qkv_ref[...] = jnp.dot(x, wa_ref[:, OFF_QKV:OFF_OG], preferred_element_type=F32)
    og_ref[...] = jnp.dot(x, wa_ref[:, OFF_OG:N_A], preferred_element_type=F32)
    ga_ref[...] = jnp.dot(x, wg_ref[:, :D_MODEL], preferred_element_type=F32)
    gb_ref[...] = jnp.dot(x, wg_ref[:, D_MODEL:], preferred_element_type=F32)
    bd_ref[...] = jnp.dot(x, wbd_ref[...], preferred_element_type=F32)


def _inproj(x, w_a, w_g, w_bd, b_glu, tm):
    n = x.shape[0]
    widths = (CONV_CH, QKV_W, DN_V, D_MODEL, D_MODEL, LANES)
    row = lambda w: pl.BlockSpec((tm, w), lambda i: (i, 0))
    return pl.pallas_call(
        _inproj_kernel,
        grid=(n // tm,),
        in_specs=[row(D_MODEL), _wspec((D_MODEL, N_A)), _wspec((D_MODEL, N_G)), _wspec((D_MODEL, LANES)),
                  _wspec((1, 2 * CONV_CH))],
        out_specs=[row(w) for w in widths],
        out_shape=[jax.ShapeDtypeStruct((n, w), F32) for w in widths],
        compiler_params=_params(("parallel",)),
        name="inproj",
    )(x, w_a, w_g, w_bd, b_glu)


def _merge_kernel(cvn_ref, on_ref, ga_ref, gb_ref, x_ref, wca_ref, wdn_ref, wo_ref, wxq_ref,
                  g1_ref, b1_ref, h_ref, xq_ref):
    br_a = _dot(cvn_ref[...], wca_ref[...])
    br_b = _dot(on_ref[...], wdn_ref[...])
    mixed = _sigmoid(ga_ref[...]) * br_a + _sigmoid(gb_ref[...]) * br_b
    h = _ln(DEEPNORM_ALPHA * x_ref[...] + _dot(mixed, wo_ref[...]), g1_ref[...], b1_ref[...])
    h_ref[...] = h
    xq_ref[...] = _dot(h, wxq_ref[...])


def _merge(cvn, on, ga, gb, x, wca, wdn, wo, wxq, g1, b1, tm):
    n = x.shape[0]
    row = lambda w: pl.BlockSpec((tm, w), lambda i: (i, 0))
    return pl.pallas_call(
        _merge_kernel,
        grid=(n // tm,),
        in_specs=[row(CONV_CH), row(DN_V), row(D_MODEL), row(D_MODEL), row(D_MODEL),
                  _wspec((CONV_CH, D_MODEL)), _wspec((DN_V, D_MODEL)), _wspec((D_MODEL, D_MODEL)),
                  _wspec((D_MODEL, D_MODEL)), _wspec((1, D_MODEL)), _wspec((1, D_MODEL))],
        out_specs=[row(D_MODEL), row(D_MODEL)],
        out_shape=[jax.ShapeDtypeStruct((n, D_MODEL), F32)] * 2,
        compiler_params=_params(("parallel",)),
        name="merge",
    )(cvn, on, ga, gb, x, wca, wdn, wo, wxq, g1, b1)


FF_BLOCK = 1024


def _tail_kernel(att_ref, h_ref, wxo_ref, w1_ref, w2_ref, g2_ref, b2_ref, g3_ref, b3_ref, y_ref):
    xo = _dot(att_ref[...], wxo_ref[...])
    h2 = _ln(DEEPNORM_ALPHA * h_ref[...] + xo, g2_ref[...], b2_ref[...])
    h2b = h2.astype(BF16)
    ff = jnp.zeros(h2.shape, F32)
    for c in range(D_FF // FF_BLOCK):
        a = jnp.dot(h2b, w1_ref[:, c * FF_BLOCK:(c + 1) * FF_BLOCK], preferred_element_type=F32)
        a = jnp.square(jnp.maximum(a, 0.0))
        ff = ff + jnp.dot(a.astype(BF16), w2_ref[c * FF_BLOCK:(c + 1) * FF_BLOCK, :],
                          preferred_element_type=F32)
    y_ref[...] = _ln(DEEPNORM_ALPHA * h2 + ff, g3_ref[...], b3_ref[...])


def _tail(att, h, wxo, w1, w2, g2, b2, g3, b3, tm):
    n = h.shape[0]
    row = lambda w: pl.BlockSpec((tm, w), lambda i: (i, 0))
    return pl.pallas_call(
        _tail_kernel,
        grid=(n // tm,),
        in_specs=[row(D_MODEL), row(D_MODEL), _wspec((D_MODEL, D_MODEL)), _wspec((D_MODEL, D_FF)),
                  _wspec((D_FF, D_MODEL)), _wspec((1, D_MODEL)), _wspec((1, D_MODEL)),
                  _wspec((1, D_MODEL)), _wspec((1, D_MODEL))],
        out_specs=row(D_MODEL),
        out_shape=jax.ShapeDtypeStruct((n, D_MODEL), F32),
        compiler_params=_params(("parallel",)),
        name="tail",
    )(att, h, wxo, w1, w2, g2, b2, g3, b3)


def _memkv_kernel(m_ref, w_ref, k_ref, v_ref, kb_ref, vb_ref):
    m = m_ref[...].astype(BF16)
    tm = m.shape[0]
    k = jnp.dot(m, w_ref[:, :D_MODEL], preferred_element_type=F32)
    v = jnp.dot(m, w_ref[:, D_MODEL:], preferred_element_type=F32)
    k_ref[...] = k.reshape(tm, X_HEADS, X_HEAD_DIM)
    v_ref[...] = v.reshape(tm, X_HEADS, X_HEAD_DIM)
    kb_ref[...] = k.astype(BF16)
    vb_ref[...] = v.astype(BF16)


def _memkv(mem, w, tm):
    n = mem.shape[0]
    row = pl.BlockSpec((tm, D_MODEL), lambda i: (i, 0))
    row4 = pl.BlockSpec((tm, X_HEADS, X_HEAD_DIM), lambda i: (i, 0, 0))
    return pl.pallas_call(
        _memkv_kernel,
        grid=(n // tm,),
        in_specs=[row, _wspec((D_MODEL, 2 * D_MODEL))],
        out_specs=[row4, row4, row, row],
        out_shape=[jax.ShapeDtypeStruct((n, X_HEADS, X_HEAD_DIM), F32)] * 2
        + [jax.ShapeDtypeStruct((n, D_MODEL), BF16)] * 2,
        compiler_params=_params(("parallel",)),
        name="memkv",
    )(mem, w)


def _softmax(sc):
    sc = sc - jnp.max(sc, -1, keepdims=True)
    e = jnp.exp(sc)
    return e / jnp.sum(e, -1, keepdims=True)


def _ptail_kernel(q_ref, k_ref, v_ref, h_ref, wxo_ref, w1_ref, w2_ref, g2_ref, b2_ref, g3_ref, b3_ref,
                  sq_ref, sk_ref, sv_ref, y_ref, so_ref, att_s, *, nseq):
    scale = X_HEAD_DIM ** -0.5
    for hd in range(X_HEADS):
        sl = slice(hd * X_HEAD_DIM, (hd + 1) * X_HEAD_DIM)
        pr = _softmax(_dot_nt(q_ref[:, sl], k_ref[0, :, sl]) * scale)
        att_s[:, sl] = _dot(pr, v_ref[0, :, sl]).astype(BF16)
    _attn_split_kernel(sq_ref, sk_ref, sv_ref, so_ref, nseq=nseq)
    _tail_kernel(att_s, h_ref, wxo_ref, w1_ref, w2_ref, g2_ref, b2_ref, g3_ref, b3_ref, y_ref)


def _ptail(xq, kb, vb, h, wxo, w1, w2, g2, b2, g3, b3, sq, sk, sv, batch, seq, tq):
    nt = seq // tq
    sb, sl_, _ = sq.shape
    nseq = sb // (batch * nt)
    row = lambda: pl.BlockSpec((tq, D_MODEL), lambda b, t: (b * nt + t, 0))
    mem = pl.BlockSpec((1, N_MEM, D_MODEL), lambda b, t: (b, 0, 0))
    s_q = pl.BlockSpec((nseq, sl_, D_MODEL), lambda b, t: (b * nt + t, 0, 0))
    s_kv = pl.BlockSpec((None, nseq, N_MEM, X_HEADS, X_HEAD_DIM), lambda b, t: (0, b * nt + t, 0, 0, 0))
    return pl.pallas_call(
        functools.partial(_ptail_kernel, nseq=nseq),
        grid=(batch, nt),
        in_specs=[row(), mem, mem, row(), _wspec((D_MODEL, D_MODEL)), _wspec((D_MODEL, D_FF)),
                  _wspec((D_FF, D_MODEL)), _wspec((1, D_MODEL)), _wspec((1, D_MODEL)),
                  _wspec((1, D_MODEL)), _wspec((1, D_MODEL)), s_q, s_kv, s_kv],
        out_specs=[row(), s_q],
        out_shape=[jax.ShapeDtypeStruct((batch * seq, D_MODEL), F32),
                   jax.ShapeDtypeStruct((sb, sl_, D_MODEL), F32)],
        scratch_shapes=[pltpu.VMEM((tq, D_MODEL), BF16)],
        compiler_params=_params(("parallel", "parallel")),
        name="prompt_tail",
    )(xq, kb, vb, h, wxo, w1, w2, g2, b2, g3, b3, sq, sk, sv)


def _attn_split_kernel(q_ref, k_ref, v_ref, o_ref, *, nseq):
    scale = X_HEAD_DIM ** -0.5
    nl = q_ref.shape[1]
    nk = N_MEM * X_HEADS
    row_head = lax.broadcasted_iota(jnp.int32, (X_HEADS * nl, nk), 0) // nl
    col_head = lax.broadcasted_iota(jnp.int32, (X_HEADS * nl, nk), 1) % X_HEADS
    own = row_head == col_head
    for s in range(nseq):
        k2 = k_ref[s].reshape(nk, X_HEAD_DIM)
        v2 = v_ref[s].reshape(nk, X_HEAD_DIM)
        q = q_ref[s]
        q4 = jnp.concatenate([q[:, h * X_HEAD_DIM:(h + 1) * X_HEAD_DIM] for h in range(X_HEADS)], axis=0)
        pr = _softmax(jnp.where(own, _dot_nt(q4, k2) * scale, -1e30))
        o4 = _dot(pr, v2)
        for h in range(X_HEADS):
            o_ref[s, :, h * X_HEAD_DIM:(h + 1) * X_HEAD_DIM] = o4[h * nl:(h + 1) * nl]


PM_TL = 512
ROW_STRIDE = 4
CONV_RB = 8 * ROW_STRIDE
CONV_PAD = 32
SHORT_PAD = 8


def _split3(x):
    h = x.astype(BF16)
    r = x - h.astype(F32)
    m = r.astype(BF16)
    l = (r - m.astype(F32)).astype(BF16)
    return h, m, l


def _dot_exact_lhs(lhs_bf16, x):
    h, m, l = _split3(x)
    d = lambda p: jnp.dot(lhs_bf16, p, preferred_element_type=F32)
    return d(h) + d(m) + d(l)


def _bmm(a, b):
    return jnp.einsum('bij,bjk->bik', a.astype(BF16), b.astype(BF16), preferred_element_type=F32)


def _bmm_nt(a, b):
    return jnp.einsum('bik,bjk->bij', a.astype(BF16), b.astype(BF16), preferred_element_type=F32)


def _bmm_tn(a, b):
    return jnp.einsum('bki,bkj->bij', a.astype(BF16), b.astype(BF16), preferred_element_type=F32)


def _strided_conv(src_ref, w_ref, dst_ref, slab, dslab, nblk, ntaps, off, post):
    ws = [w_ref[slab, j] for j in range(ntaps)]

    def body(rb, carry):
        r0 = pl.multiple_of(rb * CONV_RB, CONV_RB)
        accs = [None] * ROW_STRIDE
        for q in range(off, off + ntaps + ROW_STRIDE - 1):
            x = src_ref[slab, pl.ds(r0 + q, 8, stride=ROW_STRIDE), :]
            for m in range(ROW_STRIDE):
                j = q - off - m
                if 0 <= j < ntaps:
                    term = ws[j] * x
                    accs[m] = term if accs[m] is None else accs[m] + term
        for m in range(ROW_STRIDE):
            dst_ref[dslab, pl.ds(r0 + m, 8, stride=ROW_STRIDE), :] = post(accs[m])
        return carry

    lax.fori_loop(0, nblk, body, 0, unroll=True)


def _pfront_kernel(x_ref, wa_ref, wg_ref, wbd_ref, bglu_ref, wdw_ref, bdw_ref, lcg_ref, lcb_ref, wsh_ref, alog_ref, dtb_ref,
                   dng_ref, wca_ref, wdn_ref, wo_ref, wxq_ref, g1_ref, b1_ref,
                   h_ref, xq_ref, pconv_ref, pqkv_ref, sout_ref,
                   xx, qx, ya, qs, ks, vs, S, ktw_s, n_s, q2_s, o1_s, sl_s, egc_s, cvn_s, on_s, og_s, ga_s, gb_s):
    t = pl.program_id(1)
    nt = pl.num_programs(1)
    tl = PM_TL
    c = CHUNK
    nc = tl // c
    n_cs = CONV_CH // LANES
    n_qs = QKV_W // LANES

    @pl.when(t == 0)
    def _():
        xx[:, 0:CONV_PAD, :] = jnp.zeros((n_cs, CONV_PAD, LANES), F32)
        qx[:, 0:SHORT_PAD, :] = jnp.zeros((n_qs, SHORT_PAD, LANES), F32)
        S[...] = jnp.zeros(S.shape, F32)

    xb = x_ref[...].astype(BF16)
    proj = lambda lo, hi: jnp.dot(xb, wa_ref[:, lo:hi], preferred_element_type=F32)
    gate = lambda lo, hi: jnp.dot(xb, wg_ref[:, lo:hi], preferred_element_type=F32)
    conv_a = lambda sl: _strided_conv(xx, wdw_ref, ya, sl, sl, tl // CONV_RB, CONV_K,
                                      CONV_PAD - (CONV_K - 1), lambda y: y)

    def conv_b(part):
        for hd in range(DN_HEADS):
            _strided_conv(qx, wsh_ref, (qs, ks, vs)[part], part * DN_HEADS + hd, hd, tl // CONV_RB, SHORT_K,
                          SHORT_PAD - (SHORT_K - 1), _silu)

    def qkv_part(part):
        y = proj(OFF_QKV + part * DN_QK, OFF_QKV + (part + 1) * DN_QK)
        for hd in range(DN_HEADS):
            qx[part * DN_HEADS + hd, SHORT_PAD:SHORT_PAD + tl, :] = y[:, hd * LANES:(hd + 1) * LANES]

    glu = proj(0, OFF_QKV) + bglu_ref[...]
    u = glu[:, :CONV_CH] * _sigmoid(glu[:, CONV_CH:])
    for sl in range(n_cs):
        xx[sl, CONV_PAD:CONV_PAD + tl, :] = u[:, sl * LANES:(sl + 1) * LANES]
    qkv_part(0)
    conv_a(0)
    qkv_part(1)
    conv_a(1)
    qkv_part(2)
    conv_a(2)
    og_s[...] = proj(OFF_OG, N_A)
    conv_a(3)
    ga_s[:, :DN_V] = gate(0, DN_V)
    conv_b(0)
    ga_s[:, DN_V:] = gate(DN_V, D_MODEL)
    conv_b(1)
    gb_s[:, :DN_V] = gate(D_MODEL, D_MODEL + DN_V)
    conv_b(2)
    gb_s[:, DN_V:] = gate(D_MODEL + DN_V, N_G)
    bd = jnp.dot(xb, wbd_ref[...], preferred_element_type=F32)
    for rb in range(tl // CHUNK):
        r0 = rb * CHUNK
        cv = jnp.concatenate([ya[sl, r0:r0 + CHUNK, :] for sl in range(n_cs)], axis=1) + bdw_ref[...]
        cvn_s[r0:r0 + CHUNK, :] = _silu(_ln(cv, lcg_ref[...], lcb_ref[...])).astype(BF16)

    beta = _sigmoid(bd)
    ld = -jnp.exp(alog_ref[...]) * _softplus(bd + dtb_ref[...])
    rt = lax.broadcasted_iota(jnp.int32, (tl, tl), 0)
    ct = lax.broadcasted_iota(jnp.int32, (tl, tl), 1)
    blk_tril = jnp.where((rt >= ct) & ((rt // c) == (ct // c)), 1.0, 0.0).astype(BF16)
    gsum = _dot_exact_lhs(blk_tril, ld)
    beta3 = beta.reshape(nc, c, LANES)
    gsum3 = gsum.reshape(nc, c, LANES)
    gsum_t = gsum.T
    glast3 = gsum3[:, c - 1:c, :]
    eg3 = jnp.exp(gsum3)
    etail3 = jnp.exp(glast3 - gsum3)
    egc3 = jnp.exp(glast3)

    ri = lax.broadcasted_iota(jnp.int32, (c, c), 0)
    ci = lax.broadcasted_iota(jnp.int32, (c, c), 1)
    tril = (ri >= ci)[None]
    strict = (ri > ci)[None]
    eye = jnp.where(ri == ci, 1.0, 0.0)[None]

    lane2 = lax.broadcasted_iota(jnp.int32, (c, 2 * c), 1)
    left = (lane2 < c)[None]
    eye2 = jnp.concatenate([eye, eye], axis=2)

    def blockdiag2(m):
        return jnp.concatenate([jnp.where(left, m, 0.0), jnp.where(left, 0.0, m)], axis=1).astype(BF16)

    pairs = []
    for h0 in range(0, DN_HEADS, 2):
        per = []
        for hd in (h0, h0 + 1):
            lb = slice(hd, hd + 1)
            lg = slice(DN_HEADS + hd, DN_HEADS + hd + 1)
            q = _l2n(qs[hd].reshape(nc, c, DN_DK)) * (DN_DK ** -0.5)
            k = _l2n(ks[hd].reshape(nc, c, DN_DK))
            v = vs[hd].reshape(nc, c, DN_DV)
            b_col = beta3[:, :, lb]
            g_row = jnp.stack([gsum_t[DN_HEADS + hd:DN_HEADS + hd + 1, ch * c:(ch + 1) * c] for ch in range(nc)])
            diff = gsum3[:, :, lg] - g_row
            decay = jnp.where(tril, jnp.exp(jnp.where(tril, diff, 0.0)), 0.0)
            kq = _bmm_nt(jnp.concatenate([k, q], axis=1), k)
            a = jnp.where(strict, b_col * kq[:, :c] * decay, 0.0)
            aqk = kq[:, c:] * decay
            rhs = jnp.concatenate([(b_col * eg3[:, :, lg]) * k, b_col * v], axis=2)
            per.append((hd, lg, q, k, a, aqk, rhs))
        pairs.append(per)
    xs_ = [jnp.concatenate([per[0][4], per[1][4]], axis=2) for per in pairs]
    ps_ = [eye2 - x for x in xs_]
    for _ in range(5):
        xs_ = [jnp.einsum('bij,bjk->bik', x.astype(BF16), blockdiag2(x), preferred_element_type=F32)
               for x in xs_]
        ps_ = [p + jnp.einsum('bij,bjk->bik', p.astype(BF16), blockdiag2(x), preferred_element_type=F32)
               for p, x in zip(ps_, xs_)]
    for per, p in zip(pairs, ps_):
        for n_, (hd, lg, q, k, a, aqk, rhs) in enumerate(per):
            sol = _bmm(p[:, :, n_ * c:(n_ + 1) * c], rhs)
            qo = _bmm(aqk, sol)
            q2_s[hd] = (eg3[:, :, lg] * q - qo[:, :, :DN_DK]).astype(BF16)
            o1_s[hd] = qo[:, :, DN_DK:]
            kn = _bmm_tn(k * etail3[:, :, lg], sol)
            ktw_s[hd] = kn[:, :, :DN_DK].astype(BF16)
            n_s[hd] = kn[:, :, DN_DK:]
            egc_s[hd] = jnp.broadcast_to(egc3[:, :, lg], (nc, 1, DN_DV))

    def carry_body(ch, carry):
        for hd in range(DN_HEADS):
            s_h = S[hd]
            s_b = s_h.astype(BF16)
            sl_s[hd, ch] = s_b
            S[hd] = egc_s[hd, ch] * s_h + (n_s[hd, ch] - jnp.dot(ktw_s[hd, ch], s_b,
                                                              preferred_element_type=F32))
        return carry

    for ch in range(nc):
        carry_body(ch, 0)

    for hd in range(DN_HEADS):
        ls = slice(hd * DN_DK, (hd + 1) * DN_DK)
        o = o1_s[hd] + jnp.einsum('bck,bkv->bcv', q2_s[hd], sl_s[hd], preferred_element_type=F32)
        o = o * lax.rsqrt(jnp.mean(o * o, -1, keepdims=True) + RMS_EPS) * dng_ref[...]
        on_s[:, ls] = (o.reshape(tl, DN_DV) * _silu(og_s[:, ls])).astype(BF16)

    mixed = _sigmoid(ga_s[...]) * jnp.dot(cvn_s[...], wca_ref[...], preferred_element_type=F32)
    mixed = mixed + _sigmoid(gb_s[...]) * jnp.dot(on_s[...], wdn_ref[...], preferred_element_type=F32)
    h = _ln(DEEPNORM_ALPHA * x_ref[...] + _dot(mixed, wo_ref[...]), g1_ref[...], b1_ref[...])
    h_ref[...] = h
    xq_ref[...] = _dot(h, wxq_ref[...]).astype(BF16)

    @pl.when(t == nt - 1)
    def _():
        for sl in range(n_cs):
            pconv_ref[0, :, sl * LANES:(sl + 1) * LANES] = xx[sl, tl:tl + CONV_PAD, :]
        for s12 in range(n_qs):
            pqkv_ref[0, :, s12 * LANES:(s12 + 1) * LANES] = qx[s12, tl:tl + SHORT_PAD, :]
        sout_ref[0] = S[...]

    for sl in range(n_cs):
        xx[sl, 0:CONV_PAD, :] = xx[sl, tl:tl + CONV_PAD, :]
    for s12 in range(n_qs):
        qx[s12, 0:SHORT_PAD, :] = qx[s12, tl:tl + SHORT_PAD, :]


def _pfront(x, w_a, w_g, w_bd, bglu, wdw, bdw, lcg, lcb, wsh, alog, dtb, dng, wca, wdn, wo, wxq, g1, b1, batch, seq):
    tl = PM_TL
    nt = seq // tl
    nc = tl // CHUNK
    n = batch * seq
    row = lambda w: pl.BlockSpec((tl, w), lambda b, t: (b * nt + t, 0))
    per_seq = lambda *s: pl.BlockSpec((1,) + s, lambda b, t: (b,) + (0,) * len(s))
    return pl.pallas_call(
        _pfront_kernel,
        grid=(batch, nt),
        in_specs=[row(D_MODEL), _wspec((D_MODEL, N_A)), _wspec((D_MODEL, N_G)), _wspec((D_MODEL, LANES)),
                  _wspec((1, 2 * CONV_CH)),
                  _wspec((CONV_CH // LANES, CONV_K, 8, LANES)), _wspec((1, CONV_CH)), _wspec((1, CONV_CH)),
                  _wspec((1, CONV_CH)), _wspec((QKV_W // LANES, SHORT_K, 8, LANES)), _wspec((1, LANES)),
                  _wspec((1, LANES)), _wspec((1, DN_DV)),
                  _wspec((CONV_CH, D_MODEL)), _wspec((DN_V, D_MODEL)), _wspec((D_MODEL, D_MODEL)),
                  _wspec((D_MODEL, D_MODEL)), _wspec((1, D_MODEL)), _wspec((1, D_MODEL))],
        out_specs=[row(D_MODEL), row(D_MODEL), per_seq(CONV_PAD, CONV_CH), per_seq(SHORT_PAD, QKV_W),
                   per_seq(DN_HEADS, DN_DK, DN_DV)],
        out_shape=[jax.ShapeDtypeStruct((n, D_MODEL), F32), jax.ShapeDtypeStruct((n, D_MODEL), BF16),
                   jax.ShapeDtypeStruct((batch, CONV_PAD, CONV_CH), F32),
                   jax.ShapeDtypeStruct((batch, SHORT_PAD, QKV_W), F32),
                   jax.ShapeDtypeStruct((batch, DN_HEADS, DN_DK, DN_DV), F32)],
        scratch_shapes=[pltpu.VMEM((CONV_CH // LANES, tl + CONV_PAD, LANES), F32),
                        pltpu.VMEM((QKV_W // LANES, tl + SHORT_PAD, LANES), F32),
                        pltpu.VMEM((CONV_CH // LANES, tl, LANES), F32),
                        pltpu.VMEM((DN_HEADS, tl, DN_DK), F32), pltpu.VMEM((DN_HEADS, tl, DN_DK), F32),
                        pltpu.VMEM((DN_HEADS, tl, DN_DV), F32),
                        pltpu.VMEM((DN_HEADS, DN_DK, DN_DV), F32),
                        pltpu.VMEM((DN_HEADS, nc, DN_DK, DN_DK), BF16),
                        pltpu.VMEM((DN_HEADS, nc, DN_DK, DN_DV), F32),
                        pltpu.VMEM((DN_HEADS, nc, CHUNK, DN_DK), BF16),
                        pltpu.VMEM((DN_HEADS, nc, CHUNK, DN_DV), F32),
                        pltpu.VMEM((DN_HEADS, nc, DN_DK, DN_DV), BF16),
                        pltpu.VMEM((DN_HEADS, nc, 1, DN_DV), F32),
                        pltpu.VMEM((tl, CONV_CH), BF16), pltpu.VMEM((tl, DN_V), BF16),
                        pltpu.VMEM((tl, DN_V), F32), pltpu.VMEM((tl, D_MODEL), F32),
                        pltpu.VMEM((tl, D_MODEL), F32)],
        compiler_params=_params(("parallel", "arbitrary")),
        name="prompt_front",
    )(x, w_a, w_g, w_bd, bglu, wdw, bdw, lcg, lcb, wsh, alog, dtb, dng, wca, wdn, wo, wxq, g1, b1)


SM_BT = 32
SM_BS = 8


def _smid1_kernel(u_ref, qkv_ref, bd_ref, og_ref, hc_ref, hq_ref, wdw_ref, bdw_ref, lcg_ref, lcb_ref, wsh_ref,
                  alog_ref, dtb_ref,
                  cvn_ref, cnew_ref, qnew_ref, w_ref, q2_ref, uv_ref, o1_ref, kt_ref, ogs_ref, egc_ref):
    nl = u_ref.shape[0]
    nh = CONV_K - 1
    xx = [hc_ref[i] for i in range(nh)] + [u_ref[i] for i in range(nl)]
    for t in range(nl):
        acc = wdw_ref[0:1, :] * xx[t]
        for j in range(1, CONV_K):
            acc = acc + wdw_ref[j:j + 1, :] * xx[t + j]
        cvn_ref[t] = _silu(_ln(acc + bdw_ref[...], lcg_ref[...], lcb_ref[...]))
    for i in range(nh):
        cnew_ref[i] = xx[nl + i]

    ns = SHORT_K - 1
    qq = [hq_ref[i] for i in range(ns)] + [qkv_ref[i] for i in range(nl)]
    for i in range(ns):
        qnew_ref[i] = qq[nl + i]
    qkv = []
    for t in range(nl):
        acc = wsh_ref[0:1, :] * qq[t]
        for j in range(1, SHORT_K):
            acc = acc + wsh_ref[j:j + 1, :] * qq[t + j]
        qkv.append(_silu(acc))

    beta = [_sigmoid(bd_ref[t]) for t in range(nl)]
    ld = [-jnp.exp(alog_ref[...]) * _softplus(bd_ref[t] + dtb_ref[...]) for t in range(nl)]

    out_refs = (w_ref, uv_ref, o1_ref, q2_ref, kt_ref)
    rows = [[[None] * DN_HEADS for _ in range(nl)] for _ in out_refs]
    for hd in range(DN_HEADS):
        q = [_l2n(qkv[t][:, hd * DN_DK:(hd + 1) * DN_DK]) * (DN_DK ** -0.5) for t in range(nl)]
        k = [_l2n(qkv[t][:, DN_QK + hd * DN_DK:DN_QK + (hd + 1) * DN_DK]) for t in range(nl)]
        v = [qkv[t][:, 2 * DN_QK + hd * DN_DV:2 * DN_QK + (hd + 1) * DN_DV] for t in range(nl)]
        b = [beta[t][:, hd:hd + 1] for t in range(nl)]
        g = [ld[t][:, DN_HEADS + hd:DN_HEADS + hd + 1] for t in range(nl)]
        gc = [g[0]]
        for t in range(1, nl):
            gc.append(gc[t - 1] + g[t])
        w_l, uv_l = [], []
        for i in range(nl):
            wi = (b[i] * jnp.exp(gc[i])) * k[i]
            ui = b[i] * v[i]
            for j in range(i):
                a_ij = b[i] * jnp.sum(k[i] * k[j], -1, keepdims=True) * jnp.exp(gc[i] - gc[j])
                wi = wi - a_ij * w_l[j]
                ui = ui - a_ij * uv_l[j]
            w_l.append(wi)
            uv_l.append(ui)
        for i in range(nl):
            o1 = jnp.zeros_like(v[i])
            q2 = jnp.exp(gc[i]) * q[i]
            for j in range(i + 1):
                aqk = jnp.sum(q[i] * k[j], -1, keepdims=True) * jnp.exp(gc[i] - gc[j])
                o1 = o1 + aqk * uv_l[j]
                q2 = q2 - aqk * w_l[j]
            for r, val in zip(rows, (w_l[i], uv_l[i], o1, q2, k[i] * jnp.exp(gc[nl - 1] - gc[i]))):
                r[i][hd] = val
        egc_ref[hd] = jnp.broadcast_to(jnp.exp(gc[nl - 1]), (u_ref.shape[1], LANES))
    for i in range(nl):
        for ref, r in zip(out_refs, rows):
            ref[:, i, :] = jnp.concatenate(r[i], axis=1)
        ogs_ref[:, i, :] = og_ref[i]


def _smid1(u, qkv, bd, og, hc, hq, wdw, bdw, lcg, lcb, wsh, alog, dtb):
    nl, b, _ = u.shape
    bt = SM_BT
    tm = lambda r, w: pl.BlockSpec((r, bt, w), lambda i: (0, i, 0))
    sm = pl.BlockSpec((bt, nl, DN_V), lambda i: (i, 0, 0))
    sm_shape = jax.ShapeDtypeStruct((b, nl, DN_V), F32)
    tm_outs = [(nl, CONV_CH), (CONV_K - 1, CONV_CH), (SHORT_K - 1, QKV_W)]
    return pl.pallas_call(
        _smid1_kernel,
        grid=(b // bt,),
        in_specs=[tm(nl, CONV_CH), tm(nl, QKV_W), tm(nl, LANES), tm(nl, DN_V), tm(CONV_K - 1, CONV_CH),
                  tm(SHORT_K - 1, QKV_W),
                  _wspec((CONV_PAD, CONV_CH)), _wspec((1, CONV_CH)), _wspec((1, CONV_CH)),
                  _wspec((1, CONV_CH)), _wspec((SHORT_K, QKV_W)), _wspec((1, LANES)),
                  _wspec((1, LANES))],
        out_specs=[tm(r, w) for r, w in tm_outs] + [sm] * 6 + [tm(DN_HEADS, LANES)],
        out_shape=[jax.ShapeDtypeStruct((r, b, w), F32) for r, w in tm_outs] + [sm_shape] * 6
        + [jax.ShapeDtypeStruct((DN_HEADS, b, LANES), F32)],
        compiler_params=_params(("parallel",)),
        name="sample_mid1",
    )(u, qkv, bd, og, hc, hq, wdw, bdw, lcg, lcb, wsh, alog, dtb)


def _smid2_kernel(w_ref, q2_ref, uv_ref, o1_ref, kt_ref, og_ref, egc_ref, s_ref, dng_ref,
                  on_ref, snew_ref):
    nl = w_ref.shape[1]
    for hd in range(DN_HEADS):
        ls = slice(hd * DN_DK, (hd + 1) * DN_DK)
        s_h = s_ref[:, hd]
        lhs = jnp.concatenate([w_ref[:, :, ls], q2_ref[:, :, ls]], axis=1)
        r = jnp.einsum('bck,bkv->bcv', lhs.astype(BF16), s_h.astype(BF16),
                       preferred_element_type=F32)
        u_new = uv_ref[:, :, ls] - r[:, :nl]
        o = o1_ref[:, :, ls] + r[:, nl:]
        upd = jnp.einsum('bck,bcv->bkv', kt_ref[:, :, ls].astype(BF16), u_new.astype(BF16),
                         preferred_element_type=F32)
        snew_ref[:, hd] = egc_ref[:, hd] * s_h + upd
        o = o * lax.rsqrt(jnp.mean(o * o, -1, keepdims=True) + RMS_EPS) * dng_ref[...]
        on_ref[:, :, ls] = o * _silu(og_ref[:, :, ls])


def _smid2(w, q2, uv, o1, kt, og, egc, s, dng):
    b, nl, _ = w.shape
    bs = SM_BS
    sq = pl.BlockSpec((bs, nl, DN_V), lambda i: (i, 0, 0))
    st = pl.BlockSpec((bs, DN_HEADS, DN_DK, DN_DV), lambda i: (i, 0, 0, 0))
    return pl.pallas_call(
        _smid2_kernel,
        grid=(b // bs,),
        in_specs=[sq, sq, sq, sq, sq, sq,
                  pl.BlockSpec((bs, DN_HEADS, 1, LANES), lambda i: (i, 0, 0, 0)), st,
                  _wspec((1, DN_DV))],
        out_specs=[sq, st],
        out_shape=[jax.ShapeDtypeStruct((b, nl, DN_V), F32),
                   jax.ShapeDtypeStruct((b, DN_HEADS, DN_DK, DN_DV), F32)],
        compiler_params=_params(("parallel",)),
        name="sample_mid2",
    )(w, q2, uv, o1, kt, og, egc, s, dng)


def _tm(x):
    return jnp.swapaxes(x, 0, 1)


def kernel(x_prompt, x_sample, mem_prompt, state_conv, state_qkv_conv, state_delta, cache_mem_k, cache_mem_v, w_in, b_glu, w_dw, b_dw, ln_conv_g, ln_conv_b, w_conv_out, w_short, a_log, dt_bias, dn_norm_g, w_dn_out, w_o, ln1_g, ln1_b, w_xq, w_mem_kv, w_xo, ln2_g, ln2_b, w_ff1, w_ff2, ln3_g, ln3_b):
    assert w_in.shape[0] == DEPTH == 1
    bp, lp, _ = x_prompt.shape
    bsm, ls_, _ = x_sample.shape

    w = w_in[0]
    o_ga = N_A + 2 * DN_HEADS
    w_a = w[:, :N_A].astype(BF16)
    w_g = w[:, o_ga:o_ga + N_G].astype(BF16)
    w_bd = jnp.concatenate([w[:, N_A:o_ga], jnp.zeros((D_MODEL, LANES - 2 * DN_HEADS), w.dtype)],
                           axis=1).astype(BF16)
    lane_pad = lambda a: jnp.concatenate(
        [jnp.zeros((DN_HEADS,), F32), a.astype(F32), jnp.zeros((LANES - 2 * DN_HEADS,), F32)])[None]
    alog = lane_pad(a_log[0])
    dtb = lane_pad(dt_bias[0])
    wdw = jnp.concatenate([w_dw[0], jnp.zeros((CONV_PAD - CONV_K, CONV_CH), F32)], axis=0)
    r2 = lambda a: a[0][None]
    bglu, bdw, lcg, lcb = r2(b_glu), r2(b_dw), r2(ln_conv_g), r2(ln_conv_b)
    dng = r2(dn_norm_g)
    wsh = w_short[0]
    slabs = lambda a: jnp.broadcast_to(
        jnp.swapaxes(a.reshape(a.shape[0], a.shape[1] // LANES, LANES), 0, 1)[:, :, None, :],
        (a.shape[1] // LANES, a.shape[0], 8, LANES))
    wdw_slab = slabs(w_dw[0])
    wsh_slab = slabs(wsh)
    wca, wdn, wo, wxq, wxo = (a[0].astype(BF16) for a in (w_conv_out, w_dn_out, w_o, w_xq, w_xo))
    w1, w2, wkv = w_ff1[0].astype(BF16), w_ff2[0].astype(BF16), w_mem_kv[0].astype(BF16)
    g1, b1, g2, b2, g3, b3 = (r2(a) for a in (ln1_g, ln1_b, ln2_g, ln2_b, ln3_g, ln3_b))

    n_s = bsm * ls_
    xs = _tm(x_sample).reshape(n_s, D_MODEL)
    u, qkv, og, ga, gb, bd = _inproj(xs, w_a, w_g, w_bd, bglu, 256)
    t3 = lambda a: a.reshape(ls_, bsm, a.shape[-1])
    cvn, c_new, q_new, w_, q2, uv, o1, kt, og_s, egc = _smid1(
        t3(u), t3(qkv), t3(bd), t3(og), _tm(state_conv[0]), _tm(state_qkv_conv[0]),
        wdw, bdw, lcg, lcb, wsh, alog, dtb)
    on, s_delta = _smid2(w_, q2, uv, o1, kt, og_s, _tm(egc)[:, :, None, :], state_delta[0], dng)
    h_s, xq_s = _merge(cvn.reshape(n_s, CONV_CH), _tm(on).reshape(n_s, DN_V), ga, gb, xs,
                       wca, wdn, wo, wxq, g1, b1, 512)

    n_p = bp * lp
    xp = x_prompt.reshape(n_p, D_MODEL)
    mk, mv, mk_b, mv_b = _memkv(mem_prompt.reshape(bp * N_MEM, D_MODEL), wkv, 512)
    h, xq, p_conv, p_qkv, p_delta = _pfront(xp, w_a, w_g, w_bd, bglu, wdw_slab, bdw, lcg, lcb, wsh_slab,
                                            alog, dtb, dng, wca, wdn, wo, wxq, g1, b1, bp, lp)
    y_p, att_s = _ptail(xq, mk_b.reshape(bp, N_MEM, D_MODEL), mv_b.reshape(bp, N_MEM, D_MODEL), h,
                        wxo, w1, w2, g2, b2, g3, b3, _tm(t3(xq_s)), cache_mem_k, cache_mem_v, bp, lp, 512)
    y_p = y_p.reshape(bp, lp, D_MODEL)
    p_conv = p_conv[:, CONV_PAD - (CONV_K - 1):]
    p_qkv = p_qkv[:, SHORT_PAD - (SHORT_K - 1):]
    p_mk = mk.reshape(bp, N_MEM, X_HEADS, X_HEAD_DIM)
    p_mv = mv.reshape(bp, N_MEM, X_HEADS, X_HEAD_DIM)

    y_s = _tail(_tm(att_s).reshape(n_s, D_MODEL), h_s, wxo, w1, w2, g2, b2, g3, b3, 512)
    y_s = _tm(t3(y_s))

    return (y_p, y_s, p_conv[None], p_qkv[None], p_delta[None], p_mk[None], p_mv[None],
            _tm(c_new)[None], _tm(q_new)[None], s_delta[None])
```

```python
import functools

import jax
import jax.numpy as jnp
from jax import lax
from jax.experimental import pallas as pl
from jax.experimental.pallas import tpu as pltpu

F32 = jnp.float32
BF16 = jnp.bfloat16

D_MODEL = 1024
N_MEM = 256
CONV_CH = 512
CONV_K = 31
DN_HEADS = 4
DN_DK = 128
DN_DV = 128
DN_QK = DN_HEADS * DN_DK
DN_V = DN_HEADS * DN_DV
QKV_W = 2 * DN_QK + DN_V
SHORT_K = 4
CHUNK = 64
X_HEADS = 4
X_HEAD_DIM = D_MODEL // X_HEADS
D_FF = 4 * D_MODEL
DEPTH = 1
DEEPNORM_ALPHA = (2 * DEPTH) ** 0.25
LN_EPS = 1e-5
RMS_EPS = 1e-6
L2_EPS = 1e-6

LANES = 128
OFF_QKV = 2 * CONV_CH
OFF_OG = OFF_QKV + QKV_W
N_A = OFF_OG + DN_V
N_G = 2 * D_MODEL
VMEM_LIMIT = 56 * 1024 * 1024


def _dot(a, b):
    return jnp.dot(a.astype(BF16), b.astype(BF16), preferred_element_type=F32)


def _dot_nt(a, b):
    return lax.dot_general(a.astype(BF16), b.astype(BF16), (((1,), (1,)), ((), ())),
                           preferred_element_type=F32)


def _sigmoid(x):
    return 1.0 / (1.0 + jnp.exp(-x))


def _silu(x):
    return x * _sigmoid(x)


def _softplus(x):
    return jnp.maximum(x, 0.0) + jnp.log(1.0 + jnp.exp(-jnp.abs(x)))


def _ln(x, g, b):
    mu = jnp.mean(x, -1, keepdims=True)
    xc = x - mu
    var = jnp.mean(xc * xc, -1, keepdims=True)
    return xc * lax.rsqrt(var + LN_EPS) * g + b


def _l2n(x):
    return x * lax.rsqrt(jnp.sum(x * x, -1, keepdims=True) + L2_EPS)


def _wspec(shape):
    return pl.BlockSpec(shape, lambda *_: (0,) * len(shape), pipeline_mode=pl.Buffered(1))


def _params(sem):
    return pltpu.CompilerParams(dimension_semantics=sem, vmem_limit_bytes=VMEM_LIMIT)


def _wprep_kernel(w_ref, wa_ref, wg_ref, wbd_ref):
    w = w_ref[...]
    o_ga = N_A + 2 * DN_HEADS
    wa_ref[...] = w[:, :N_A].astype(BF16)
    wg_ref[...] = w[:, o_ga:o_ga + N_G].astype(BF16)
    head = w[:, N_A:N_A + LANES]
    lane = lax.broadcasted_iota(jnp.int32, head.shape, 1)
    wbd_ref[...] = jnp.where(lane < 2 * DN_HEADS, head, 0.0).astype(BF16)


def _wprep(w_in, tm):
    n_in = w_in.shape[2]
    row = lambda w: pl.BlockSpec((tm, w), lambda i: (i, 0))
    return pl.pallas_call(
        _wprep_kernel,
        grid=(D_MODEL // tm,),
        in_specs=[pl.BlockSpec((None, tm, n_in), lambda i: (0, i, 0))],
        out_specs=[row(N_A), row(N_G), row(LANES)],
        out_shape=[jax.ShapeDtypeStruct((D_MODEL, N_A), BF16), jax.ShapeDtypeStruct((D_MODEL, N_G), BF16),
                   jax.ShapeDtypeStruct((D_MODEL, LANES), BF16)],
        compiler_params=_params(("parallel",)),
        name="wprep",
    )(w_in)


def _inproj_kernel(x_ref, wa_ref, wg_ref, wbd_ref, bglu_ref, u_ref, qkv_ref, og_ref, ga_ref, gb_ref, bd_ref):
    x = x_ref[...].astype(BF16)
    glu = jnp.dot(x, wa_ref[:, 0:OFF_QKV], preferred_element_type=F32) + bglu_ref[...]
    u_ref[...] = glu[:, :CONV_CH] * _sigmoid(glu[:, CONV_CH:])
    qkv_ref[...] = jnp.dot(x, wa_ref[:, OFF_QKV:OFF_OG], preferred_element_type=F32)
    og_ref[...] = jnp.dot(x, wa_ref[:, OFF_OG:N_A], preferred_element_type=F32)
    ga_ref[...] = jnp.dot(x, wg_ref[:, :D_MODEL], preferred_element_type=F32)
    gb_ref[...] = jnp.dot(x, wg_ref[:, D_MODEL:], preferred_element_type=F32)
    bd_ref[...] = jnp.dot(x, wbd_ref[...], preferred_element_type=F32)


def _inproj(x, w_a, w_g, w_bd, b_glu, tm):
    n = x.shape[0]
    widths = (CONV_CH, QKV_W, DN_V, D_MODEL, D_MODEL, LANES)
    row = lambda w: pl.BlockSpec((tm, w), lambda i: (i, 0))
    return pl.pallas_call(
        _inproj_kernel,
        grid=(n // tm,),
        in_specs=[row(D_MODEL), _wspec((D_MODEL, N_A)), _wspec((D_MODEL, N_G)), _wspec((D_MODEL, LANES)),
                  _wspec((1, 2 * CONV_CH))],
        out_specs=[row(w) for w in widths],
        out_shape=[jax.ShapeDtypeStruct((n, w), F32) for w in widths],
        compiler_params=_params(("parallel",)),
        name="inproj",
    )(x, w_a, w_g, w_bd, b_glu)


def _merge_kernel(cvn_ref, on_ref, ga_ref, gb_ref, x_ref, wca_ref, wdn_ref, wo_ref, wxq_ref,
                  g1_ref, b1_ref, h_ref, xq_ref):
    br_a = _dot(cvn_ref[...], wca_ref[...])
    br_b = _dot(on_ref[...], wdn_ref[...])
    mixed = _sigmoid(ga_ref[...]) * br_a + _sigmoid(gb_ref[...]) * br_b
    h = _ln(DEEPNORM_ALPHA * x_ref[...] + _dot(mixed, wo_ref[...]), g1_ref[...], b1_ref[...])
    h_ref[...] = h
    xq_ref[...] = _dot(h, wxq_ref[...])


def _merge(cvn, on, ga, gb, x, wca, wdn, wo, wxq, g1, b1, tm):
    n = x.shape[0]
    row = lambda w: pl.BlockSpec((tm, w), lambda i: (i, 0))
    return pl.pallas_call(
        _merge_kernel,
        grid=(n // tm,),
        in_specs=[row(CONV_CH), row(DN_V), row(D_MODEL), row(D_MODEL), row(D_MODEL),
                  _wspec((CONV_CH, D_MODEL)), _wspec((DN_V, D_MODEL)), _wspec((D_MODEL, D_MODEL)),
                  _wspec((D_MODEL, D_MODEL)), _wspec((1, D_MODEL)), _wspec((1, D_MODEL))],
        out_specs=[row(D_MODEL), row(D_MODEL)],
        out_shape=[jax.ShapeDtypeStruct((n, D_MODEL), F32)] * 2,
        compiler_params=_params(("parallel",)),
        name="merge",
    )(cvn, on, ga, gb, x, wca, wdn, wo, wxq, g1, b1)


FF_BLOCK = 1024


def _tail_kernel(att_ref, h_ref, wxo_ref, w1_ref, w2_ref, g2_ref, b2_ref, g3_ref, b3_ref, y_ref):
    xo = _dot(att_ref[...], wxo_ref[...])
    h2 = _ln(DEEPNORM_ALPHA * h_ref[...] + xo, g2_ref[...], b2_ref[...])
    h2b = h2.astype(BF16)
    ff = jnp.zeros(h2.shape, F32)
    for c in range(D_FF // FF_BLOCK):
        a = jnp.dot(h2b, w1_ref[:, c * FF_BLOCK:(c + 1) * FF_BLOCK], preferred_element_type=F32)
        a = jnp.square(jnp.maximum(a, 0.0))
        ff = ff + jnp.dot(a.astype(BF16), w2_ref[c * FF_BLOCK:(c + 1) * FF_BLOCK, :],
                          preferred_element_type=F32)
    y_ref[...] = _ln(DEEPNORM_ALPHA * h2 + ff, g3_ref[...], b3_ref[...])


def _tail(att, h, wxo, w1, w2, g2, b2, g3, b3, tm):
    n = h.shape[0]
    row = lambda w: pl.BlockSpec((tm, w), lambda i: (i, 0))
    return pl.pallas_call(
        _tail_kernel,
        grid=(n // tm,),
        in_specs=[row(D_MODEL), row(D_MODEL), _wspec((D_MODEL, D_MODEL)), _wspec((D_MODEL, D_FF)),
                  _wspec((D_FF, D_MODEL)), _wspec((1, D_MODEL)), _wspec((1, D_MODEL)),
                  _wspec((1, D_MODEL)), _wspec((1, D_MODEL))],
        out_specs=row(D_MODEL),
        out_shape=jax.ShapeDtypeStruct((n, D_MODEL), F32),
        compiler_params=_params(("parallel",)),
        name="tail",
    )(att, h, wxo, w1, w2, g2, b2, g3, b3)


def _memkv_kernel(m_ref, w_ref, k_ref, v_ref, kb_ref, vb_ref):
    m = m_ref[...].astype(BF16)
    tm = m.shape[0]
    k = jnp.dot(m, w_ref[:, :D_MODEL], preferred_element_type=F32)
    v = jnp.dot(m, w_ref[:, D_MODEL:], preferred_element_type=F32)
    k_ref[...] = k.reshape(tm, X_HEADS, X_HEAD_DIM)
    v_ref[...] = v.reshape(tm, X_HEADS, X_HEAD_DIM)
    kb_ref[...] = k.astype(BF16)
    vb_ref[...] = v.astype(BF16)


def _memkv(mem, w, tm):
    n = mem.shape[0]
    row = pl.BlockSpec((tm, D_MODEL), lambda i: (i, 0))
    row4 = pl.BlockSpec((tm, X_HEADS, X_HEAD_DIM), lambda i: (i, 0, 0))
    return pl.pallas_call(
        _memkv_kernel,
        grid=(n // tm,),
        in_specs=[row, _wspec((D_MODEL, 2 * D_MODEL))],
        out_specs=[row4, row4, row, row],
        out_shape=[jax.ShapeDtypeStruct((n, X_HEADS, X_HEAD_DIM), F32)] * 2
        + [jax.ShapeDtypeStruct((n, D_MODEL), BF16)] * 2,
        compiler_params=_params(("parallel",)),
        name="memkv",
    )(mem, w)


def _softmax(sc):
    sc = sc - jnp.max(sc, -1, keepdims=True)
    e = jnp.exp(sc)
    return e / jnp.sum(e, -1, keepdims=True)


def _ptail_kernel(q_ref, k_ref, v_ref, h_ref, wxo_ref, w1_ref, w2_ref, g2_ref, b2_ref, g3_ref, b3_ref,
                  sq_ref, sk_ref, sv_ref, y_ref, so_ref, att_s, *, nseq):
    scale = X_HEAD_DIM ** -0.5
    for hd in range(X_HEADS):
        sl = slice(hd * X_HEAD_DIM, (hd + 1) * X_HEAD_DIM)
        pr = _softmax(_dot_nt(q_ref[:, sl], k_ref[0, :, sl]) * scale)
        att_s[:, sl] = _dot(pr, v_ref[0, :, sl]).astype(BF16)
    _attn_split_kernel(sq_ref, sk_ref, sv_ref, so_ref, nseq=nseq)
    _tail_kernel(att_s, h_ref, wxo_ref, w1_ref, w2_ref, g2_ref, b2_ref, g3_ref, b3_ref, y_ref)


def _ptail(xq, kb, vb, h, wxo, w1, w2, g2, b2, g3, b3, sq, sk, sv, batch, seq, tq):
    nt = seq // tq
    sb, sl_, _ = sq.shape
    nseq = sb // (batch * nt)
    row = lambda: pl.BlockSpec((tq, D_MODEL), lambda b, t: (b * nt + t, 0))
    mem = pl.BlockSpec((1, N_MEM, D_MODEL), lambda b, t: (b, 0, 0))
    s_q = pl.BlockSpec((nseq, sl_, D_MODEL), lambda b, t: (b * nt + t, 0, 0))
    s_kv = pl.BlockSpec((None, nseq, N_MEM, X_HEADS, X_HEAD_DIM), lambda b, t: (0, b * nt + t, 0, 0, 0))
    return pl.pallas_call(
        functools.partial(_ptail_kernel, nseq=nseq),
        grid=(batch, nt),
        in_specs=[row(), mem, mem, row(), _wspec((D_MODEL, D_MODEL)), _wspec((D_MODEL, D_FF)),
                  _wspec((D_FF, D_MODEL)), _wspec((1, D_MODEL)), _wspec((1, D_MODEL)),
                  _wspec((1, D_MODEL)), _wspec((1, D_MODEL)), s_q, s_kv, s_kv],
        out_specs=[row(), s_q],
        out_shape=[jax.ShapeDtypeStruct((batch * seq, D_MODEL), F32),
                   jax.ShapeDtypeStruct((sb, sl_, D_MODEL), F32)],
        scratch_shapes=[pltpu.VMEM((tq, D_MODEL), BF16)],
        compiler_params=_params(("parallel", "parallel")),
        name="prompt_tail",
    )(xq, kb, vb, h, wxo, w1, w2, g2, b2, g3, b3, sq, sk, sv)


def _attn_split_kernel(q_ref, k_ref, v_ref, o_ref, *, nseq):
    scale = X_HEAD_DIM ** -0.5
    nl = q_ref.shape[1]
    nk = N_MEM * X_HEADS
    row_head = lax.broadcasted_iota(jnp.int32, (X_HEADS * nl, nk), 0) // nl
    col_head = lax.broadcasted_iota(jnp.int32, (X_HEADS * nl, nk), 1) % X_HEADS
    own = row_head == col_head
    for s in range(nseq):
        k2 = k_ref[s].reshape(nk, X_HEAD_DIM)
        v2 = v_ref[s].reshape(nk, X_HEAD_DIM)
        q = q_ref[s]
        q4 = jnp.concatenate([q[:, h * X_HEAD_DIM:(h + 1) * X_HEAD_DIM] for h in range(X_HEADS)], axis=0)
        pr = _softmax(jnp.where(own, _dot_nt(q4, k2) * scale, -1e30))
        o4 = _dot(pr, v2)
        for h in range(X_HEADS):
            o_ref[s, :, h * X_HEAD_DIM:(h + 1) * X_HEAD_DIM] = o4[h * nl:(h + 1) * nl]


PM_TL = 512
ROW_STRIDE = 4
CONV_RB = 8 * ROW_STRIDE
CONV_PAD = 32
SHORT_PAD = 8


def _split3(x):
    h = x.astype(BF16)
    r = x - h.astype(F32)
    m = r.astype(BF16)
    l = (r - m.astype(F32)).astype(BF16)
    return h, m, l


def _dot_exact_lhs(lhs_bf16, x):
    h, m, l = _split3(x)
    d = lambda p: jnp.dot(lhs_bf16, p, preferred_element_type=F32)
    return d(h) + d(m) + d(l)


def _bmm(a, b):
    return jnp.einsum('bij,bjk->bik', a.astype(BF16), b.astype(BF16), preferred_element_type=F32)


def _bmm_nt(a, b):
    return jnp.einsum('bik,bjk->bij', a.astype(BF16), b.astype(BF16), preferred_element_type=F32)


def _bmm_tn(a, b):
    return jnp.einsum('bki,bkj->bij', a.astype(BF16), b.astype(BF16), preferred_element_type=F32)


def _strided_conv(src_ref, w_ref, dst_ref, slab, dslab, nblk, ntaps, off, post):
    ws = [w_ref[slab, j] for j in range(ntaps)]

    def body(rb, carry):
        r0 = pl.multiple_of(rb * CONV_RB, CONV_RB)
        accs = [None] * ROW_STRIDE
        for q in range(off, off + ntaps + ROW_STRIDE - 1):
            x = src_ref[slab, pl.ds(r0 + q, 8, stride=ROW_STRIDE), :]
            for m in range(ROW_STRIDE):
                j = q - off - m
                if 0 <= j < ntaps:
                    term = ws[j] * x
                    accs[m] = term if accs[m] is None else accs[m] + term
        for m in range(ROW_STRIDE):
            dst_ref[dslab, pl.ds(r0 + m, 8, stride=ROW_STRIDE), :] = post(accs[m])
        return carry

    lax.fori_loop(0, nblk, body, 0, unroll=True)


def _pfront_kernel(x_ref, wa_ref, wg_ref, wbd_ref, bglu_ref, wdw_ref, bdw_ref, lcg_ref, lcb_ref, wsh_ref, alog_ref, dtb_ref,
                   dng_ref, wca_ref, wdn_ref, wo_ref, wxq_ref, g1_ref, b1_ref,
                   h_ref, xq_ref, pconv_ref, pqkv_ref, sout_ref,
                   xx, qx, ya, qs, ks, vs, S, ktw_s, n_s, q2_s, o1_s, sl_s, egc_s, cvn_s, on_s, og_s, ga_s, gb_s):
    t = pl.program_id(1)
    nt = pl.num_programs(1)
    tl = PM_TL
    c = CHUNK
    nc = tl // c
    n_cs = CONV_CH // LANES
    n_qs = QKV_W // LANES

    @pl.when(t == 0)
    def _():
        xx[:, 0:CONV_PAD, :] = jnp.zeros((n_cs, CONV_PAD, LANES), F32)
        qx[:, 0:SHORT_PAD, :] = jnp.zeros((n_qs, SHORT_PAD, LANES), F32)
        S[...] = jnp.zeros(S.shape, F32)

    xb = x_ref[...].astype(BF16)
    proj = lambda lo, hi: jnp.dot(xb, wa_ref[:, lo:hi], preferred_element_type=F32)
    gate = lambda lo, hi: jnp.dot(xb, wg_ref[:, lo:hi], preferred_element_type=F32)
    conv_a = lambda sl: _strided_conv(xx, wdw_ref, ya, sl, sl, tl // CONV_RB, CONV_K,
                                      CONV_PAD - (CONV_K - 1), lambda y: y)

    def conv_b(part):
        for hd in range(DN_HEADS):
            _strided_conv(qx, wsh_ref, (qs, ks, vs)[part], part * DN_HEADS + hd, hd, tl // CONV_RB, SHORT_K,
                          SHORT_PAD - (SHORT_K - 1), _silu)

    def qkv_part(part):
        y = proj(OFF_QKV + part * DN_QK, OFF_QKV + (part + 1) * DN_QK)
        for hd in range(DN_HEADS):
            qx[part * DN_HEADS + hd, SHORT_PAD:SHORT_PAD + tl, :] = y[:, hd * LANES:(hd + 1) * LANES]

    glu = proj(0, OFF_QKV) + bglu_ref[...]
    u = glu[:, :CONV_CH] * _sigmoid(glu[:, CONV_CH:])
    for sl in range(n_cs):
        xx[sl, CONV_PAD:CONV_PAD + tl, :] = u[:, sl * LANES:(sl + 1) * LANES]
    qkv_part(0)
    conv_a(0)
    qkv_part(1)
    conv_a(1)
    qkv_part(2)
    conv_a(2)
    og_s[...] = proj(OFF_OG, N_A)
    conv_a(3)
    ga_s[:, :DN_V] = gate(0, DN_V)
    conv_b(0)
    ga_s[:, DN_V:] = gate(DN_V, D_MODEL)
    conv_b(1)
    gb_s[:, :DN_V] = gate(D_MODEL, D_MODEL + DN_V)
    conv_b(2)
    gb_s[:, DN_V:] = gate(D_MODEL + DN_V, N_G)
    bd = jnp.dot(xb, wbd_ref[...], preferred_element_type=F32)
    for rb in range(tl // CHUNK):
        r0 = rb * CHUNK
        cv = jnp.concatenate([ya[sl, r0:r0 + CHUNK, :] for sl in range(n_cs)], axis=1) + bdw_ref[...]
        cvn_s[r0:r0 + CHUNK, :] = _silu(_ln(cv, lcg_ref[...], lcb_ref[...])).astype(BF16)

    beta = _sigmoid(bd)
    ld = -jnp.exp(alog_ref[...]) * _softplus(bd + dtb_ref[...])
    rt = lax.broadcasted_iota(jnp.int32, (tl, tl), 0)
    ct = lax.broadcasted_iota(jnp.int32, (tl, tl), 1)
    blk_tril = jnp.where((rt >= ct) & ((rt // c) == (ct // c)), 1.0, 0.0).astype(BF16)
    gsum = _dot_exact_lhs(blk_tril, ld)
    beta3 = beta.reshape(nc, c, LANES)
    gsum3 = gsum.reshape(nc, c, LANES)
    gsum_t = gsum.T
    glast3 = gsum3[:, c - 1:c, :]
    eg3 = jnp.exp(gsum3)
    etail3 = jnp.exp(glast3 - gsum3)
    egc3 = jnp.exp(glast3)

    ri = lax.broadcasted_iota(jnp.int32, (c, c), 0)
    ci = lax.broadcasted_iota(jnp.int32, (c, c), 1)
    tril = (ri >= ci)[None]
    strict = (ri > ci)[None]
    eye = jnp.where(ri == ci, 1.0, 0.0)[None]

    lane2 = lax.broadcasted_iota(jnp.int32, (c, 2 * c), 1)
    left = (lane2 < c)[None]
    eye2 = jnp.concatenate([eye, eye], axis=2)

    def blockdiag2(m):
        return jnp.concatenate([jnp.where(left, m, 0.0), jnp.where(left, 0.0, m)], axis=1).astype(BF16)

    pairs = []
    for h0 in range(0, DN_HEADS, 2):
        per = []
        for hd in (h0, h0 + 1):
            lb = slice(hd, hd + 1)
            lg = slice(DN_HEADS + hd, DN_HEADS + hd + 1)
            q = _l2n(qs[hd].reshape(nc, c, DN_DK)) * (DN_DK ** -0.5)
            k = _l2n(ks[hd].reshape(nc, c, DN_DK))
            v = vs[hd].reshape(nc, c, DN_DV)
            b_col = beta3[:, :, lb]
            g_row = jnp.stack([gsum_t[DN_HEADS + hd:DN_HEADS + hd + 1, ch * c:(ch + 1) * c] for ch in range(nc)])
            diff = gsum3[:, :, lg] - g_row
            decay = jnp.where(tril, jnp.exp(jnp.where(tril, diff, 0.0)), 0.0)
            kq = _bmm_nt(jnp.concatenate([k, q], axis=1), k)
            a = jnp.where(strict, b_col * kq[:, :c] * decay, 0.0)
            aqk = kq[:, c:] * decay
            rhs = jnp.concatenate([(b_col * eg3[:, :, lg]) * k, b_col * v], axis=2)
            per.append((hd, lg, q, k, a, aqk, rhs))
        pairs.append(per)
    xs_ = [jnp.concatenate([per[0][4], per[1][4]], axis=2) for per in pairs]
    ps_ = [eye2 - x for x in xs_]
    for _ in range(5):
        xs_ = [jnp.einsum('bij,bjk->bik', x.astype(BF16), blockdiag2(x), preferred_element_type=F32)
               for x in xs_]
        ps_ = [p + jnp.einsum('bij,bjk->bik', p.astype(BF16), blockdiag2(x), preferred_element_type=F32)
               for p, x in zip(ps_, xs_)]
    for per, p in zip(pairs, ps_):
        for n_, (hd, lg, q, k, a, aqk, rhs) in enumerate(per):
            sol = _bmm(p[:, :, n_ * c:(n_ + 1) * c], rhs)
            qo = _bmm(aqk, sol)
            q2_s[hd] = (eg3[:, :, lg] * q - qo[:, :, :DN_DK]).astype(BF16)
            o1_s[hd] = qo[:, :, DN_DK:]
            kn = _bmm_tn(k * etail3[:, :, lg], sol)
            ktw_s[hd] = kn[:, :, :DN_DK].astype(BF16)
            n_s[hd] = kn[:, :, DN_DK:]
            egc_s[hd] = jnp.broadcast_to(egc3[:, :, lg], (nc, 1, DN_DV))

    def carry_body(ch, carry):
        for hd in range(DN_HEADS):
            s_h = S[hd]
            s_b = s_h.astype(BF16)
            sl_s[hd, ch] = s_b
            S[hd] = egc_s[hd, ch] * s_h + (n_s[hd, ch] - jnp.dot(ktw_s[hd, ch], s_b,
                                                              preferred_element_type=F32))
        return carry

    for ch in range(nc):
        carry_body(ch, 0)

    for hd in range(DN_HEADS):
        ls = slice(hd * DN_DK, (hd + 1) * DN_DK)
        o = o1_s[hd] + jnp.einsum('bck,bkv->bcv', q2_s[hd], sl_s[hd], preferred_element_type=F32)
        o = o * lax.rsqrt(jnp.mean(o * o, -1, keepdims=True) + RMS_EPS) * dng_ref[...]
        on_s[:, ls] = (o.reshape(tl, DN_DV) * _silu(og_s[:, ls])).astype(BF16)

    mixed = _sigmoid(ga_s[...]) * jnp.dot(cvn_s[...], wca_ref[...], preferred_element_type=F32)
    mixed = mixed + _sigmoid(gb_s[...]) * jnp.dot(on_s[...], wdn_ref[...], preferred_element_type=F32)
    h = _ln(DEEPNORM_ALPHA * x_ref[...] + _dot(mixed, wo_ref[...]), g1_ref[...], b1_ref[...])
    h_ref[...] = h
    xq_ref[...] = _dot(h, wxq_ref[...]).astype(BF16)

    @pl.when(t == nt - 1)
    def _():
        for sl in range(n_cs):
            pconv_ref[0, :, sl * LANES:(sl + 1) * LANES] = xx[sl, tl:tl + CONV_PAD, :]
        for s12 in range(n_qs):
            pqkv_ref[0, :, s12 * LANES:(s12 + 1) * LANES] = qx[s12, tl:tl + SHORT_PAD, :]
        sout_ref[0] = S[...]

    for sl in range(n_cs):
        xx[sl, 0:CONV_PAD, :] = xx[sl, tl:tl + CONV_PAD, :]
    for s12 in range(n_qs):
        qx[s12, 0:SHORT_PAD, :] = qx[s12, tl:tl + SHORT_PAD, :]


def _pfront(x, w_a, w_g, w_bd, bglu, wdw, bdw, lcg, lcb, wsh, alog, dtb, dng, wca, wdn, wo, wxq, g1, b1, batch, seq):
    tl = PM_TL
    nt = seq // tl
    nc = tl // CHUNK
    n = batch * seq
    row = lambda w: pl.BlockSpec((tl, w), lambda b, t: (b * nt + t, 0))
    per_seq = lambda *s: pl.BlockSpec((1,) + s, lambda b, t: (b,) + (0,) * len(s))
    return pl.pallas_call(
        _pfront_kernel,
        grid=(batch, nt),
        in_specs=[row(D_MODEL), _wspec((D_MODEL, N_A)), _wspec((D_MODEL, N_G)), _wspec((D_MODEL, LANES)),
                  _wspec((1, 2 * CONV_CH)),
                  _wspec((CONV_CH // LANES, CONV_K, 8, LANES)), _wspec((1, CONV_CH)), _wspec((1, CONV_CH)),
                  _wspec((1, CONV_CH)), _wspec((QKV_W // LANES, SHORT_K, 8, LANES)), _wspec((1, LANES)),
                  _wspec((1, LANES)), _wspec((1, DN_DV)),
                  _wspec((CONV_CH, D_MODEL)), _wspec((DN_V, D_MODEL)), _wspec((D_MODEL, D_MODEL)),
                  _wspec((D_MODEL, D_MODEL)), _wspec((1, D_MODEL)), _wspec((1, D_MODEL))],
        out_specs=[row(D_MODEL), row(D_MODEL), per_seq(CONV_PAD, CONV_CH), per_seq(SHORT_PAD, QKV_W),
                   per_seq(DN_HEADS, DN_DK, DN_DV)],
        out_shape=[jax.ShapeDtypeStruct((n, D_MODEL), F32), jax.ShapeDtypeStruct((n, D_MODEL), BF16),
                   jax.ShapeDtypeStruct((batch, CONV_PAD, CONV_CH), F32),
                   jax.ShapeDtypeStruct((batch, SHORT_PAD, QKV_W), F32),
                   jax.ShapeDtypeStruct((batch, DN_HEADS, DN_DK, DN_DV), F32)],
        scratch_shapes=[pltpu.VMEM((CONV_CH // LANES, tl + CONV_PAD, LANES), F32),
                        pltpu.VMEM((QKV_W // LANES, tl + SHORT_PAD, LANES), F32),
                        pltpu.VMEM((CONV_CH // LANES, tl, LANES), F32),
                        pltpu.VMEM((DN_HEADS, tl, DN_DK), F32), pltpu.VMEM((DN_HEADS, tl, DN_DK), F32),
                        pltpu.VMEM((DN_HEADS, tl, DN_DV), F32),
                        pltpu.VMEM((DN_HEADS, DN_DK, DN_DV), F32),
                        pltpu.VMEM((DN_HEADS, nc, DN_DK, DN_DK), BF16),
                        pltpu.VMEM((DN_HEADS, nc, DN_DK, DN_DV), F32),
                        pltpu.VMEM((DN_HEADS, nc, CHUNK, DN_DK), BF16),
                        pltpu.VMEM((DN_HEADS, nc, CHUNK, DN_DV), F32),
                        pltpu.VMEM((DN_HEADS, nc, DN_DK, DN_DV), BF16),
                        pltpu.VMEM((DN_HEADS, nc, 1, DN_DV), F32),
                        pltpu.VMEM((tl, CONV_CH), BF16), pltpu.VMEM((tl, DN_V), BF16),
                        pltpu.VMEM((tl, DN_V), F32), pltpu.VMEM((tl, D_MODEL), F32),
                        pltpu.VMEM((tl, D_MODEL), F32)],
        compiler_params=_params(("parallel", "arbitrary")),
        name="prompt_front",
    )(x, w_a, w_g, w_bd, bglu, wdw, bdw, lcg, lcb, wsh, alog, dtb, dng, wca, wdn, wo, wxq, g1, b1)


SM_BT = 32
SM_BS = 8


def _smid1_kernel(u_ref, qkv_ref, bd_ref, og_ref, hc_ref, hq_ref, wdw_ref, bdw_ref, lcg_ref, lcb_ref, wsh_ref,
                  alog_ref, dtb_ref,
                  cvn_ref, cnew_ref, qnew_ref, w_ref, q2_ref, uv_ref, o1_ref, kt_ref, ogs_ref, egc_ref):
    nl = u_ref.shape[0]
    nh = CONV_K - 1
    xx = [hc_ref[i] for i in range(nh)] + [u_ref[i] for i in range(nl)]
    for t in range(nl):
        acc = wdw_ref[0:1, :] * xx[t]
        for j in range(1, CONV_K):
            acc = acc + wdw_ref[j:j + 1, :] * xx[t + j]
        cvn_ref[t] = _silu(_ln(acc + bdw_ref[...], lcg_ref[...], lcb_ref[...]))
    for i in range(nh):
        cnew_ref[i] = xx[nl + i]

    ns = SHORT_K - 1
    qq = [hq_ref[i] for i in range(ns)] + [qkv_ref[i] for i in range(nl)]
    for i in range(ns):
        qnew_ref[i] = qq[nl + i]
    qkv = []
    for t in range(nl):
        acc = wsh_ref[0:1, :] * qq[t]
        for j in range(1, SHORT_K):
            acc = acc + wsh_ref[j:j + 1, :] * qq[t + j]
        qkv.append(_silu(acc))

    beta = [_sigmoid(bd_ref[t]) for t in range(nl)]
    ld = [-jnp.exp(alog_ref[...]) * _softplus(bd_ref[t] + dtb_ref[...]) for t in range(nl)]

    out_refs = (w_ref, uv_ref, o1_ref, q2_ref, kt_ref)
    rows = [[[None] * DN_HEADS for _ in range(nl)] for _ in out_refs]
    for hd in range(DN_HEADS):
        q = [_l2n(qkv[t][:, hd * DN_DK:(hd + 1) * DN_DK]) * (DN_DK ** -0.5) for t in range(nl)]
        k = [_l2n(qkv[t][:, DN_QK + hd * DN_DK:DN_QK + (hd + 1) * DN_DK]) for t in range(nl)]
        v = [qkv[t][:, 2 * DN_QK + hd * DN_DV:2 * DN_QK + (hd + 1) * DN_DV] for t in range(nl)]
        b = [beta[t][:, hd:hd + 1] for t in range(nl)]
        g = [ld[t][:, DN_HEADS + hd:DN_HEADS + hd + 1] for t in range(nl)]
        gc = [g[0]]
        for t in range(1, nl):
            gc.append(gc[t - 1] + g[t])
        w_l, uv_l = [], []
        for i in range(nl):
            wi = (b[i] * jnp.exp(gc[i])) * k[i]
            ui = b[i] * v[i]
            for j in range(i):
                a_ij = b[i] * jnp.sum(k[i] * k[j], -1, keepdims=True) * jnp.exp(gc[i] - gc[j])
                wi = wi - a_ij * w_l[j]
                ui = ui - a_ij * uv_l[j]
            w_l.append(wi)
            uv_l.append(ui)
        for i in range(nl):
            o1 = jnp.zeros_like(v[i])
            q2 = jnp.exp(gc[i]) * q[i]
            for j in range(i + 1):
                aqk = jnp.sum(q[i] * k[j], -1, keepdims=True) * jnp.exp(gc[i] - gc[j])
                o1 = o1 + aqk * uv_l[j]
                q2 = q2 - aqk * w_l[j]
            for r, val in zip(rows, (w_l[i], uv_l[i], o1, q2, k[i] * jnp.exp(gc[nl - 1] - gc[i]))):
                r[i][hd] = val
        egc_ref[hd] = jnp.broadcast_to(jnp.exp(gc[nl - 1]), (u_ref.shape[1], LANES))
    for i in range(nl):
        for ref, r in zip(out_refs, rows):
            ref[:, i, :] = jnp.concatenate(r[i], axis=1)
        ogs_ref[:, i, :] = og_ref[i]


def _smid1(u, qkv, bd, og, hc, hq, wdw, bdw, lcg, lcb, wsh, alog, dtb):
    nl, b, _ = u.shape
    bt = SM_BT
    tm = lambda r, w: pl.BlockSpec((r, bt, w), lambda i: (0, i, 0))
    sm = pl.BlockSpec((bt, nl, DN_V), lambda i: (i, 0, 0))
    sm_shape = jax.ShapeDtypeStruct((b, nl, DN_V), F32)
    tm_outs = [(nl, CONV_CH), (CONV_K - 1, CONV_CH), (SHORT_K - 1, QKV_W)]
    return pl.pallas_call(
        _smid1_kernel,
        grid=(b // bt,),
        in_specs=[tm(nl, CONV_CH), tm(nl, QKV_W), tm(nl, LANES), tm(nl, DN_V), tm(CONV_K - 1, CONV_CH),
                  tm(SHORT_K - 1, QKV_W),
                  _wspec((CONV_PAD, CONV_CH)), _wspec((1, CONV_CH)), _wspec((1, CONV_CH)),
                  _wspec((1, CONV_CH)), _wspec((SHORT_K, QKV_W)), _wspec((1, LANES)),
                  _wspec((1, LANES))],
        out_specs=[tm(r, w) for r, w in tm_outs] + [sm] * 6 + [tm(DN_HEADS, LANES)],
        out_shape=[jax.ShapeDtypeStruct((r, b, w), F32) for r, w in tm_outs] + [sm_shape] * 6
        + [jax.ShapeDtypeStruct((DN_HEADS, b, LANES), F32)],
        compiler_params=_params(("parallel",)),
        name="sample_mid1",
    )(u, qkv, bd, og, hc, hq, wdw, bdw, lcg, lcb, wsh, alog, dtb)


def _smid2_kernel(w_ref, q2_ref, uv_ref, o1_ref, kt_ref, og_ref, egc_ref, s_ref, dng_ref,
                  on_ref, snew_ref):
    nl = w_ref.shape[1]
    for hd in range(DN_HEADS):
        ls = slice(hd * DN_DK, (hd + 1) * DN_DK)
        s_h = s_ref[:, hd]
        lhs = jnp.concatenate([w_ref[:, :, ls], q2_ref[:, :, ls]], axis=1)
        r = jnp.einsum('bck,bkv->bcv', lhs.astype(BF16), s_h.astype(BF16),
                       preferred_element_type=F32)
        u_new = uv_ref[:, :, ls] - r[:, :nl]
        o = o1_ref[:, :, ls] + r[:, nl:]
        upd = jnp.einsum('bck,bcv->bkv', kt_ref[:, :, ls].astype(BF16), u_new.astype(BF16),
                         preferred_element_type=F32)
        snew_ref[:, hd] = egc_ref[:, hd] * s_h + upd
        o = o * lax.rsqrt(jnp.mean(o * o, -1, keepdims=True) + RMS_EPS) * dng_ref[...]
        on_ref[:, :, ls] = o * _silu(og_ref[:, :, ls])


def _smid2(w, q2, uv, o1, kt, og, egc, s, dng):
    b, nl, _ = w.shape
    bs = SM_BS
    sq = pl.BlockSpec((bs, nl, DN_V), lambda i: (i, 0, 0))
    st = pl.BlockSpec((bs, DN_HEADS, DN_DK, DN_DV), lambda i: (i, 0, 0, 0))
    return pl.pallas_call(
        _smid2_kernel,
        grid=(b // bs,),
        in_specs=[sq, sq, sq, sq, sq, sq,
                  pl.BlockSpec((bs, DN_HEADS, 1, LANES), lambda i: (i, 0, 0, 0)), st,
                  _wspec((1, DN_DV))],
        out_specs=[sq, st],
        out_shape=[jax.ShapeDtypeStruct((b, nl, DN_V), F32),
                   jax.ShapeDtypeStruct((b, DN_HEADS, DN_DK, DN_DV), F32)],
        compiler_params=_params(("parallel",)),
        name="sample_mid2",
    )(w, q2, uv, o1, kt, og, egc, s, dng)


def _tm(x):
    return jnp.swapaxes(x, 0, 1)


def kernel(x_prompt, x_sample, mem_prompt, state_conv, state_qkv_conv, state_delta, cache_mem_k, cache_mem_v, w_in, b_glu, w_dw, b_dw, ln_conv_g, ln_conv_b, w_conv_out, w_short, a_log, dt_bias, dn_norm_g, w_dn_out, w_o, ln1_g, ln1_b, w_xq, w_mem_kv, w_xo, ln2_g, ln2_b, w_ff1, w_ff2, ln3_g, ln3_b):
    assert w_in.shape[0] == DEPTH == 1
    bp, lp, _ = x_prompt.shape
    bsm, ls_, _ = x_sample.shape

    w_a, w_g, w_bd = _wprep(w_in, 256)
    lane_pad = lambda a: jnp.concatenate(
        [jnp.zeros((DN_HEADS,), F32), a.astype(F32), jnp.zeros((LANES - 2 * DN_HEADS,), F32)])[None]
    alog = lane_pad(a_log[0])
    dtb = lane_pad(dt_bias[0])
    wdw = jnp.concatenate([w_dw[0], jnp.zeros((CONV_PAD - CONV_K, CONV_CH), F32)], axis=0)
    r2 = lambda a: a[0][None]
    bglu, bdw, lcg, lcb = r2(b_glu), r2(b_dw), r2(ln_conv_g), r2(ln_conv_b)
    dng = r2(dn_norm_g)
    wsh = w_short[0]
    slabs = lambda a: jnp.broadcast_to(
        jnp.swapaxes(a.reshape(a.shape[0], a.shape[1] // LANES, LANES), 0, 1)[:, :, None, :],
        (a.shape[1] // LANES, a.shape[0], 8, LANES))
    wdw_slab = slabs(w_dw[0])
    wsh_slab = slabs(wsh)
    wca, wdn, wo, wxq, wxo = (a[0].astype(BF16) for a in (w_conv_out, w_dn_out, w_o, w_xq, w_xo))
    w1, w2, wkv = w_ff1[0].astype(BF16), w_ff2[0].astype(BF16), w_mem_kv[0].astype(BF16)
    g1, b1, g2, b2, g3, b3 = (r2(a) for a in (ln1_g, ln1_b, ln2_g, ln2_b, ln3_g, ln3_b))

    n_s = bsm * ls_
    xs = _tm(x_sample).reshape(n_s, D_MODEL)
    u, qkv, og, ga, gb, bd = _inproj(xs, w_a, w_g, w_bd, bglu, 256)
    t3 = lambda a: a.reshape(ls_, bsm, a.shape[-1])
    cvn, c_new, q_new, w_, q2, uv, o1, kt, og_s, egc = _smid1(
        t3(u), t3(qkv), t3(bd), t3(og), _tm(state_conv[0]), _tm(state_qkv_conv[0]),
        wdw, bdw, lcg, lcb, wsh, alog, dtb)
    on, s_delta = _smid2(w_, q2, uv, o1, kt, og_s, _tm(egc)[:, :, None, :], state_delta[0], dng)
    h_s, xq_s = _merge(cvn.reshape(n_s, CONV_CH), _tm(on).reshape(n_s, DN_V), ga, gb, xs,
                       wca, wdn, wo, wxq, g1, b1, 512)

    n_p = bp * lp
    xp = x_prompt.reshape(n_p, D_MODEL)
    mk, mv, mk_b, mv_b = _memkv(mem_prompt.reshape(bp * N_MEM, D_MODEL), wkv, 512)
    h, xq, p_conv, p_qkv, p_delta = _pfront(xp, w_a, w_g, w_bd, bglu, wdw_slab, bdw, lcg, lcb, wsh_slab,
                                            alog, dtb, dng, wca, wdn, wo, wxq, g1, b1, bp, lp)
    y_p, att_s = _ptail(xq, mk_b.reshape(bp, N_MEM, D_MODEL), mv_b.reshape(bp, N_MEM, D_MODEL), h,
                        wxo, w1, w2, g2, b2, g3, b3, _tm(t3(xq_s)), cache_mem_k, cache_mem_v, bp, lp, 512)
    y_p = y_p.reshape(bp, lp, D_MODEL)
    p_conv = p_conv[:, CONV_PAD - (CONV_K - 1):]
    p_qkv = p_qkv[:, SHORT_PAD - (SHORT_K - 1):]
    p_mk = mk.reshape(bp, N_MEM, X_HEADS, X_HEAD_DIM)
    p_mv = mv.reshape(bp, N_MEM, X_HEADS, X_HEAD_DIM)

    y_s = _tail(_tm(att_s).reshape(n_s, D_MODEL), h_s, wxo, w1, w2, g2, b2, g3, b3, 512)
    y_s = _tm(t3(y_s))

    return (y_p, y_s, p_conv[None], p_qkv[None], p_delta[None], p_mk[None], p_mv[None],
            _tm(c_new)[None], _tm(q_new)[None], s_delta[None])
```
